```python
import jax, jax.numpy as jnp
from jax import lax
import numpy as np

D_MODEL = 1024
BATCH = 2
SEQ = 8192
DEPTH = 1

CHUNK = 64
D_MIX = D_MODEL
D_GMLP = D_MIX // 2
D_HGRN = D_MIX - D_GMLP
GMLP_HEADS = 4
GMLP_HEAD_DIM = D_GMLP // GMLP_HEADS
GMLP_BLOCK = 128
HGRN_HEADS = 4
HGRN_HEAD_DIM = D_HGRN // HGRN_HEADS
D_FF = -(-(8 * D_MODEL) // (3 * 256)) * 256
N_ADA = 6
D_IN = 2 * D_GMLP + 4 * D_HGRN
EPS = 1e-6

kernel_name = "hybrid_gmlp_hgrn2_adaln_block"


def rmsnorm(x, w):
    xf = x.astype(jnp.float32)
    y = xf * lax.rsqrt(jnp.mean(xf * xf, axis=-1, keepdims=True) + EPS)
    return (y * w.astype(jnp.float32)).astype(x.dtype)


def layernorm(x, w, b):
    xf = x.astype(jnp.float32)
    mu = jnp.mean(xf, axis=-1, keepdims=True)
    var = jnp.mean(jnp.square(xf - mu), axis=-1, keepdims=True)
    y = (xf - mu) * lax.rsqrt(var + EPS)
    return (y * w.astype(jnp.float32) + b.astype(jnp.float32)).astype(x.dtype)


def modulate(h, shift, scale):
    return h * (1 + scale[:, None, :]) + shift[:, None, :]


def gmlp_spatial_gating(u, v, w_s, b_s, ln_w, ln_b):
    bsz, seq, _ = u.shape
    nb = seq // GMLP_BLOCK
    u = jax.nn.gelu(u, approximate=False)
    v = layernorm(jax.nn.gelu(v, approximate=False), ln_w, ln_b)
    vb = v.reshape(bsz, nb, GMLP_BLOCK, GMLP_HEADS, GMLP_HEAD_DIM)
    cid = jnp.arange(GMLP_BLOCK) // CHUNK
    mask = cid[:, None] >= cid[None, :]
    ws = jnp.where(mask[None], w_s, 0).astype(v.dtype)
    mixed = jnp.einsum('hts,bnshc->bnthc', ws, vb) + b_s.T.astype(v.dtype)[None, None, :, :, None]
    return u * mixed.reshape(bsz, seq, D_GMLP)


def hgrn2_recurrence(q, f_logit, inp, g, lb, gn_w):
    dtype = q.dtype
    bsz, seq, _ = q.shape
    nc = seq // CHUNK
    qf = jax.nn.silu(q.astype(jnp.float32))
    f = lb + (1.0 - lb) * jax.nn.sigmoid(f_logit.astype(jnp.float32))
    logf = jnp.log(f)
    k = 1.0 - f
    vf = inp.astype(jnp.float32)

    def to_chunks(t):
        return t.reshape(bsz, nc, CHUNK, HGRN_HEADS, HGRN_HEAD_DIM).transpose(1, 0, 3, 2, 4)

    tri = jnp.arange(CHUNK)[:, None] >= jnp.arange(CHUNK)[None, :]

    def step(state, xs):
        qc, kc, vc, lc = xs
        b = jnp.cumsum(lc, axis=2)
        inter = jnp.einsum('bhtd,bhde->bhte', qc * jnp.exp(b), state)
        diff = b[:, :, :, None, :] - b[:, :, None, :, :]
        decay = jnp.where(tri[:, :, None], jnp.exp(jnp.minimum(diff, 0.0)), 0.0)
        attn = jnp.einsum('bhtd,bhsd,bhtsd->bhts', qc, kc, decay)
        intra = jnp.einsum('bhts,bhse->bhte', attn, vc)
        b_last = b[:, :, -1, :]
        state = state * jnp.exp(b_last)[..., None] + jnp.einsum(
            'bhsd,bhse->bhde', kc * jnp.exp(b_last[:, :, None, :] - b), vc)
        return state, inter + intra

    s0 = jnp.zeros((bsz, HGRN_HEADS, HGRN_HEAD_DIM, HGRN_HEAD_DIM), jnp.float32)
    _, o = lax.scan(step, s0, (to_chunks(qf), to_chunks(k), to_chunks(vf), to_chunks(logf)))
    o = o.transpose(1, 0, 3, 2, 4).reshape(bsz, seq, HGRN_HEADS, HGRN_HEAD_DIM)
    gate = jax.nn.silu(g.astype(jnp.float32)).reshape(bsz, seq, HGRN_HEADS, HGRN_HEAD_DIM)
    o = rmsnorm(o, gn_w) * gate
    return o.reshape(bsz, seq, D_HGRN).astype(dtype)


def setup_inputs(seed: int = 0) -> dict:
    key = jax.random.key(seed)
    ks = jax.random.split(key, 20)
    nrm = lambda k, shape, s: jax.random.normal(k, shape, jnp.float32) * s
    return {
        "x": nrm(ks[0], (BATCH, SEQ, D_MODEL), 1.0),
        "c": nrm(ks[1], (BATCH, D_MODEL), 1.0),
        "w_ada": nrm(ks[2], (DEPTH, D_MODEL, N_ADA * D_MODEL), 0.5 * D_MODEL ** -0.5),
        "b_ada": nrm(ks[3], (DEPTH, N_ADA * D_MODEL), 0.02),
        "norm1_w": 1.0 + nrm(ks[4], (DEPTH, D_MODEL), 0.02),
        "w_in": nrm(ks[5], (DEPTH, D_MODEL, D_IN), D_MODEL ** -0.5),
        "w_s": nrm(ks[6], (DEPTH, GMLP_HEADS, GMLP_BLOCK, GMLP_BLOCK), GMLP_BLOCK ** -0.5),
        "b_s": 1.0 + nrm(ks[7], (DEPTH, GMLP_HEADS, GMLP_BLOCK), 0.02),
        "v_ln_w": 1.0 + nrm(ks[8], (DEPTH, D_GMLP), 0.02),
        "v_ln_b": nrm(ks[9], (DEPTH, D_GMLP), 0.02),
        "lower_bounds": nrm(ks[10], (DEPTH + 1, D_HGRN), 0.5),
        "gn_w": 1.0 + nrm(ks[11], (DEPTH, HGRN_HEAD_DIM), 0.02),
        "w_out": nrm(ks[12], (DEPTH, D_MIX, D_MODEL), D_MIX ** -0.5),
        "norm2_w": 1.0 + nrm(ks[13], (DEPTH, D_MODEL), 0.02),
        "w_ffn_in": nrm(ks[14], (DEPTH, D_MODEL, 2 * D_FF), D_MODEL ** -0.5),
        "w_ffn_out": nrm(ks[15], (DEPTH, D_FF, D_MODEL), D_FF ** -0.5),
        "final_norm_w": 1.0 + nrm(ks[16], (D_MODEL,), 0.02),
    }


def reference(x, c, w_ada, b_ada, norm1_w, w_in, w_s, b_s, v_ln_w, v_ln_b,
              lower_bounds, gn_w, w_out, norm2_w, w_ffn_in, w_ffn_out, final_norm_w):
    lb_all = jnp.cumsum(jax.nn.softmax(lower_bounds.astype(jnp.float32), axis=0), axis=0)
    c_act = jax.nn.silu(c)
    split_at = [D_GMLP, 2 * D_GMLP, 2 * D_GMLP + D_HGRN,
                2 * D_GMLP + 2 * D_HGRN, 2 * D_GMLP + 3 * D_HGRN]
    for l in range(DEPTH):
        ada = (c_act @ w_ada[l] + b_ada[l]).astype(x.dtype)
        sh1, sc1, g1, sh2, sc2, g2 = jnp.split(ada, N_ADA, axis=-1)

        h = modulate(rmsnorm(x, norm1_w[l]), sh1, sc1)
        proj = h @ w_in[l]
        u, v, q, f_logit, inp, g = jnp.split(proj, split_at, axis=-1)
        y_a = gmlp_spatial_gating(u, v, w_s[l], b_s[l], v_ln_w[l], v_ln_b[l])
        y_b = hgrn2_recurrence(q, f_logit, inp, g, lb_all[l], gn_w[l])
        mix = jnp.concatenate([y_a, y_b], axis=-1) @ w_out[l]
        x = x + g1[:, None, :] * mix

        h = modulate(rmsnorm(x, norm2_w[l]), sh2, sc2)
        gate, up = jnp.split(h @ w_ffn_in[l], 2, axis=-1)
        x = x + g2[:, None, :] * ((jax.nn.silu(gate) * up) @ w_ffn_out[l])
    return rmsnorm(x, final_norm_w)
```

```python
import functools

import numpy as np
import jax
import jax.numpy as jnp
from jax import lax
from jax.experimental import pallas as pl
from jax.experimental.pallas import tpu as pltpu

D_MODEL = 1024
D_GMLP = 512
D_HGRN = 512
HEADS = 4
HEAD_DIM = 128
BLOCK = 128
GATE_CHUNK = 64
D_IN = 2 * D_GMLP + 4 * D_HGRN
D_FF = 2816
N_ADA = 6
EPS = 1e-6

LEVELS = (128, 64, 32, 16, 8, 4, 2)
DIAG_LEVEL = len(LEVELS)

MIX_TILE = 256
FFN_TILE = 256
VMEM_LIMIT = 56 * 1024 * 1024

_F32 = jnp.float32
_BF16 = jnp.bfloat16


def _dot(a, b):
    return jnp.dot(a, b, preferred_element_type=_F32)


def _dot_nt(a, b):
    return lax.dot_general(a, b, (((1,), (1,)), ((), ())), preferred_element_type=_F32)


def _dot_tn(a, b):
    return lax.dot_general(a, b, (((0,), (0,)), ((), ())), preferred_element_type=_F32)


def _silu(x):
    return x * jax.nn.sigmoid(x)


def _gelu(x):
    return 0.5 * x * (1.0 + lax.erf(x * np.float32(1.0 / np.sqrt(2.0))))


def _level_table():
    t = np.arange(BLOCK)[:, None]
    s = np.arange(BLOCK)[None, :]
    table = np.full((BLOCK, BLOCK), -1, np.int32)
    table[t == s] = DIAG_LEVEL
    for idx, n in enumerate(LEVELS):
        h = n // 2
        same = (t // n) == (s // n)
        table[same & ((t % n) >= h) & ((s % n) < h)] = idx
    return table


def _ada_kernel(c_ref, w_ref, b_ref, o_ref):
    c = c_ref[...]
    c_act = _silu(c).astype(_BF16)
    o_ref[...] = _dot(c_act, w_ref[...].astype(_BF16)) + b_ref[...]


def _ada_call(c_pad, w_ada, b_ada):
    n_out = w_ada.shape[1]
    tn = 1536
    return pl.pallas_call(
        _ada_kernel,
        grid=(n_out // tn,),
        in_specs=[
            pl.BlockSpec((c_pad.shape[0], D_MODEL), lambda j: (0, 0)),
            pl.BlockSpec((D_MODEL, tn), lambda j: (0, j)),
            pl.BlockSpec((1, tn), lambda j: (0, j)),
        ],
        out_specs=pl.BlockSpec((c_pad.shape[0], tn), lambda j: (0, j)),
        out_shape=jax.ShapeDtypeStruct((c_pad.shape[0], n_out), _F32),
        compiler_params=pltpu.CompilerParams(dimension_semantics=("arbitrary",)),
        name="ada",
    )(c_pad, w_ada, b_ada)


def _decay_factor(level_n, b_sc, f_h, row, col0):
    h = level_n // 2
    if level_n == 2:
        return jnp.where((row % 2) == 1, f_h, 1.0)
    if level_n == 4:
        f_next = pltpu.roll(f_h, BLOCK - 1, axis=0)
        f_prev = pltpu.roll(f_h, 1, axis=0)
        pos = row % 4
        return jnp.where(pos == 0, f_next,
                         jnp.where(pos == 1, 1.0,
                                   jnp.where(pos == 2, f_h, f_h * f_prev)))
    pieces = []
    for blk in range(BLOCK // level_n):
        r = blk * level_n + h - 1
        ref_row = b_sc[pl.ds(r, 1), pl.ds(col0, HEAD_DIM)]
        pieces.append(jnp.broadcast_to(ref_row, (level_n, HEAD_DIM)))
    b_ref = pieces[0] if len(pieces) == 1 else jnp.concatenate(pieces, axis=0)
    b_h = b_sc[:, pl.ds(col0, HEAD_DIM)]
    return jnp.exp(-jnp.abs(b_h - b_ref))


def _mixer_kernel(x_ref, ada_ref, n1w_ref, w_in_ref, ws_ref, bs_ref, lnw_ref, lnb_ref,
                  lbraw_ref, gnw_ref, w_out_ref, tri_ref, lvl_ref,
                  o_ref, state_sc, proj_sc, y_sc, b_sc, *, n_blocks):
    @pl.when(pl.program_id(1) == 0)
    def _():
        state_sc[...] = jnp.zeros_like(state_sc)

    x = x_ref[0]
    ada = ada_ref[0]
    sh1, sc1, g1 = ada[0:1], ada[1:2], ada[2:3]

    ms = jnp.mean(x * x, axis=-1, keepdims=True)
    hmod = x * lax.rsqrt(ms + EPS) * n1w_ref[...]
    hmod = hmod * (1.0 + sc1) + sh1
    proj_sc[...] = _dot(hmod.astype(_BF16), w_in_ref[...])

    lbraw = lbraw_ref[...]
    lbe = jnp.exp(lbraw - jnp.max(lbraw, axis=0, keepdims=True))
    lb = (lbe / jnp.sum(lbe, axis=0, keepdims=True))[0:1]

    row = lax.broadcasted_iota(jnp.int32, (BLOCK, HEAD_DIM), 0)
    col = lax.broadcasted_iota(jnp.int32, (BLOCK, BLOCK), 1)
    rowb = lax.broadcasted_iota(jnp.int32, (BLOCK, BLOCK), 0)
    gate_mask = (rowb // GATE_CHUNK) >= (col // GATE_CHUNK)
    lvl = lvl_ref[...]

    def block_body(i, carry):
        r0 = pl.multiple_of(i * BLOCK, BLOCK)
        rows = pl.ds(r0, BLOCK)

        u = proj_sc[rows, pl.ds(0, D_GMLP)]
        v = proj_sc[rows, pl.ds(D_GMLP, D_GMLP)]
        gv = _gelu(v)
        mu = jnp.mean(gv, axis=-1, keepdims=True)
        cen = gv - mu
        var = jnp.mean(cen * cen, axis=-1, keepdims=True)
        vn = (cen * lax.rsqrt(var + EPS) * lnw_ref[...] + lnb_ref[...]).astype(_BF16)
        mixed = []
        for hd in range(HEADS):
            wsm = jnp.where(gate_mask, ws_ref[hd], 0.0).astype(_BF16)
            mixed.append(_dot(wsm, vn[:, hd * HEAD_DIM:(hd + 1) * HEAD_DIM]) + bs_ref[hd])
        y_a = _gelu(u) * jnp.concatenate(mixed, axis=-1)
        y_sc[rows, pl.ds(0, D_GMLP)] = y_a.astype(_BF16)

        o0 = 2 * D_GMLP
        fl = proj_sc[rows, pl.ds(o0 + D_HGRN, D_HGRN)]
        f = lb + (1.0 - lb) * jax.nn.sigmoid(fl)
        logf = jnp.log(f)
        p_hi = logf.astype(_BF16)
        rem = logf - p_hi.astype(_F32)
        p_mid = rem.astype(_BF16)
        p_lo = (rem - p_mid.astype(_F32)).astype(_BF16)
        tri = tri_ref[...]
        b_sc[...] = (_dot(tri, p_hi) + _dot(tri, p_mid)) + _dot(tri, p_lo)

        for hd in range(HEADS):
            c0 = hd * HEAD_DIM
            q_h = _silu(proj_sc[rows, pl.ds(o0 + c0, HEAD_DIM)])
            f_h = f[:, c0:c0 + HEAD_DIM]
            k_h = 1.0 - f_h
            v_h = proj_sc[rows, pl.ds(o0 + 2 * D_HGRN + c0, HEAD_DIM)]
            g_h = proj_sc[rows, pl.ds(o0 + 3 * D_HGRN + c0, HEAD_DIM)]
            b_h = b_sc[:, pl.ds(c0, HEAD_DIM)]

            attn = jnp.zeros((BLOCK, BLOCK), _F32)
            for idx, n in enumerate(LEVELS):
                e = _decay_factor(n, b_sc, f_h, row, c0)
                upper = (row % n) >= (n // 2)
                q_t = jnp.where(upper, q_h * e, 0.0).astype(_BF16)
                k_t = jnp.where(upper, 0.0, k_h * e).astype(_BF16)
                attn = jnp.where(lvl == idx, _dot_nt(q_t, k_t), attn)
            diag = jnp.sum(q_h * k_h, axis=-1, keepdims=True)

            s_t = state_sc[hd]
            q_hat = (q_h * jnp.exp(b_h)).astype(_BF16)
            v_bf = v_h.astype(_BF16)
            o_h = _dot_nt(q_hat, s_t.astype(_BF16)) + _dot(attn.astype(_BF16), v_bf) + diag * v_h

            b_last = b_sc[pl.ds(BLOCK - 1, 1), pl.ds(c0, HEAD_DIM)]
            k_hat = (k_h * jnp.exp(b_last - b_h)).astype(_BF16)
            state_sc[hd] = s_t * jnp.exp(b_last) + _dot_tn(v_bf, k_hat)

            oms = jnp.mean(o_h * o_h, axis=-1, keepdims=True)
            y_b = o_h * lax.rsqrt(oms + EPS) * gnw_ref[...] * _silu(g_h)
            y_sc[rows, pl.ds(D_GMLP + c0, HEAD_DIM)] = y_b.astype(_BF16)
        return carry

    lax.fori_loop(0, n_blocks, block_body, 0)

    mix = _dot(y_sc[...], w_out_ref[...])
    o_ref[0] = x + g1 * mix


def _mixer_call(x, ada, n1w, w_in, w_s, b_s, lnw, lnb, lbraw, gnw, w_out, tri, lvl):
    bsz, seq, _ = x.shape
    n_blocks = MIX_TILE // BLOCK
    const2 = lambda b, s: (0, 0)
    const3 = lambda b, s: (0, 0, 0)
    kernel = functools.partial(_mixer_kernel, n_blocks=n_blocks)
    return pl.pallas_call(
        kernel,
        grid=(bsz, seq // MIX_TILE),
        in_specs=[
            pl.BlockSpec((1, MIX_TILE, D_MODEL), lambda b, s: (b, s, 0)),
            pl.BlockSpec((1, N_ADA, D_MODEL), lambda b, s: (b, 0, 0)),
            pl.BlockSpec((1, D_MODEL), const2),
            pl.BlockSpec((D_MODEL, D_IN), const2),
            pl.BlockSpec((HEADS, BLOCK, BLOCK), const3),
            pl.BlockSpec((HEADS, BLOCK, 1), const3),
            pl.BlockSpec((1, D_GMLP), const2),
            pl.BlockSpec((1, D_GMLP), const2),
            pl.BlockSpec((2, D_HGRN), const2),
            pl.BlockSpec((1, HEAD_DIM), const2),
            pl.BlockSpec((D_MODEL, D_MODEL), const2),
            pl.BlockSpec((BLOCK, BLOCK), const2),
            pl.BlockSpec((BLOCK, BLOCK), const2),
        ],
        out_specs=pl.BlockSpec((1, MIX_TILE, D_MODEL), lambda b, s: (b, s, 0)),
        out_shape=jax.ShapeDtypeStruct(x.shape, _F32),
        scratch_shapes=[
            pltpu.VMEM((HEADS, HEAD_DIM, HEAD_DIM), _F32),
            pltpu.VMEM((MIX_TILE, D_IN), _F32),
            pltpu.VMEM((MIX_TILE, D_MODEL), _BF16),
            pltpu.VMEM((BLOCK, D_HGRN), _F32),
        ],
        compiler_params=pltpu.CompilerParams(
            dimension_semantics=("arbitrary", "arbitrary"),
            vmem_limit_bytes=VMEM_LIMIT),
        name="mixer",
    )(x, ada, n1w, w_in, w_s, b_s, lnw, lnb, lbraw, gnw, w_out, tri, lvl)


def _ffn_kernel(x_ref, ada_ref, n2w_ref, w1_ref, w2_ref, fnw_ref, o_ref):
    x = x_ref[0]
    ada = ada_ref[0]
    sh2, sc2, g2 = ada[3:4], ada[4:5], ada[5:6]
    ms = jnp.mean(x * x, axis=-1, keepdims=True)
    hmod = x * lax.rsqrt(ms + EPS) * n2w_ref[...]
    hmod = (hmod * (1.0 + sc2) + sh2).astype(_BF16)
    gu = _dot(hmod, w1_ref[...])
    act = (_silu(gu[:, :D_FF]) * gu[:, D_FF:]).astype(_BF16)
    x2 = x + g2 * _dot(act, w2_ref[...])
    ms2 = jnp.mean(x2 * x2, axis=-1, keepdims=True)
    o_ref[0] = x2 * lax.rsqrt(ms2 + EPS) * fnw_ref[...]


def _ffn_call(x, ada, n2w, w1, w2, fnw):
    bsz, seq, _ = x.shape
    const2 = lambda b, s: (0, 0)
    return pl.pallas_call(
        _ffn_kernel,
        grid=(bsz, seq // FFN_TILE),
        in_specs=[
            pl.BlockSpec((1, FFN_TILE, D_MODEL), lambda b, s: (b, s, 0)),
            pl.BlockSpec((1, N_ADA, D_MODEL), lambda b, s: (b, 0, 0)),
            pl.BlockSpec((1, D_MODEL), const2),
            pl.BlockSpec((D_MODEL, 2 * D_FF), const2),
            pl.BlockSpec((D_FF, D_MODEL), const2),
            pl.BlockSpec((1, D_MODEL), const2),
        ],
        out_specs=pl.BlockSpec((1, FFN_TILE, D_MODEL), lambda b, s: (b, s, 0)),
        out_shape=jax.ShapeDtypeStruct(x.shape, _F32),
        compiler_params=pltpu.CompilerParams(
            dimension_semantics=("arbitrary", "arbitrary"),
            vmem_limit_bytes=VMEM_LIMIT),
        name="ffn",
    )(x, ada, n2w, w1, w2, fnw)


def kernel(x, c, w_ada, b_ada, norm1_w, w_in, w_s, b_s, v_ln_w, v_ln_b, lower_bounds,
           gn_w, w_out, norm2_w, w_ffn_in, w_ffn_out, final_norm_w):
    bsz = x.shape[0]
    depth = w_in.shape[0]
    assert depth == 1 and lower_bounds.shape[0] == 2

    tri = jnp.asarray(np.tril(np.ones((BLOCK, BLOCK), np.float32)), _BF16)
    lvl = jnp.asarray(_level_table())

    c_pad = jnp.zeros((8, D_MODEL), _F32).at[:bsz].set(c)
    for l in range(depth):
        ada = _ada_call(c_pad, w_ada[l], b_ada[l][None, :])[:bsz]
        ada = ada.reshape(bsz, N_ADA, D_MODEL)
        x = _mixer_call(
            x, ada, norm1_w[l][None, :], w_in[l].astype(_BF16), w_s[l],
            b_s[l][:, :, None], v_ln_w[l][None, :], v_ln_b[l][None, :],
            lower_bounds, gn_w[l][None, :], w_out[l].astype(_BF16), tri, lvl)
        x = _ffn_call(x, ada, norm2_w[l][None, :], w_ffn_in[l].astype(_BF16),
                      w_ffn_out[l].astype(_BF16), final_norm_w[None, :])
    return x
```

```python
import functools

import numpy as np
import jax
import jax.numpy as jnp
from jax import lax
from jax.experimental import pallas as pl
from jax.experimental.pallas import tpu as pltpu

D_MODEL = 1024
D_GMLP = 512
D_HGRN = 512
HEADS = 4
HEAD_DIM = 128
BLOCK = 128
GATE_CHUNK = 64
D_IN = 2 * D_GMLP + 4 * D_HGRN
D_FF = 2816
N_ADA = 6
EPS = 1e-6

LEVELS = (128, 64, 32, 16, 8, 4, 2)
DIAG_LEVEL = len(LEVELS)

MIX_TILE = 256
FFN_TILE = 256
VMEM_LIMIT = 56 * 1024 * 1024

_F32 = jnp.float32
_BF16 = jnp.bfloat16


def _dot(a, b):
    return jnp.dot(a, b, preferred_element_type=_F32)


def _dot_nt(a, b):
    return lax.dot_general(a, b, (((1,), (1,)), ((), ())), preferred_element_type=_F32)


def _dot_tn(a, b):
    return lax.dot_general(a, b, (((0,), (0,)), ((), ())), preferred_element_type=_F32)


def _silu(x):
    return x * jax.nn.sigmoid(x)


def _gelu(x):
    return 0.5 * x * (1.0 + lax.erf(x * np.float32(1.0 / np.sqrt(2.0))))


def _level_table():
    t = np.arange(BLOCK)[:, None]
    s = np.arange(BLOCK)[None, :]
    table = np.full((BLOCK, BLOCK), -1, np.int32)
    table[t == s] = DIAG_LEVEL
    for idx, n in enumerate(LEVELS):
        h = n // 2
        same = (t // n) == (s // n)
        table[same & ((t % n) >= h) & ((s % n) < h)] = idx
    return table


def _ada_kernel(c_ref, w_ref, b_ref, o_ref):
    c = c_ref[...]
    c_act = _silu(c).astype(_BF16)
    o_ref[...] = _dot(c_act, w_ref[...].astype(_BF16)) + b_ref[...]


def _ada_call(c_pad, w_ada, b_ada):
    n_out = w_ada.shape[1]
    tn = 1536
    return pl.pallas_call(
        _ada_kernel,
        grid=(n_out // tn,),
        in_specs=[
            pl.BlockSpec((c_pad.shape[0], D_MODEL), lambda j: (0, 0)),
            pl.BlockSpec((D_MODEL, tn), lambda j: (0, j)),
            pl.BlockSpec((1, tn), lambda j: (0, j)),
        ],
        out_specs=pl.BlockSpec((c_pad.shape[0], tn), lambda j: (0, j)),
        out_shape=jax.ShapeDtypeStruct((c_pad.shape[0], n_out), _F32),
        compiler_params=pltpu.CompilerParams(dimension_semantics=("arbitrary",)),
        name="ada",
    )(c_pad, w_ada, b_ada)


def _neg_abs(x):
    bits = pltpu.bitcast(x, jnp.uint32) | jnp.uint32(0x80000000)
    return pltpu.bitcast(bits, _F32)


def _decay_factor(level_n, b_sc, f_h, row, col0):
    h = level_n // 2
    if level_n == 2:
        return jnp.where((row % 2) == 1, f_h, 1.0)
    if level_n == 4:
        f_next = pltpu.roll(f_h, BLOCK - 1, axis=0)
        f_prev = pltpu.roll(f_h, 1, axis=0)
        pos = row % 4
        return jnp.where(pos == 0, f_next,
                         jnp.where(pos == 1, 1.0,
                                   jnp.where(pos == 2, f_h, f_h * f_prev)))
    pieces = []
    for blk in range(BLOCK // level_n):
        r = blk * level_n + h - 1
        ref_row = b_sc[pl.ds(r, 1), pl.ds(col0, HEAD_DIM)]
        pieces.append(jnp.broadcast_to(ref_row, (level_n, HEAD_DIM)))
    b_ref = pieces[0] if len(pieces) == 1 else jnp.concatenate(pieces, axis=0)
    b_h = b_sc[:, pl.ds(col0, HEAD_DIM)]
    return jnp.exp(_neg_abs(b_h - b_ref))


def _mixer_kernel(x_ref, ada_ref, n1w_ref, w_in_ref, ws_ref, bs_ref, lnw_ref, lnb_ref,
                  lbraw_ref, gnw_ref, w_out_ref, tri_ref, lvl_ref,
                  o_ref, state_sc, proj_sc, y_sc, b_sc, *, n_blocks):
    @pl.when(pl.program_id(1) == 0)
    def _():
        state_sc[...] = jnp.zeros_like(state_sc)

    x = x_ref[0]
    ada = ada_ref[0]
    sh1, sc1, g1 = ada[0:1], ada[1:2], ada[2:3]

    ms = jnp.mean(x * x, axis=-1, keepdims=True)
    hmod = x * lax.rsqrt(ms + EPS) * n1w_ref[...]
    hmod = hmod * (1.0 + sc1) + sh1
    proj_sc[...] = _dot(hmod.astype(_BF16), w_in_ref[...])

    lbraw = lbraw_ref[...]
    lbe = jnp.exp(lbraw - jnp.max(lbraw, axis=0, keepdims=True))
    lb = (lbe / jnp.sum(lbe, axis=0, keepdims=True))[0:1]

    row = lax.broadcasted_iota(jnp.int32, (BLOCK, HEAD_DIM), 0)
    col = lax.broadcasted_iota(jnp.int32, (BLOCK, BLOCK), 1)
    rowb = lax.broadcasted_iota(jnp.int32, (BLOCK, BLOCK), 0)
    gate_mask = (rowb // GATE_CHUNK) >= (col // GATE_CHUNK)
    ws_masked = [jnp.where(gate_mask, ws_ref[hd], 0.0).astype(_BF16) for hd in range(HEADS)]
    lvl = lvl_ref[...]
    level_is = [lvl == idx for idx in range(len(LEVELS))]
    upper_rows = [(row % n) >= (n // 2) for n in LEVELS]

    def block_body(i):
        rows = pl.ds(i * BLOCK, BLOCK)

        u = proj_sc[rows, pl.ds(0, D_GMLP)]
        v = proj_sc[rows, pl.ds(D_GMLP, D_GMLP)]
        gv = _gelu(v)
        mu = jnp.mean(gv, axis=-1, keepdims=True)
        cen = gv - mu
        var = jnp.mean(cen * cen, axis=-1, keepdims=True)
        vn = (cen * lax.rsqrt(var + EPS) * lnw_ref[...] + lnb_ref[...]).astype(_BF16)
        mixed = []
        for hd in range(HEADS):
            mixed.append(_dot(ws_masked[hd], vn[:, hd * HEAD_DIM:(hd + 1) * HEAD_DIM]) + bs_ref[hd])
        y_a = _gelu(u) * jnp.concatenate(mixed, axis=-1)
        y_sc[rows, pl.ds(0, D_GMLP)] = y_a.astype(_BF16)

        o0 = 2 * D_GMLP
        fl = proj_sc[rows, pl.ds(o0 + D_HGRN, D_HGRN)]
        f = lb + (1.0 - lb) * jax.nn.sigmoid(fl)
        logf = jnp.log(f)
        p_hi = logf.astype(_BF16)
        rem = logf - p_hi.astype(_F32)
        p_mid = rem.astype(_BF16)
        p_lo = (rem - p_mid.astype(_F32)).astype(_BF16)
        tri = tri_ref[...]
        b_sc[...] = (_dot(tri, p_hi) + _dot(tri, p_mid)) + _dot(tri, p_lo)

        for hd in range(HEADS):
            c0 = hd * HEAD_DIM
            q_h = _silu(proj_sc[rows, pl.ds(o0 + c0, HEAD_DIM)])
            f_h = f[:, c0:c0 + HEAD_DIM]
            k_h = 1.0 - f_h
            v_h = proj_sc[rows, pl.ds(o0 + 2 * D_HGRN + c0, HEAD_DIM)]
            g_h = proj_sc[rows, pl.ds(o0 + 3 * D_HGRN + c0, HEAD_DIM)]
            b_h = b_sc[:, pl.ds(c0, HEAD_DIM)]

            attn = jnp.zeros((BLOCK, BLOCK), _F32)
            for idx, n in enumerate(LEVELS):
                e = _decay_factor(n, b_sc, f_h, row, c0)
                z = (jnp.where(upper_rows[idx], q_h, k_h) * e).astype(_BF16)
                attn = jnp.where(level_is[idx], _dot_nt(z, z), attn)
            diag = jnp.sum(q_h * k_h, axis=-1, keepdims=True)

            s_t = state_sc[hd]
            q_hat = (q_h * jnp.exp(b_h)).astype(_BF16)
            v_bf = v_h.astype(_BF16)
            o_h = _dot_nt(q_hat, s_t.astype(_BF16)) + _dot(attn.astype(_BF16), v_bf) + diag * v_h

            b_last = b_sc[pl.ds(BLOCK - 1, 1), pl.ds(c0, HEAD_DIM)]
            k_hat = (k_h * jnp.exp(b_last - b_h)).astype(_BF16)
            state_sc[hd] = s_t * jnp.exp(b_last) + _dot_tn(v_bf, k_hat)

            oms = jnp.mean(o_h * o_h, axis=-1, keepdims=True)
            y_b = o_h * lax.rsqrt(oms + EPS) * gnw_ref[...] * _silu(g_h)
            y_sc[rows, pl.ds(D_GMLP + c0, HEAD_DIM)] = y_b.astype(_BF16)

    for i in range(n_blocks):
        block_body(i)

    mix = _dot(y_sc[...], w_out_ref[...])
    o_ref[0] = x + g1 * mix


def _mixer_call(x, ada, n1w, w_in, w_s, b_s, lnw, lnb, lbraw, gnw, w_out, tri, lvl):
    bsz, seq, _ = x.shape
    n_blocks = MIX_TILE // BLOCK
    const2 = lambda b, s: (0, 0)
    const3 = lambda b, s: (0, 0, 0)
    kernel = functools.partial(_mixer_kernel, n_blocks=n_blocks)
    return pl.pallas_call(
        kernel,
        grid=(bsz, seq // MIX_TILE),
        in_specs=[
            pl.BlockSpec((1, MIX_TILE, D_MODEL), lambda b, s: (b, s, 0)),
            pl.BlockSpec((1, N_ADA, D_MODEL), lambda b, s: (b, 0, 0)),
            pl.BlockSpec((1, D_MODEL), const2),
            pl.BlockSpec((D_MODEL, D_IN), const2),
            pl.BlockSpec((HEADS, BLOCK, BLOCK), const3),
            pl.BlockSpec((HEADS, BLOCK, 1), const3),
            pl.BlockSpec((1, D_GMLP), const2),
            pl.BlockSpec((1, D_GMLP), const2),
            pl.BlockSpec((2, D_HGRN), const2),
            pl.BlockSpec((1, HEAD_DIM), const2),
            pl.BlockSpec((D_MODEL, D_MODEL), const2),
            pl.BlockSpec((BLOCK, BLOCK), const2),
            pl.BlockSpec((BLOCK, BLOCK), const2),
        ],
        out_specs=pl.BlockSpec((1, MIX_TILE, D_MODEL), lambda b, s: (b, s, 0)),
        out_shape=jax.ShapeDtypeStruct(x.shape, _F32),
        scratch_shapes=[
            pltpu.VMEM((HEADS, HEAD_DIM, HEAD_DIM), _F32),
            pltpu.VMEM((MIX_TILE, D_IN), _F32),
            pltpu.VMEM((MIX_TILE, D_MODEL), _BF16),
            pltpu.VMEM((BLOCK, D_HGRN), _F32),
        ],
        compiler_params=pltpu.CompilerParams(
            dimension_semantics=("arbitrary", "arbitrary"),
            vmem_limit_bytes=VMEM_LIMIT),
        name="mixer",
    )(x, ada, n1w, w_in, w_s, b_s, lnw, lnb, lbraw, gnw, w_out, tri, lvl)


def _ffn_kernel(x_ref, ada_ref, n2w_ref, w1_ref, w2_ref, fnw_ref, o_ref):
    x = x_ref[0]
    ada = ada_ref[0]
    sh2, sc2, g2 = ada[3:4], ada[4:5], ada[5:6]
    ms = jnp.mean(x * x, axis=-1, keepdims=True)
    hmod = x * lax.rsqrt(ms + EPS) * n2w_ref[...]
    hmod = (hmod * (1.0 + sc2) + sh2).astype(_BF16)
    gu = _dot(hmod, w1_ref[...])
    act = (_silu(gu[:, :D_FF]) * gu[:, D_FF:]).astype(_BF16)
    x2 = x + g2 * _dot(act, w2_ref[...])
    ms2 = jnp.mean(x2 * x2, axis=-1, keepdims=True)
    o_ref[0] = x2 * lax.rsqrt(ms2 + EPS) * fnw_ref[...]


def _ffn_call(x, ada, n2w, w1, w2, fnw):
    bsz, seq, _ = x.shape
    const2 = lambda b, s: (0, 0)
    return pl.pallas_call(
        _ffn_kernel,
        grid=(bsz, seq // FFN_TILE),
        in_specs=[
            pl.BlockSpec((1, FFN_TILE, D_MODEL), lambda b, s: (b, s, 0)),
            pl.BlockSpec((1, N_ADA, D_MODEL), lambda b, s: (b, 0, 0)),
            pl.BlockSpec((1, D_MODEL), const2),
            pl.BlockSpec((D_MODEL, 2 * D_FF), const2),
            pl.BlockSpec((D_FF, D_MODEL), const2),
            pl.BlockSpec((1, D_MODEL), const2),
        ],
        out_specs=pl.BlockSpec((1, FFN_TILE, D_MODEL), lambda b, s: (b, s, 0)),
        out_shape=jax.ShapeDtypeStruct(x.shape, _F32),
        compiler_params=pltpu.CompilerParams(
            dimension_semantics=("arbitrary", "arbitrary"),
            vmem_limit_bytes=VMEM_LIMIT),
        name="ffn",
    )(x, ada, n2w, w1, w2, fnw)


def kernel(x, c, w_ada, b_ada, norm1_w, w_in, w_s, b_s, v_ln_w, v_ln_b, lower_bounds,
           gn_w, w_out, norm2_w, w_ffn_in, w_ffn_out, final_norm_w):
    bsz = x.shape[0]
    depth = w_in.shape[0]
    assert depth == 1 and lower_bounds.shape[0] == 2

    tri = jnp.asarray(np.tril(np.ones((BLOCK, BLOCK), np.float32)), _BF16)
    lvl = jnp.asarray(_level_table())

    c_pad = jnp.zeros((8, D_MODEL), _F32).at[:bsz].set(c)
    for l in range(depth):
        ada = _ada_call(c_pad, w_ada[l], b_ada[l][None, :])[:bsz]
        ada = ada.reshape(bsz, N_ADA, D_MODEL)
        x = _mixer_call(
            x, ada, norm1_w[l][None, :], w_in[l].astype(_BF16), w_s[l],
            b_s[l][:, :, None], v_ln_w[l][None, :], v_ln_b[l][None, :],
            lower_bounds, gn_w[l][None, :], w_out[l].astype(_BF16), tri, lvl)
        x = _ffn_call(x, ada, norm2_w[l][None, :], w_ffn_in[l].astype(_BF16),
                      w_ffn_out[l].astype(_BF16), final_norm_w[None, :])
    return x
```

```python
import functools

import numpy as np
import jax
import jax.numpy as jnp
from jax import lax
from jax.experimental import pallas as pl
from jax.experimental.pallas import tpu as pltpu

D_MODEL = 1024
D_GMLP = 512
D_HGRN = 512
HEADS = 4
HEAD_DIM = 128
BLOCK = 128
GATE_CHUNK = 64
D_IN = 2 * D_GMLP + 4 * D_HGRN
D_FF = 2816
N_ADA = 6
EPS = 1e-6

LEVELS = (128, 64, 32, 16, 8, 4, 2)
DIAG_LEVEL = len(LEVELS)

MIX_TILE = 256
MXU_CHUNK = 256
FFN_TILE = 256
VMEM_LIMIT = 56 * 1024 * 1024

_F32 = jnp.float32
_BF16 = jnp.bfloat16


def _dot(a, b):
    return jnp.dot(a, b, preferred_element_type=_F32)


def _dot_nt(a, b):
    return lax.dot_general(a, b, (((1,), (1,)), ((), ())), preferred_element_type=_F32)


def _dot_tn(a, b):
    return lax.dot_general(a, b, (((0,), (0,)), ((), ())), preferred_element_type=_F32)


def _silu(x):
    return x * jax.nn.sigmoid(x)


def _gelu(x):
    half = 0.5 * x
    return half + half * lax.erf(x * np.float32(1.0 / np.sqrt(2.0)))


def _level_table():
    t = np.arange(BLOCK)[:, None]
    s = np.arange(BLOCK)[None, :]
    table = np.full((BLOCK, BLOCK), -1, np.int32)
    table[t == s] = DIAG_LEVEL
    for idx, n in enumerate(LEVELS):
        h = n // 2
        same = (t // n) == (s // n)
        table[same & ((t % n) >= h) & ((s % n) < h)] = idx
    return table


def _ada_kernel(c_ref, w_ref, b_ref, o_ref):
    c = c_ref[...]
    c_act = _silu(c).astype(_BF16)
    o_ref[...] = _dot(c_act, w_ref[...].astype(_BF16)) + b_ref[...]


def _ada_call(c_pad, w_ada, b_ada):
    n_out = w_ada.shape[1]
    tn = 1536
    return pl.pallas_call(
        _ada_kernel,
        grid=(n_out // tn,),
        in_specs=[
            pl.BlockSpec((c_pad.shape[0], D_MODEL), lambda j: (0, 0)),
            pl.BlockSpec((D_MODEL, tn), lambda j: (0, j)),
            pl.BlockSpec((1, tn), lambda j: (0, j)),
        ],
        out_specs=pl.BlockSpec((c_pad.shape[0], tn), lambda j: (0, j)),
        out_shape=jax.ShapeDtypeStruct((c_pad.shape[0], n_out), _F32),
        compiler_params=pltpu.CompilerParams(dimension_semantics=("arbitrary",)),
        name="ada",
    )(c_pad, w_ada, b_ada)


def _neg_abs(x):
    bits = pltpu.bitcast(x, jnp.uint32) | jnp.uint32(0x80000000)
    return pltpu.bitcast(bits, _F32)


SUBLANES = 8


def _decay_factor(level_n, b_sc, f_h, row, col0):
    h = level_n // 2
    if level_n == 2:
        return jnp.where((row % 2) == 1, f_h, 1.0)

    def ref_rows(r, n_rows):
        return jnp.broadcast_to(b_sc[pl.ds(r, 1), pl.ds(col0, HEAD_DIM)], (n_rows, HEAD_DIM))

    if level_n < SUBLANES:
        lower_block = (row[:SUBLANES] % SUBLANES) < level_n
        pieces = [jnp.where(lower_block, ref_rows(r0 + h - 1, SUBLANES),
                            ref_rows(r0 + level_n + h - 1, SUBLANES))
                  for r0 in range(0, BLOCK, SUBLANES)]
    else:
        pieces = [ref_rows(r0 + h - 1, level_n) for r0 in range(0, BLOCK, level_n)]
    b_ref = pieces[0] if len(pieces) == 1 else jnp.concatenate(pieces, axis=0)
    b_h = b_sc[:, pl.ds(col0, HEAD_DIM)]
    return jnp.exp2(_neg_abs(b_h - b_ref))


def _modulated_input(x, ada_b, n1w_ref):
    sh1, sc1 = ada_b[0:1], ada_b[1:2]
    ms = jnp.mean(x * x, axis=-1, keepdims=True)
    hmod = x * lax.rsqrt(ms + EPS) * n1w_ref[...]
    return (hmod * (1.0 + sc1) + sh1).astype(_BF16)


class _Interleaver:
    def __init__(self, tasks, n_points):
        self._tasks = list(tasks)
        self._n_points = n_points
        self._calls = 0
        self._done = 0

    def __call__(self):
        self._calls += 1
        due = min(len(self._tasks), self._calls * len(self._tasks) // self._n_points)
        while self._done < due:
            self._tasks[self._done]()
            self._done += 1

    def flush(self):
        assert self._calls == self._n_points, (self._calls, self._n_points)
        while self._done < len(self._tasks):
            self._tasks[self._done]()
            self._done += 1


POINTS_PER_BLOCK = 3 + HEADS * (len(LEVELS) // 2 + 1)


def _mixer_kernel(x_ref, xn_ref, ada_ref, n1w_ref, w_in_ref, ws_ref, bs_ref, lnw_ref, lnb_ref,
                  lbraw_ref, gnw_ref, w_out_ref, tri_ref, lvl_ref,
                  o_ref, state_sc, proj_a, proj_b, h_a, h_b, y_sc, b_sc, wsm_sc,
                  *, steps_per_seq, n_steps):
    g = pl.program_id(0)
    batch = g // steps_per_seq
    batch_next = jnp.minimum(g + 1, n_steps - 1) // steps_per_seq
    ada_b = ada_ref[batch]

    @pl.when(g == 0)
    def _():
        h0 = _modulated_input(x_ref[pl.ds(0, MIX_TILE), :], ada_b, n1w_ref)
        proj_a[...] = _dot(h0, w_in_ref[...])
        col = lax.broadcasted_iota(jnp.int32, (BLOCK, BLOCK), 1)
        rowb = lax.broadcasted_iota(jnp.int32, (BLOCK, BLOCK), 0)
        gate_mask = (rowb // GATE_CHUNK) >= (col // GATE_CHUNK)
        for hd in range(HEADS):
            wsm_sc[hd] = jnp.where(gate_mask, ws_ref[hd], 0.0).astype(_BF16)

    @pl.when(g % steps_per_seq == 0)
    def _():
        state_sc[...] = jnp.zeros_like(state_sc)

    lbraw = lbraw_ref[...]
    lbe = jnp.exp(lbraw - jnp.max(lbraw, axis=0, keepdims=True))
    lb = (lbe / jnp.sum(lbe, axis=0, keepdims=True))[0:1]

    row = lax.broadcasted_iota(jnp.int32, (BLOCK, HEAD_DIM), 0)
    lvl = lvl_ref[...]
    level_is = [lvl == idx for idx in range(len(LEVELS))]
    upper_rows = [(row % n) >= (n // 2) for n in LEVELS]

    def mix_block(proj_sc, prow0, yrow0, b_blk, background):
        rows = pl.ds(prow0, BLOCK)
        yrows = pl.ds(yrow0, BLOCK)

        u = proj_sc[rows, pl.ds(0, D_GMLP)]
        v = proj_sc[rows, pl.ds(D_GMLP, D_GMLP)]
        gv = _gelu(v)
        mu = jnp.mean(gv, axis=-1, keepdims=True)
        cen = gv - mu
        var = jnp.mean(cen * cen, axis=-1, keepdims=True)
        vn = (cen * lax.rsqrt(var + EPS) * lnw_ref[...] + lnb_ref[...]).astype(_BF16)
        background()
        mixed = []
        for hd in range(HEADS):
            mixed.append(_dot(wsm_sc[hd], vn[:, hd * HEAD_DIM:(hd + 1) * HEAD_DIM]) + bs_ref[hd])
        y_a = _gelu(u) * jnp.concatenate(mixed, axis=-1)
        y_sc[yrows, pl.ds(0, D_GMLP)] = y_a.astype(_BF16)
        background()

        o0 = 2 * D_GMLP
        fl = proj_sc[rows, pl.ds(o0 + D_HGRN, D_HGRN)]
        f = lb + (1.0 - lb) * jax.nn.sigmoid(fl)
        logf = jnp.log2(f)
        p_hi = logf.astype(_BF16)
        rem = logf - p_hi.astype(_F32)
        p_mid = rem.astype(_BF16)
        p_lo = (rem - p_mid.astype(_F32)).astype(_BF16)
        tri = tri_ref[...]
        b_blk[...] = (_dot(tri, p_hi) + _dot(tri, p_mid)) + _dot(tri, p_lo)
        background()

        for hd in range(HEADS):
            c0 = hd * HEAD_DIM
            q_h = _silu(proj_sc[rows, pl.ds(o0 + c0, HEAD_DIM)])
            f_h = f[:, c0:c0 + HEAD_DIM]
            k_h = 1.0 - f_h
            v_h = proj_sc[rows, pl.ds(o0 + 2 * D_HGRN + c0, HEAD_DIM)]
            g_h = proj_sc[rows, pl.ds(o0 + 3 * D_HGRN + c0, HEAD_DIM)]
            b_h = b_blk[:, pl.ds(c0, HEAD_DIM)]

            attn = jnp.zeros((BLOCK, BLOCK), _F32)
            for idx, n in enumerate(LEVELS):
                e = _decay_factor(n, b_blk, f_h, row, c0)
                z = (jnp.where(upper_rows[idx], q_h, k_h) * e).astype(_BF16)
                attn = jnp.where(level_is[idx], _dot_nt(z, z), attn)
                if idx % 2 == 1:
                    background()
            diag = jnp.sum(q_h * k_h, axis=-1, keepdims=True)

            s_t = state_sc[hd]
            q_hat = (q_h * jnp.exp2(b_h)).astype(_BF16)
            v_bf = v_h.astype(_BF16)
            o_h = _dot_nt(q_hat, s_t.astype(_BF16)) + _dot(attn.astype(_BF16), v_bf) + diag * v_h

            b_last = b_blk[pl.ds(BLOCK - 1, 1), pl.ds(c0, HEAD_DIM)]
            k_hat = (k_h * jnp.exp2(b_last - b_h)).astype(_BF16)
            state_sc[hd] = s_t * jnp.exp2(b_last) + _dot_tn(v_bf, k_hat)

            oms = jnp.mean(o_h * o_h, axis=-1, keepdims=True)
            y_b = o_h * lax.rsqrt(oms + EPS) * gnw_ref[...] * _silu(g_h)
            y_sc[yrows, pl.ds(D_GMLP + c0, HEAD_DIM)] = y_b.astype(_BF16)
            background()

    blocks_per_tile = MIX_TILE // BLOCK
    g1 = ada_b[2:3]

    def projection_tasks(h_sc, dst):
        def chunk(c):
            cols = pl.ds(c * MXU_CHUNK, MXU_CHUNK)
            dst[:, cols] = _dot(h_sc[...], w_in_ref[:, cols])
        return [functools.partial(chunk, c) for c in range(D_IN // MXU_CHUNK)]

    def finish_tasks(t):
        trow = pl.ds(t * MIX_TILE, MIX_TILE)

        def chunk(c):
            cols = pl.ds(c * MXU_CHUNK, MXU_CHUNK)
            mix = _dot(y_sc[trow, :], w_out_ref[:, cols])
            o_ref[trow, cols] = x_ref[trow, cols] + g1[:, c * MXU_CHUNK:(c + 1) * MXU_CHUNK] * mix
        return [functools.partial(chunk, c) for c in range(D_MODEL // MXU_CHUNK)]

    def mix_tile(proj_sc, t, tasks):
        background = _Interleaver(tasks, blocks_per_tile * POINTS_PER_BLOCK)
        for i in range(blocks_per_tile):
            mix_block(proj_sc, i * BLOCK, t * MIX_TILE + i * BLOCK,
                      b_sc.at[t * blocks_per_tile + i], background)
        background.flush()

    h_b[...] = _modulated_input(x_ref[pl.ds(MIX_TILE, MIX_TILE), :], ada_b, n1w_ref)
    mix_tile(proj_a, 0, projection_tasks(h_b, proj_b))
    h_a[...] = _modulated_input(xn_ref[...], ada_ref[batch_next], n1w_ref)
    mix_tile(proj_b, 1, finish_tasks(0) + projection_tasks(h_a, proj_a))
    for task in finish_tasks(1):
        task()


def _mixer_call(x, ada, n1w, w_in, w_s, b_s, lnw, lnb, lbraw, gnw, w_out, tri, lvl):
    bsz, seq, _ = x.shape
    step_rows = 2 * MIX_TILE
    steps_per_seq = seq // step_rows
    n_steps = bsz * steps_per_seq
    n_tiles = 2 * n_steps
    x2 = x.reshape(bsz * seq, D_MODEL)
    const2 = lambda g: (0, 0)
    const3 = lambda g: (0, 0, 0)
    kernel = functools.partial(_mixer_kernel, steps_per_seq=steps_per_seq, n_steps=n_steps)
    out = pl.pallas_call(
        kernel,
        grid=(n_steps,),
        in_specs=[
            pl.BlockSpec((step_rows, D_MODEL), lambda g: (g, 0)),
            pl.BlockSpec((MIX_TILE, D_MODEL), lambda g: (jnp.minimum(2 * g + 2, n_tiles - 1), 0)),
            pl.BlockSpec((bsz, N_ADA, D_MODEL), const3),
            pl.BlockSpec((1, D_MODEL), const2),
            pl.BlockSpec((D_MODEL, D_IN), const2),
            pl.BlockSpec((HEADS, BLOCK, BLOCK), const3),
            pl.BlockSpec((HEADS, BLOCK, 1), const3),
            pl.BlockSpec((1, D_GMLP), const2),
            pl.BlockSpec((1, D_GMLP), const2),
            pl.BlockSpec((2, D_HGRN), const2),
            pl.BlockSpec((1, HEAD_DIM), const2),
            pl.BlockSpec((D_MODEL, D_MODEL), const2),
            pl.BlockSpec((BLOCK, BLOCK), const2),
            pl.BlockSpec((BLOCK, BLOCK), const2),
        ],
        out_specs=pl.BlockSpec((step_rows, D_MODEL), lambda g: (g, 0)),
        out_shape=jax.ShapeDtypeStruct(x2.shape, _F32),
        scratch_shapes=[
            pltpu.VMEM((HEADS, HEAD_DIM, HEAD_DIM), _F32),
            pltpu.VMEM((MIX_TILE, D_IN), _F32),
            pltpu.VMEM((MIX_TILE, D_IN), _F32),
            pltpu.VMEM((MIX_TILE, D_MODEL), _BF16),
            pltpu.VMEM((MIX_TILE, D_MODEL), _BF16),
            pltpu.VMEM((step_rows, D_MODEL), _BF16),
            pltpu.VMEM((step_rows // BLOCK, BLOCK, D_HGRN), _F32),
            pltpu.VMEM((HEADS, BLOCK, BLOCK), _BF16),
        ],
        compiler_params=pltpu.CompilerParams(
            dimension_semantics=("arbitrary",),
            vmem_limit_bytes=VMEM_LIMIT),
        name="mixer",
    )(x2, x2, ada, n1w, w_in, w_s, b_s, lnw, lnb, lbraw, gnw, w_out, tri, lvl)
    return out.reshape(x.shape)


def _ffn_kernel(x_ref, ada_ref, n2w_ref, w1_ref, w2_ref, fnw_ref, o_ref):
    x = x_ref[0]
    ada = ada_ref[0]
    sh2, sc2, g2 = ada[3:4], ada[4:5], ada[5:6]
    ms = jnp.mean(x * x, axis=-1, keepdims=True)
    hmod = x * lax.rsqrt(ms + EPS) * n2w_ref[...]
    hmod = (hmod * (1.0 + sc2) + sh2).astype(_BF16)
    gu = _dot(hmod, w1_ref[...])
    act = (_silu(gu[:, :D_FF]) * gu[:, D_FF:]).astype(_BF16)
    x2 = x + g2 * _dot(act, w2_ref[...])
    ms2 = jnp.mean(x2 * x2, axis=-1, keepdims=True)
    o_ref[0] = x2 * lax.rsqrt(ms2 + EPS) * fnw_ref[...]


def _ffn_call(x, ada, n2w, w1, w2, fnw):
    bsz, seq, _ = x.shape
    const2 = lambda b, s: (0, 0)
    return pl.pallas_call(
        _ffn_kernel,
        grid=(bsz, seq // FFN_TILE),
        in_specs=[
            pl.BlockSpec((1, FFN_TILE, D_MODEL), lambda b, s: (b, s, 0)),
            pl.BlockSpec((1, N_ADA, D_MODEL), lambda b, s: (b, 0, 0)),
            pl.BlockSpec((1, D_MODEL), const2),
            pl.BlockSpec((D_MODEL, 2 * D_FF), const2),
            pl.BlockSpec((D_FF, D_MODEL), const2),
            pl.BlockSpec((1, D_MODEL), const2),
        ],
        out_specs=pl.BlockSpec((1, FFN_TILE, D_MODEL), lambda b, s: (b, s, 0)),
        out_shape=jax.ShapeDtypeStruct(x.shape, _F32),
        compiler_params=pltpu.CompilerParams(
            dimension_semantics=("arbitrary", "arbitrary"),
            vmem_limit_bytes=VMEM_LIMIT),
        name="ffn",
    )(x, ada, n2w, w1, w2, fnw)


def kernel(x, c, w_ada, b_ada, norm1_w, w_in, w_s, b_s, v_ln_w, v_ln_b, lower_bounds,
           gn_w, w_out, norm2_w, w_ffn_in, w_ffn_out, final_norm_w):
    bsz = x.shape[0]
    depth = w_in.shape[0]
    assert depth == 1 and lower_bounds.shape[0] == 2

    tri = jnp.asarray(np.tril(np.ones((BLOCK, BLOCK), np.float32)), _BF16)
    lvl = jnp.asarray(_level_table())

    c_pad = jnp.zeros((8, D_MODEL), _F32).at[:bsz].set(c)
    for l in range(depth):
        ada = _ada_call(c_pad, w_ada[l], b_ada[l][None, :])[:bsz]
        ada = ada.reshape(bsz, N_ADA, D_MODEL)
        x = _mixer_call(
            x, ada, norm1_w[l][None, :], w_in[l].astype(_BF16), w_s[l],
            b_s[l][:, :, None], v_ln_w[l][None, :], v_ln_b[l][None, :],
            lower_bounds, gn_w[l][None, :], w_out[l].astype(_BF16), tri, lvl)
        x = _ffn_call(x, ada, norm2_w[l][None, :], w_ffn_in[l].astype(_BF16),
                      w_ffn_out[l].astype(_BF16), final_norm_w[None, :])
    return x
```

```python
import functools

import numpy as np
import jax
import jax.numpy as jnp
from jax import lax
from jax.experimental import pallas as pl
from jax.experimental.pallas import tpu as pltpu

D_MODEL = 1024
D_GMLP = 512
D_HGRN = 512
HEADS = 4
HEAD_DIM = 128
BLOCK = 128
GATE_CHUNK = 64
D_IN = 2 * D_GMLP + 4 * D_HGRN
D_FF = 2816
N_ADA = 6
EPS = 1e-6

LEVELS = (128, 64, 32, 16, 8, 4, 2)
DIAG_LEVEL = len(LEVELS)

MIX_TILE = 256
MXU_CHUNK = 256
FFN_PART = 256
FFN_PARTS = 4
VMEM_LIMIT = 56 * 1024 * 1024

_F32 = jnp.float32
_BF16 = jnp.bfloat16


def _dot(a, b):
    return jnp.dot(a, b, preferred_element_type=_F32)


def _dot_nt(a, b):
    return lax.dot_general(a, b, (((1,), (1,)), ((), ())), preferred_element_type=_F32)


def _silu(x):
    return x * jax.nn.sigmoid(x)


def _gelu(x):
    half = 0.5 * x
    return half + half * lax.erf(x * np.float32(1.0 / np.sqrt(2.0)))


def _level_table():
    t = np.arange(BLOCK)[:, None]
    s = np.arange(BLOCK)[None, :]
    table = np.full((BLOCK, BLOCK), -1, np.int32)
    table[t == s] = DIAG_LEVEL
    for idx, n in enumerate(LEVELS):
        h = n // 2
        same = (t // n) == (s // n)
        table[same & ((t % n) >= h) & ((s % n) < h)] = idx
    return table


def _ada_kernel(c_ref, w_ref, b_ref, o_ref):
    c = c_ref[...]
    c_act = _silu(c).astype(_BF16)
    o_ref[...] = _dot(c_act, w_ref[...].astype(_BF16)) + b_ref[...]


def _ada_call(c_pad, w_ada, b_ada):
    n_out = w_ada.shape[1]
    tn = 1536
    return pl.pallas_call(
        _ada_kernel,
        grid=(n_out // tn,),
        in_specs=[
            pl.BlockSpec((c_pad.shape[0], D_MODEL), lambda j: (0, 0)),
            pl.BlockSpec((D_MODEL, tn), lambda j: (0, j)),
            pl.BlockSpec((1, tn), lambda j: (0, j)),
        ],
        out_specs=pl.BlockSpec((c_pad.shape[0], tn), lambda j: (0, j)),
        out_shape=jax.ShapeDtypeStruct((c_pad.shape[0], n_out), _F32),
        compiler_params=pltpu.CompilerParams(dimension_semantics=("arbitrary",)),
        name="ada",
    )(c_pad, w_ada, b_ada)


def _neg_abs(x):
    bits = pltpu.bitcast(x, jnp.uint32) | jnp.uint32(0x80000000)
    return pltpu.bitcast(bits, _F32)


SUBLANES = 8


def _half_select(level_n, q, k, upper):
    h = level_n // 2
    if h % SUBLANES != 0:
        return jnp.where(upper, q, k)
    pieces = []
    for r0 in range(0, BLOCK, level_n):
        pieces += [k[r0:r0 + h], q[r0 + h:r0 + level_n]]
    return jnp.concatenate(pieces, axis=0)


def _decay_factor(level_n, b_sc, row, col0):
    h = level_n // 2

    def ref_rows(r, n_rows):
        return jnp.broadcast_to(b_sc[pl.ds(r, 1), pl.ds(col0, HEAD_DIM)], (n_rows, HEAD_DIM))

    if level_n < SUBLANES:
        lower_block = (row[:SUBLANES] % SUBLANES) < level_n
        pieces = [jnp.where(lower_block, ref_rows(r0 + h - 1, SUBLANES),
                            ref_rows(r0 + level_n + h - 1, SUBLANES))
                  for r0 in range(0, BLOCK, SUBLANES)]
    else:
        pieces = [ref_rows(r0 + h - 1, level_n) for r0 in range(0, BLOCK, level_n)]
    b_ref = pieces[0] if len(pieces) == 1 else jnp.concatenate(pieces, axis=0)
    b_h = b_sc[:, pl.ds(col0, HEAD_DIM)]
    return jnp.exp2(_neg_abs(b_h - b_ref))


def _modulated_input(x, ada_b, n1w_ref):
    sh1, sc1 = ada_b[0:1], ada_b[1:2]
    ms = jnp.mean(x * x, axis=-1, keepdims=True)
    hmod = x * lax.rsqrt(ms + EPS) * n1w_ref[...]
    return (hmod * (1.0 + sc1) + sh1).astype(_BF16)


BF16_ROWS = 16


def _cast_rows(n_rows, n_steps):
    rows = -(-n_rows // n_steps)
    rows = -(-rows // BF16_ROWS) * BF16_ROWS
    while n_rows % rows:
        rows += BF16_ROWS
    return rows


class _Interleaver:
    def __init__(self, tasks, n_points):
        self._tasks = list(tasks)
        self._n_points = n_points
        self._calls = 0
        self._done = 0

    def __call__(self):
        self._calls += 1
        due = min(len(self._tasks), self._calls * len(self._tasks) // self._n_points)
        while self._done < due:
            self._tasks[self._done]()
            self._done += 1

    def flush(self):
        assert self._calls == self._n_points, (self._calls, self._n_points)
        while self._done < len(self._tasks):
            self._tasks[self._done]()
            self._done += 1


POINTS_PER_BLOCK = 3 + HEADS * (len(LEVELS) // 2 + 1)


def _mixer_kernel(x_ref, xn_ref, ada_ref, n1w_ref, w_in_f32, ws_ref, bs_ref, lnw_ref, lnb_ref,
                  lbraw_ref, gnw_ref, w_out_f32, tri_ref, lvl_ref, w1_f32, w2_f32,
                  o_ref, w1_bf, w2_bf,
                  state_sc, proj_a, proj_b, h_a, h_b, y_sc, b_sc, wsm_sc, w_in_ref, w_out_ref,
                  *, steps_per_seq, n_steps):
    g = pl.program_id(0)
    batch = g // steps_per_seq
    batch_next = jnp.minimum(g + 1, n_steps - 1) // steps_per_seq
    ada_b = ada_ref[batch]

    w1_bf[...] = w1_f32[...].astype(_BF16)
    w2_bf[...] = w2_f32[...].astype(_BF16)

    @pl.when(g == 0)
    def _():
        w_in_ref[...] = w_in_f32[...].astype(_BF16)
        w_out_ref[...] = w_out_f32[...].astype(_BF16)
        h0 = _modulated_input(x_ref[pl.ds(0, MIX_TILE), :], ada_b, n1w_ref)
        proj_a[...] = _dot(h0, w_in_ref[...])
        col = lax.broadcasted_iota(jnp.int32, (BLOCK, BLOCK), 1)
        rowb = lax.broadcasted_iota(jnp.int32, (BLOCK, BLOCK), 0)
        gate_mask = (rowb // GATE_CHUNK) >= (col // GATE_CHUNK)
        for hd in range(HEADS):
            wsm_sc[hd] = jnp.where(gate_mask, ws_ref[hd], 0.0).astype(_BF16)

    @pl.when(g % steps_per_seq == 0)
    def _():
        state_sc[...] = jnp.zeros_like(state_sc)

    lbraw = lbraw_ref[...]
    lbe = jnp.exp(lbraw - jnp.max(lbraw, axis=0, keepdims=True))
    lb = (lbe / jnp.sum(lbe, axis=0, keepdims=True))[0:1]

    row = lax.broadcasted_iota(jnp.int32, (BLOCK, HEAD_DIM), 0)
    lvl = lvl_ref[...]
    level_is = [lvl == idx for idx in range(len(LEVELS))]
    upper_rows = [(row % n) >= (n // 2) for n in LEVELS]

    def mix_block(proj_sc, prow0, yrow0, b_blk, background):
        rows = pl.ds(prow0, BLOCK)
        yrows = pl.ds(yrow0, BLOCK)

        u = proj_sc[rows, pl.ds(0, D_GMLP)]
        v = proj_sc[rows, pl.ds(D_GMLP, D_GMLP)]
        gv = _gelu(v)
        mu = jnp.mean(gv, axis=-1, keepdims=True)
        cen = gv - mu
        var = jnp.mean(cen * cen, axis=-1, keepdims=True)
        vn = (cen * lax.rsqrt(var + EPS) * lnw_ref[...] + lnb_ref[...]).astype(_BF16)
        background()
        mixed = []
        for hd in range(HEADS):
            mixed.append(_dot(wsm_sc[hd], vn[:, hd * HEAD_DIM:(hd + 1) * HEAD_DIM]) + bs_ref[hd])
        y_a = _gelu(u) * jnp.concatenate(mixed, axis=-1)
        y_sc[yrows, pl.ds(0, D_GMLP)] = y_a.astype(_BF16)
        background()

        o0 = 2 * D_GMLP
        fl = proj_sc[rows, pl.ds(o0 + D_HGRN, D_HGRN)]
        f = lb + (1.0 - lb) * jax.nn.sigmoid(fl)
        logf = jnp.log2(f)
        p_hi = logf.astype(_BF16)
        p_lo = (logf - p_hi.astype(_F32)).astype(_BF16)
        tri = tri_ref[...]
        b_blk[...] = _dot(tri, p_hi) + _dot(tri, p_lo)
        background()

        for hd in range(HEADS):
            c0 = hd * HEAD_DIM
            q_h = _silu(proj_sc[rows, pl.ds(o0 + c0, HEAD_DIM)])
            f_h = f[:, c0:c0 + HEAD_DIM]
            k_h = 1.0 - f_h
            v_h = proj_sc[rows, pl.ds(o0 + 2 * D_HGRN + c0, HEAD_DIM)]
            g_h = proj_sc[rows, pl.ds(o0 + 3 * D_HGRN + c0, HEAD_DIM)]
            b_h = b_blk[:, pl.ds(c0, HEAD_DIM)]

            attn = jnp.zeros((BLOCK, BLOCK), _F32)
            for idx, n in enumerate(LEVELS):
                if n == 2:
                    z = jnp.where(upper_rows[idx], q_h * f_h, k_h)
                else:
                    z = _half_select(n, q_h, k_h, upper_rows[idx]) * _decay_factor(n, b_blk, row, c0)
                z = z.astype(_BF16)
                attn = jnp.where(level_is[idx], _dot_nt(z, z), attn)
                if idx % 2 == 1:
                    background()
            diag = jnp.sum(q_h * k_h, axis=-1, keepdims=True)

            s_t = state_sc[hd]
            q_hat = (q_h * jnp.exp2(b_h)).astype(_BF16)
            v_t = v_h.T.astype(_BF16)
            lhs = jnp.concatenate([attn.astype(_BF16), q_hat], axis=1)
            rhs_t = jnp.concatenate([v_t, s_t.astype(_BF16)], axis=1)
            o_h = _dot_nt(lhs, rhs_t) + diag * v_h

            b_last = b_blk[pl.ds(BLOCK - 1, 1), pl.ds(c0, HEAD_DIM)]
            k_hat = (k_h * jnp.exp2(b_last - b_h)).astype(_BF16)
            state_sc[hd] = s_t * jnp.exp2(b_last) + _dot(v_t, k_hat)

            oms = jnp.mean(o_h * o_h, axis=-1, keepdims=True)
            y_b = o_h * lax.rsqrt(oms + EPS) * gnw_ref[...] * _silu(g_h)
            y_sc[yrows, pl.ds(D_GMLP + c0, HEAD_DIM)] = y_b.astype(_BF16)
            background()

    blocks_per_tile = MIX_TILE // BLOCK
    g1 = ada_b[2:3]

    def projection_tasks(h_sc, dst):
        def chunk(c):
            cols = pl.ds(c * MXU_CHUNK, MXU_CHUNK)
            dst[:, cols] = _dot(h_sc[...], w_in_ref[:, cols])
        return [functools.partial(chunk, c) for c in range(D_IN // MXU_CHUNK)]

    def finish_tasks(t):
        trow = pl.ds(t * MIX_TILE, MIX_TILE)

        def chunk(c):
            cols = pl.ds(c * MXU_CHUNK, MXU_CHUNK)
            mix = _dot(y_sc[trow, :], w_out_ref[:, cols])
            o_ref[trow, cols] = x_ref[trow, cols] + g1[:, c * MXU_CHUNK:(c + 1) * MXU_CHUNK] * mix
        return [functools.partial(chunk, c) for c in range(D_MODEL // MXU_CHUNK)]

    def mix_tile(proj_sc, t, tasks):
        background = _Interleaver(tasks, blocks_per_tile * POINTS_PER_BLOCK)
        for i in range(blocks_per_tile):
            mix_block(proj_sc, i * BLOCK, t * MIX_TILE + i * BLOCK,
                      b_sc.at[t * blocks_per_tile + i], background)
        background.flush()

    h_b[...] = _modulated_input(x_ref[pl.ds(MIX_TILE, MIX_TILE), :], ada_b, n1w_ref)
    mix_tile(proj_a, 0, projection_tasks(h_b, proj_b))
    h_a[...] = _modulated_input(xn_ref[...], ada_ref[batch_next], n1w_ref)
    mix_tile(proj_b, 1, finish_tasks(0) + projection_tasks(h_a, proj_a))
    for task in finish_tasks(1):
        task()


def _mixer_call(x, ada, n1w, w_in, w_s, b_s, lnw, lnb, lbraw, gnw, w_out, tri, lvl, w1, w2):
    bsz, seq, _ = x.shape
    step_rows = 2 * MIX_TILE
    steps_per_seq = seq // step_rows
    n_steps = bsz * steps_per_seq
    n_tiles = 2 * n_steps
    x2 = x.reshape(bsz * seq, D_MODEL)
    const2 = lambda g: (0, 0)
    const3 = lambda g: (0, 0, 0)
    resident = dict(pipeline_mode=pl.Buffered(1))
    w1_rows = _cast_rows(w1.shape[0], n_steps)
    w2_rows = _cast_rows(w2.shape[0], n_steps)
    w1_map = lambda g: (jnp.minimum(g, w1.shape[0] // w1_rows - 1), 0)
    w2_map = lambda g: (jnp.minimum(g, w2.shape[0] // w2_rows - 1), 0)
    kernel = functools.partial(_mixer_kernel, steps_per_seq=steps_per_seq, n_steps=n_steps)
    out, w1_bf, w2_bf = pl.pallas_call(
        kernel,
        grid=(n_steps,),
        in_specs=[
            pl.BlockSpec((step_rows, D_MODEL), lambda g: (g, 0)),
            pl.BlockSpec((MIX_TILE, D_MODEL), lambda g: (jnp.minimum(2 * g + 2, n_tiles - 1), 0)),
            pl.BlockSpec((bsz, N_ADA, D_MODEL), const3),
            pl.BlockSpec((1, D_MODEL), const2),
            pl.BlockSpec((D_MODEL, D_IN), const2, **resident),
            pl.BlockSpec((HEADS, BLOCK, BLOCK), const3),
            pl.BlockSpec((HEADS, BLOCK, 1), const3),
            pl.BlockSpec((1, D_GMLP), const2),
            pl.BlockSpec((1, D_GMLP), const2),
            pl.BlockSpec((2, D_HGRN), const2),
            pl.BlockSpec((1, HEAD_DIM), const2),
            pl.BlockSpec((D_MODEL, D_MODEL), const2, **resident),
            pl.BlockSpec((BLOCK, BLOCK), const2),
            pl.BlockSpec((BLOCK, BLOCK), const2),
            pl.BlockSpec((w1_rows, w1.shape[1]), w1_map),
            pl.BlockSpec((w2_rows, w2.shape[1]), w2_map),
        ],
        out_specs=[
            pl.BlockSpec((step_rows, D_MODEL), lambda g: (g, 0)),
            pl.BlockSpec((w1_rows, w1.shape[1]), w1_map),
            pl.BlockSpec((w2_rows, w2.shape[1]), w2_map),
        ],
        out_shape=[
            jax.ShapeDtypeStruct(x2.shape, _F32),
            jax.ShapeDtypeStruct(w1.shape, _BF16),
            jax.ShapeDtypeStruct(w2.shape, _BF16),
        ],
        scratch_shapes=[
            pltpu.VMEM((HEADS, HEAD_DIM, HEAD_DIM), _F32),
            pltpu.VMEM((MIX_TILE, D_IN), _F32),
            pltpu.VMEM((MIX_TILE, D_IN), _F32),
            pltpu.VMEM((MIX_TILE, D_MODEL), _BF16),
            pltpu.VMEM((MIX_TILE, D_MODEL), _BF16),
            pltpu.VMEM((step_rows, D_MODEL), _BF16),
            pltpu.VMEM((step_rows // BLOCK, BLOCK, D_HGRN), _F32),
            pltpu.VMEM((HEADS, BLOCK, BLOCK), _BF16),
            pltpu.VMEM((D_MODEL, D_IN), _BF16),
            pltpu.VMEM((D_MODEL, D_MODEL), _BF16),
        ],
        compiler_params=pltpu.CompilerParams(
            dimension_semantics=("arbitrary",),
            vmem_limit_bytes=VMEM_LIMIT),
        name="mixer",
    )(x2, x2, ada, n1w, w_in, w_s, b_s, lnw, lnb, lbraw, gnw, w_out, tri, lvl, w1, w2)
    return out.reshape(x.shape), w1_bf, w2_bf


def _ffn_kernel(x_ref, ada_ref, n2w_ref, w1_ref, w2_ref, fnw_ref, o_ref):
    ada = ada_ref[0]
    sh2, sc2, g2 = ada[3:4], ada[4:5], ada[5:6]

    def hidden(p):
        x = x_ref[0, pl.ds(p * FFN_PART, FFN_PART), :]
        ms = jnp.mean(x * x, axis=-1, keepdims=True)
        hmod = x * lax.rsqrt(ms + EPS) * n2w_ref[...]
        hmod = (hmod * (1.0 + sc2) + sh2).astype(_BF16)
        gu = _dot(hmod, w1_ref[...])
        return (_silu(gu[:, :D_FF]) * gu[:, D_FF:]).astype(_BF16)

    def finish(p, act):
        rows = pl.ds(p * FFN_PART, FFN_PART)
        x2 = x_ref[0, rows, :] + g2 * _dot(act, w2_ref[...])
        ms2 = jnp.mean(x2 * x2, axis=-1, keepdims=True)
        o_ref[0, rows, :] = x2 * lax.rsqrt(ms2 + EPS) * fnw_ref[...]

    act = hidden(0)
    for p in range(1, FFN_PARTS):
        act_next = hidden(p)
        finish(p - 1, act)
        act = act_next
    finish(FFN_PARTS - 1, act)


def _ffn_call(x, ada, n2w, w1, w2, fnw):
    bsz, seq, _ = x.shape
    tile = FFN_PART * FFN_PARTS
    const2 = lambda b, s: (0, 0)
    resident = dict(pipeline_mode=pl.Buffered(1))
    return pl.pallas_call(
        _ffn_kernel,
        grid=(bsz, seq // tile),
        in_specs=[
            pl.BlockSpec((1, tile, D_MODEL), lambda b, s: (b, s, 0)),
            pl.BlockSpec((1, N_ADA, D_MODEL), lambda b, s: (b, 0, 0)),
            pl.BlockSpec((1, D_MODEL), const2),
            pl.BlockSpec((D_MODEL, 2 * D_FF), const2, **resident),
            pl.BlockSpec((D_FF, D_MODEL), const2, **resident),
            pl.BlockSpec((1, D_MODEL), const2),
        ],
        out_specs=pl.BlockSpec((1, tile, D_MODEL), lambda b, s: (b, s, 0)),
        out_shape=jax.ShapeDtypeStruct(x.shape, _F32),
        compiler_params=pltpu.CompilerParams(
            dimension_semantics=("arbitrary", "arbitrary"),
            vmem_limit_bytes=VMEM_LIMIT),
        name="ffn",
    )(x, ada, n2w, w1, w2, fnw)


def kernel(x, c, w_ada, b_ada, norm1_w, w_in, w_s, b_s, v_ln_w, v_ln_b, lower_bounds,
           gn_w, w_out, norm2_w, w_ffn_in, w_ffn_out, final_norm_w):
    bsz = x.shape[0]
    depth = w_in.shape[0]
    assert depth == 1 and lower_bounds.shape[0] == 2

    tri = jnp.asarray(np.tril(np.ones((BLOCK, BLOCK), np.float32)), _BF16)
    lvl = jnp.asarray(_level_table())

    for l in range(depth):
        ada = _ada_call(c, w_ada[l], b_ada[l][None, :])
        ada = ada.reshape(bsz, N_ADA, D_MODEL)
        x, w1_bf, w2_bf = _mixer_call(
            x, ada, norm1_w[l][None, :], w_in[l], w_s[l],
            b_s[l][:, :, None], v_ln_w[l][None, :], v_ln_b[l][None, :],
            lower_bounds, gn_w[l][None, :], w_out[l], tri, lvl, w_ffn_in[l], w_ffn_out[l])
        x = _ffn_call(x, ada, norm2_w[l][None, :], w1_bf, w2_bf, final_norm_w[None, :])
    return x
```

```python
import functools

import numpy as np
import jax
import jax.numpy as jnp
from jax import lax
from jax.experimental import pallas as pl
from jax.experimental.pallas import tpu as pltpu

D_MODEL = 1024
D_GMLP = 512
D_HGRN = 512
HEADS = 4
HEAD_DIM = 128
BLOCK = 128
GATE_CHUNK = 64
D_IN = 2 * D_GMLP + 4 * D_HGRN
D_FF = 2816
N_ADA = 6
EPS = 1e-6

LEVELS = (128, 64, 32, 16, 8, 4, 2)
DIAG_LEVEL = len(LEVELS)

MIX_TILE = 256
MXU_CHUNK = 256
FFN_PART = 256
FFN_PARTS = 4
VMEM_LIMIT = 56 * 1024 * 1024

_F32 = jnp.float32
_BF16 = jnp.bfloat16


def _dot(a, b):
    return jnp.dot(a, b, preferred_element_type=_F32)


def _dot_nt(a, b):
    return lax.dot_general(a, b, (((1,), (1,)), ((), ())), preferred_element_type=_F32)


def _silu(x):
    return x * jax.nn.sigmoid(x)


def _gelu(x):
    half = 0.5 * x
    return half + half * lax.erf(x * np.float32(1.0 / np.sqrt(2.0)))


def _level_table():
    t = np.arange(BLOCK)[:, None]
    s = np.arange(BLOCK)[None, :]
    table = np.full((BLOCK, BLOCK), -1, np.int32)
    table[t == s] = DIAG_LEVEL
    for idx, n in enumerate(LEVELS):
        h = n // 2
        same = (t // n) == (s // n)
        table[same & ((t % n) >= h) & ((s % n) < h)] = idx
    return table


def _ada_kernel(c_ref, w_ref, b_ref, o_ref):
    c = c_ref[...]
    c_act = _silu(c).astype(_BF16)
    o_ref[...] = _dot(c_act, w_ref[...].astype(_BF16)) + b_ref[...]


def _ada_call(c_pad, w_ada, b_ada):
    n_out = w_ada.shape[1]
    tn = 1536
    return pl.pallas_call(
        _ada_kernel,
        grid=(n_out // tn,),
        in_specs=[
            pl.BlockSpec((c_pad.shape[0], D_MODEL), lambda j: (0, 0)),
            pl.BlockSpec((D_MODEL, tn), lambda j: (0, j)),
            pl.BlockSpec((1, tn), lambda j: (0, j)),
        ],
        out_specs=pl.BlockSpec((c_pad.shape[0], tn), lambda j: (0, j)),
        out_shape=jax.ShapeDtypeStruct((c_pad.shape[0], n_out), _F32),
        compiler_params=pltpu.CompilerParams(dimension_semantics=("arbitrary",)),
        name="ada",
    )(c_pad, w_ada, b_ada)


def _neg_abs(x):
    bits = pltpu.bitcast(x, jnp.uint32) | jnp.uint32(0x80000000)
    return pltpu.bitcast(bits, _F32)


SUBLANES = 8


def _half_select(level_n, q, k, upper):
    h = level_n // 2
    if h % SUBLANES != 0:
        return jnp.where(upper, q, k)
    pieces = []
    for r0 in range(0, BLOCK, level_n):
        pieces += [k[r0:r0 + h], q[r0 + h:r0 + level_n]]
    return jnp.concatenate(pieces, axis=0)


def _decay_factor(level_n, b_sc, row, col0):
    h = level_n // 2

    def ref_rows(r, n_rows):
        return jnp.broadcast_to(b_sc[pl.ds(r, 1), pl.ds(col0, HEAD_DIM)], (n_rows, HEAD_DIM))

    if level_n < SUBLANES:
        lower_block = (row[:SUBLANES] % SUBLANES) < level_n
        pieces = [jnp.where(lower_block, ref_rows(r0 + h - 1, SUBLANES),
                            ref_rows(r0 + level_n + h - 1, SUBLANES))
                  for r0 in range(0, BLOCK, SUBLANES)]
    else:
        pieces = [ref_rows(r0 + h - 1, level_n) for r0 in range(0, BLOCK, level_n)]
    b_ref = pieces[0] if len(pieces) == 1 else jnp.concatenate(pieces, axis=0)
    b_h = b_sc[:, pl.ds(col0, HEAD_DIM)]
    return jnp.exp2(_neg_abs(b_h - b_ref))


def _modulated_input(x, ada_b, n1w_ref):
    sh1, sc1 = ada_b[0:1], ada_b[1:2]
    ms = jnp.mean(x * x, axis=-1, keepdims=True)
    hmod = x * lax.rsqrt(ms + EPS) * n1w_ref[...]
    return (hmod * (1.0 + sc1) + sh1).astype(_BF16)


BF16_ROWS = 16


def _cast_rows(n_rows, n_steps):
    rows = -(-n_rows // n_steps)
    rows = -(-rows // BF16_ROWS) * BF16_ROWS
    while n_rows % rows:
        rows += BF16_ROWS
    return rows


class _Interleaver:
    def __init__(self, tasks, n_points):
        self._tasks = list(tasks)
        self._n_points = n_points
        self._calls = 0
        self._done = 0

    def __call__(self):
        self._calls += 1
        due = min(len(self._tasks), self._calls * len(self._tasks) // self._n_points)
        while self._done < due:
            self._tasks[self._done]()
            self._done += 1

    def flush(self):
        assert self._calls == self._n_points, (self._calls, self._n_points)
        while self._done < len(self._tasks):
            self._tasks[self._done]()
            self._done += 1


POINTS_PER_BLOCK = 3 + HEADS * (len(LEVELS) // 2 + 1)


def _mixer_kernel(x_ref, xn_ref, ada_ref, n1w_ref, w_in_f32, ws_ref, bs_ref, lnw_ref, lnb_ref,
                  lbraw_ref, gnw_ref, w_out_f32, tri_ref, lvl_ref, w1_f32, w2_f32,
                  o_ref, w1_bf, w2_bf,
                  state_sc, proj_a, proj_b, h_a, h_b, y_sc, b_sc, wsm_sc, w_in_ref, w_out_ref,
                  *, steps_per_seq, n_steps):
    g = pl.program_id(0)
    batch = g // steps_per_seq
    batch_next = jnp.minimum(g + 1, n_steps - 1) // steps_per_seq
    ada_b = ada_ref[batch]

    w1_bf[...] = w1_f32[...].astype(_BF16)
    w2_bf[...] = w2_f32[...].astype(_BF16)

    @pl.when(g == 0)
    def _():
        w_in_ref[...] = w_in_f32[...].astype(_BF16)
        w_out_ref[...] = w_out_f32[...].astype(_BF16)
        h0 = _modulated_input(x_ref[pl.ds(0, MIX_TILE), :], ada_b, n1w_ref)
        proj_a[...] = _dot(h0, w_in_ref[...])
        col = lax.broadcasted_iota(jnp.int32, (BLOCK, BLOCK), 1)
        rowb = lax.broadcasted_iota(jnp.int32, (BLOCK, BLOCK), 0)
        gate_mask = (rowb // GATE_CHUNK) >= (col // GATE_CHUNK)
        for hd in range(HEADS):
            wsm_sc[hd] = jnp.where(gate_mask, ws_ref[hd], 0.0).astype(_BF16)

    @pl.when(g % steps_per_seq == 0)
    def _():
        state_sc[...] = jnp.zeros_like(state_sc)

    lbraw = lbraw_ref[...]
    lbe = jnp.exp(lbraw - jnp.max(lbraw, axis=0, keepdims=True))
    lb = (lbe / jnp.sum(lbe, axis=0, keepdims=True))[0:1]

    row = lax.broadcasted_iota(jnp.int32, (BLOCK, HEAD_DIM), 0)
    lvl = lvl_ref[...]
    upper_rows = [(row % n) >= (n // 2) for n in LEVELS]

    def mix_block(proj_sc, prow0, yrow0, b_blk, background):
        rows = pl.ds(prow0, BLOCK)
        yrows = pl.ds(yrow0, BLOCK)

        u = proj_sc[rows, pl.ds(0, D_GMLP)]
        v = proj_sc[rows, pl.ds(D_GMLP, D_GMLP)]
        gv = _gelu(v)
        mu = jnp.mean(gv, axis=-1, keepdims=True)
        cen = gv - mu
        var = jnp.mean(cen * cen, axis=-1, keepdims=True)
        vn = (cen * lax.rsqrt(var + EPS) * lnw_ref[...] + lnb_ref[...]).astype(_BF16)
        background()
        mixed = []
        for hd in range(HEADS):
            mixed.append(_dot(wsm_sc[hd], vn[:, hd * HEAD_DIM:(hd + 1) * HEAD_DIM]) + bs_ref[hd])
        y_a = _gelu(u) * jnp.concatenate(mixed, axis=-1)
        y_sc[yrows, pl.ds(0, D_GMLP)] = y_a.astype(_BF16)
        background()

        o0 = 2 * D_GMLP
        fl = proj_sc[rows, pl.ds(o0 + D_HGRN, D_HGRN)]
        f = lb + (1.0 - lb) * jax.nn.sigmoid(fl)
        logf = jnp.log2(f)
        p_hi = logf.astype(_BF16)
        p_lo = (logf - p_hi.astype(_F32)).astype(_BF16)
        tri = tri_ref[...]
        b_blk[...] = _dot(tri, p_hi) + _dot(tri, p_lo)
        background()

        for hd in range(HEADS):
            c0 = hd * HEAD_DIM
            q_h = _silu(proj_sc[rows, pl.ds(o0 + c0, HEAD_DIM)])
            f_h = f[:, c0:c0 + HEAD_DIM]
            k_h = 1.0 - f_h
            v_h = proj_sc[rows, pl.ds(o0 + 2 * D_HGRN + c0, HEAD_DIM)]
            g_h = proj_sc[rows, pl.ds(o0 + 3 * D_HGRN + c0, HEAD_DIM)]
            b_h = b_blk[:, pl.ds(c0, HEAD_DIM)]

            attn = [jnp.zeros((SUBLANES, BLOCK), _F32)] * (BLOCK // SUBLANES)
            for idx, n in enumerate(LEVELS):
                h = n // 2
                if n == 2:
                    z = jnp.where(upper_rows[idx], q_h * f_h, k_h)
                else:
                    z = _half_select(n, q_h, k_h, upper_rows[idx]) * _decay_factor(n, b_blk, row, c0)
                if h % SUBLANES == 0:
                    slab_rows = [r for r0 in range(0, BLOCK, n) for r in range(r0 + h, r0 + n, SUBLANES)]
                else:
                    slab_rows = list(range(0, BLOCK, SUBLANES))
                lhs = jnp.concatenate([z[r:r + SUBLANES] for r in slab_rows], axis=0)
                res = _dot(lhs.astype(_BF16), z.T.astype(_BF16))
                for j, r in enumerate(slab_rows):
                    valid = lvl[r:r + SUBLANES] == idx
                    attn[r // SUBLANES] = jnp.where(valid, res[j * SUBLANES:(j + 1) * SUBLANES],
                                                    attn[r // SUBLANES])
                if idx % 2 == 1:
                    background()
            attn = jnp.concatenate(attn, axis=0)
            diag = jnp.sum(q_h * k_h, axis=-1, keepdims=True)

            s_t = state_sc[hd]
            q_hat = (q_h * jnp.exp2(b_h)).astype(_BF16)
            v_t = v_h.T.astype(_BF16)
            lhs = jnp.concatenate([attn.astype(_BF16), q_hat], axis=1)
            rhs_t = jnp.concatenate([v_t, s_t.astype(_BF16)], axis=1)
            o_h = _dot_nt(lhs, rhs_t) + diag * v_h

            b_last = b_blk[pl.ds(BLOCK - 1, 1), pl.ds(c0, HEAD_DIM)]
            k_hat = (k_h * jnp.exp2(b_last - b_h)).astype(_BF16)
            state_sc[hd] = s_t * jnp.exp2(b_last) + _dot(v_t, k_hat)

            oms = jnp.mean(o_h * o_h, axis=-1, keepdims=True)
            y_b = o_h * lax.rsqrt(oms + EPS) * gnw_ref[...] * _silu(g_h)
            y_sc[yrows, pl.ds(D_GMLP + c0, HEAD_DIM)] = y_b.astype(_BF16)
            background()

    blocks_per_tile = MIX_TILE // BLOCK
    g1 = ada_b[2:3]

    def projection_tasks(h_sc, dst):
        def chunk(c):
            cols = pl.ds(c * MXU_CHUNK, MXU_CHUNK)
            dst[:, cols] = _dot(h_sc[...], w_in_ref[:, cols])
        return [functools.partial(chunk, c) for c in range(D_IN // MXU_CHUNK)]

    def finish_tasks(t):
        trow = pl.ds(t * MIX_TILE, MIX_TILE)

        def chunk(c):
            cols = pl.ds(c * MXU_CHUNK, MXU_CHUNK)
            mix = _dot(y_sc[trow, :], w_out_ref[:, cols])
            o_ref[trow, cols] = x_ref[trow, cols] + g1[:, c * MXU_CHUNK:(c + 1) * MXU_CHUNK] * mix
        return [functools.partial(chunk, c) for c in range(D_MODEL // MXU_CHUNK)]

    def mix_tile(proj_sc, t, tasks):
        background = _Interleaver(tasks, blocks_per_tile * POINTS_PER_BLOCK)
        for i in range(blocks_per_tile):
            mix_block(proj_sc, i * BLOCK, t * MIX_TILE + i * BLOCK,
                      b_sc.at[t * blocks_per_tile + i], background)
        background.flush()

    h_b[...] = _modulated_input(x_ref[pl.ds(MIX_TILE, MIX_TILE), :], ada_b, n1w_ref)
    mix_tile(proj_a, 0, projection_tasks(h_b, proj_b))
    h_a[...] = _modulated_input(xn_ref[...], ada_ref[batch_next], n1w_ref)
    mix_tile(proj_b, 1, finish_tasks(0) + projection_tasks(h_a, proj_a))
    for task in finish_tasks(1):
        task()


def _mixer_call(x, ada, n1w, w_in, w_s, b_s, lnw, lnb, lbraw, gnw, w_out, tri, lvl, w1, w2):
    bsz, seq, _ = x.shape
    step_rows = 2 * MIX_TILE
    steps_per_seq = seq // step_rows
    n_steps = bsz * steps_per_seq
    n_tiles = 2 * n_steps
    x2 = x.reshape(bsz * seq, D_MODEL)
    const2 = lambda g: (0, 0)
    const3 = lambda g: (0, 0, 0)
    resident = dict(pipeline_mode=pl.Buffered(1))
    w1_rows = _cast_rows(w1.shape[0], n_steps)
    w2_rows = _cast_rows(w2.shape[0], n_steps)
    w1_map = lambda g: (jnp.minimum(g, w1.shape[0] // w1_rows - 1), 0)
    w2_map = lambda g: (jnp.minimum(g, w2.shape[0] // w2_rows - 1), 0)
    kernel = functools.partial(_mixer_kernel, steps_per_seq=steps_per_seq, n_steps=n_steps)
    out, w1_bf, w2_bf = pl.pallas_call(
        kernel,
        grid=(n_steps,),
        in_specs=[
            pl.BlockSpec((step_rows, D_MODEL), lambda g: (g, 0)),
            pl.BlockSpec((MIX_TILE, D_MODEL), lambda g: (jnp.minimum(2 * g + 2, n_tiles - 1), 0)),
            pl.BlockSpec((bsz, N_ADA, D_MODEL), const3),
            pl.BlockSpec((1, D_MODEL), const2),
            pl.BlockSpec((D_MODEL, D_IN), const2, **resident),
            pl.BlockSpec((HEADS, BLOCK, BLOCK), const3),
            pl.BlockSpec((HEADS, BLOCK, 1), const3),
            pl.BlockSpec((1, D_GMLP), const2),
            pl.BlockSpec((1, D_GMLP), const2),
            pl.BlockSpec((2, D_HGRN), const2),
            pl.BlockSpec((1, HEAD_DIM), const2),
            pl.BlockSpec((D_MODEL, D_MODEL), const2, **resident),
            pl.BlockSpec((BLOCK, BLOCK), const2),
            pl.BlockSpec((BLOCK, BLOCK), const2),
            pl.BlockSpec((w1_rows, w1.shape[1]), w1_map),
            pl.BlockSpec((w2_rows, w2.shape[1]), w2_map),
        ],
        out_specs=[
            pl.BlockSpec((step_rows, D_MODEL), lambda g: (g, 0)),
            pl.BlockSpec((w1_rows, w1.shape[1]), w1_map),
            pl.BlockSpec((w2_rows, w2.shape[1]), w2_map),
        ],
        out_shape=[
            jax.ShapeDtypeStruct(x2.shape, _F32),
            jax.ShapeDtypeStruct(w1.shape, _BF16),
            jax.ShapeDtypeStruct(w2.shape, _BF16),
        ],
        scratch_shapes=[
            pltpu.VMEM((HEADS, HEAD_DIM, HEAD_DIM), _F32),
            pltpu.VMEM((MIX_TILE, D_IN), _F32),
            pltpu.VMEM((MIX_TILE, D_IN), _F32),
            pltpu.VMEM((MIX_TILE, D_MODEL), _BF16),
            pltpu.VMEM((MIX_TILE, D_MODEL), _BF16),
            pltpu.VMEM((step_rows, D_MODEL), _BF16),
            pltpu.VMEM((step_rows // BLOCK, BLOCK, D_HGRN), _F32),
            pltpu.VMEM((HEADS, BLOCK, BLOCK), _BF16),
            pltpu.VMEM((D_MODEL, D_IN), _BF16),
            pltpu.VMEM((D_MODEL, D_MODEL), _BF16),
        ],
        compiler_params=pltpu.CompilerParams(
            dimension_semantics=("arbitrary",),
            vmem_limit_bytes=VMEM_LIMIT),
        name="mixer",
    )(x2, x2, ada, n1w, w_in, w_s, b_s, lnw, lnb, lbraw, gnw, w_out, tri, lvl, w1, w2)
    return out.reshape(x.shape), w1_bf, w2_bf


def _ffn_kernel(x_ref, ada_ref, n2w_ref, w1_ref, w2_ref, fnw_ref, o_ref):
    ada = ada_ref[0]
    sh2, sc2, g2 = ada[3:4], ada[4:5], ada[5:6]

    def hidden(p):
        x = x_ref[0, pl.ds(p * FFN_PART, FFN_PART), :]
        ms = jnp.mean(x * x, axis=-1, keepdims=True)
        hmod = x * lax.rsqrt(ms + EPS) * n2w_ref[...]
        hmod = (hmod * (1.0 + sc2) + sh2).astype(_BF16)
        gu = _dot(hmod, w1_ref[...])
        return (_silu(gu[:, :D_FF]) * gu[:, D_FF:]).astype(_BF16)

    def finish(p, act):
        rows = pl.ds(p * FFN_PART, FFN_PART)
        x2 = x_ref[0, rows, :] + g2 * _dot(act, w2_ref[...])
        ms2 = jnp.mean(x2 * x2, axis=-1, keepdims=True)
        o_ref[0, rows, :] = x2 * lax.rsqrt(ms2 + EPS) * fnw_ref[...]

    act = hidden(0)
    for p in range(1, FFN_PARTS):
        act_next = hidden(p)
        finish(p - 1, act)
        act = act_next
    finish(FFN_PARTS - 1, act)


def _ffn_call(x, ada, n2w, w1, w2, fnw):
    bsz, seq, _ = x.shape
    tile = FFN_PART * FFN_PARTS
    const2 = lambda b, s: (0, 0)
    resident = dict(pipeline_mode=pl.Buffered(1))
    return pl.pallas_call(
        _ffn_kernel,
        grid=(bsz, seq // tile),
        in_specs=[
            pl.BlockSpec((1, tile, D_MODEL), lambda b, s: (b, s, 0)),
            pl.BlockSpec((1, N_ADA, D_MODEL), lambda b, s: (b, 0, 0)),
            pl.BlockSpec((1, D_MODEL), const2),
            pl.BlockSpec((D_MODEL, 2 * D_FF), const2, **resident),
            pl.BlockSpec((D_FF, D_MODEL), const2, **resident),
            pl.BlockSpec((1, D_MODEL), const2),
        ],
        out_specs=pl.BlockSpec((1, tile, D_MODEL), lambda b, s: (b, s, 0)),
        out_shape=jax.ShapeDtypeStruct(x.shape, _F32),
        compiler_params=pltpu.CompilerParams(
            dimension_semantics=("arbitrary", "arbitrary"),
            vmem_limit_bytes=VMEM_LIMIT),
        name="ffn",
    )(x, ada, n2w, w1, w2, fnw)


def kernel(x, c, w_ada, b_ada, norm1_w, w_in, w_s, b_s, v_ln_w, v_ln_b, lower_bounds,
           gn_w, w_out, norm2_w, w_ffn_in, w_ffn_out, final_norm_w):
    bsz = x.shape[0]
    depth = w_in.shape[0]
    assert depth == 1 and lower_bounds.shape[0] == 2

    tri = jnp.asarray(np.tril(np.ones((BLOCK, BLOCK), np.float32)), _BF16)
    lvl = jnp.asarray(_level_table())

    for l in range(depth):
        ada = _ada_call(c, w_ada[l], b_ada[l][None, :])
        ada = ada.reshape(bsz, N_ADA, D_MODEL)
        x, w1_bf, w2_bf = _mixer_call(
            x, ada, norm1_w[l][None, :], w_in[l], w_s[l],
            b_s[l][:, :, None], v_ln_w[l][None, :], v_ln_b[l][None, :],
            lower_bounds, gn_w[l][None, :], w_out[l], tri, lvl, w_ffn_in[l], w_ffn_out[l])
        x = _ffn_call(x, ada, norm2_w[l][None, :], w1_bf, w2_bf, final_norm_w[None, :])
    return x
```

```python
import functools

import numpy as np
import jax
import jax.numpy as jnp
from jax import lax
from jax.experimental import pallas as pl
from jax.experimental.pallas import tpu as pltpu

D_MODEL = 1024
D_GMLP = 512
D_HGRN = 512
HEADS = 4
HEAD_DIM = 128
BLOCK = 128
GATE_CHUNK = 64
D_IN = 2 * D_GMLP + 4 * D_HGRN
D_FF = 2816
N_ADA = 6
EPS = 1e-6

LEVELS = (128, 64, 32, 16, 8, 4, 2)
DIAG_LEVEL = len(LEVELS)

MIX_TILE = 256
MXU_CHUNK = 256
FFN_PART = 256
FFN_PARTS = 4
VMEM_LIMIT = 56 * 1024 * 1024

_F32 = jnp.float32
_BF16 = jnp.bfloat16


def _dot(a, b):
    return jnp.dot(a, b, preferred_element_type=_F32)


def _dot_nt(a, b):
    return lax.dot_general(a, b, (((1,), (1,)), ((), ())), preferred_element_type=_F32)


def _silu(x):
    return x * jax.nn.sigmoid(x)


def _gelu(x):
    half = 0.5 * x
    return half + half * lax.erf(x * np.float32(1.0 / np.sqrt(2.0)))


def _level_table():
    t = np.arange(BLOCK)[:, None]
    s = np.arange(BLOCK)[None, :]
    table = np.full((BLOCK, BLOCK), -1, np.int32)
    table[t == s] = DIAG_LEVEL
    for idx, n in enumerate(LEVELS):
        h = n // 2
        same = (t // n) == (s // n)
        table[same & ((t % n) >= h) & ((s % n) < h)] = idx
    return table


def _ada_kernel(c_ref, w_ref, b_ref, o_ref):
    c = c_ref[...]
    c_act = _silu(c).astype(_BF16)
    o_ref[...] = _dot(c_act, w_ref[...].astype(_BF16)) + b_ref[...]


def _ada_call(c_pad, w_ada, b_ada):
    n_out = w_ada.shape[1]
    tn = 1536
    return pl.pallas_call(
        _ada_kernel,
        grid=(n_out // tn,),
        in_specs=[
            pl.BlockSpec((c_pad.shape[0], D_MODEL), lambda j: (0, 0)),
            pl.BlockSpec((D_MODEL, tn), lambda j: (0, j)),
            pl.BlockSpec((1, tn), lambda j: (0, j)),
        ],
        out_specs=pl.BlockSpec((c_pad.shape[0], tn), lambda j: (0, j)),
        out_shape=jax.ShapeDtypeStruct((c_pad.shape[0], n_out), _F32),
        compiler_params=pltpu.CompilerParams(dimension_semantics=("arbitrary",)),
        name="ada",
    )(c_pad, w_ada, b_ada)


def _neg_abs(x):
    bits = pltpu.bitcast(x, jnp.uint32) | jnp.uint32(0x80000000)
    return pltpu.bitcast(bits, _F32)


SUBLANES = 8


def _half_select(level_n, q, k, upper):
    h = level_n // 2
    if h % SUBLANES != 0:
        return jnp.where(upper, q, k)
    pieces = []
    for r0 in range(0, BLOCK, level_n):
        pieces += [k[r0:r0 + h], q[r0 + h:r0 + level_n]]
    return jnp.concatenate(pieces, axis=0)


def _decay_factor(level_n, b_sc, row, col0):
    h = level_n // 2

    def ref_rows(r, n_rows):
        return jnp.broadcast_to(b_sc[pl.ds(r, 1), pl.ds(col0, HEAD_DIM)], (n_rows, HEAD_DIM))

    if level_n < SUBLANES:
        lower_block = (row[:SUBLANES] % SUBLANES) < level_n
        pieces = [jnp.where(lower_block, ref_rows(r0 + h - 1, SUBLANES),
                            ref_rows(r0 + level_n + h - 1, SUBLANES))
                  for r0 in range(0, BLOCK, SUBLANES)]
    else:
        pieces = [ref_rows(r0 + h - 1, level_n) for r0 in range(0, BLOCK, level_n)]
    b_ref = pieces[0] if len(pieces) == 1 else jnp.concatenate(pieces, axis=0)
    b_h = b_sc[:, pl.ds(col0, HEAD_DIM)]
    return jnp.exp2(_neg_abs(b_h - b_ref))


def _modulated_input(x, ada_b, n1w_ref):
    sh1, sc1 = ada_b[0:1], ada_b[1:2]
    ms = jnp.mean(x * x, axis=-1, keepdims=True)
    hmod = x * lax.rsqrt(ms + EPS) * n1w_ref[...]
    return (hmod * (1.0 + sc1) + sh1).astype(_BF16)


BF16_ROWS = 16


def _cast_rows(n_rows, n_steps):
    rows = -(-n_rows // n_steps)
    rows = -(-rows // BF16_ROWS) * BF16_ROWS
    while n_rows % rows:
        rows += BF16_ROWS
    return rows


class _Interleaver:
    def __init__(self, tasks, total_weight):
        self._tasks = list(tasks)
        self._total = total_weight
        self._weight = 0
        self._done = 0

    def __call__(self, weight=1):
        self._weight += weight
        due = min(len(self._tasks), self._weight * len(self._tasks) // self._total)
        while self._done < due:
            self._tasks[self._done]()
            self._done += 1

    def flush(self):
        assert self._weight == self._total, (self._weight, self._total)
        while self._done < len(self._tasks):
            self._tasks[self._done]()
            self._done += 1


PRELUDE_WEIGHT = 4
WEIGHT_PER_BLOCK = 2 * PRELUDE_WEIGHT + HEADS * (len(LEVELS) // 2 + 1)


def _mixer_kernel(x_ref, xn_ref, ada_ref, n1w_ref, w_in_f32, ws_ref, bs_ref, lnw_ref, lnb_ref,
                  lbraw_ref, gnw_ref, w_out_f32, tri_ref, lvl_ref, w1_f32, w2_f32,
                  o_ref, w1_bf, w2_bf,
                  state_sc, proj_a, proj_b, h_a, h_b, y_sc, b_sc, wsm_sc, w_in_ref, w_out_ref,
                  *, steps_per_seq, n_steps):
    g = pl.program_id(0)
    batch = g // steps_per_seq
    batch_next = jnp.minimum(g + 1, n_steps - 1) // steps_per_seq
    ada_b = ada_ref[batch]

    w1_bf[...] = w1_f32[...].astype(_BF16)
    w2_bf[...] = w2_f32[...].astype(_BF16)

    @pl.when(g == 0)
    def _():
        w_in_ref[...] = w_in_f32[...].astype(_BF16)
        w_out_ref[...] = w_out_f32[...].astype(_BF16)
        h0 = _modulated_input(x_ref[pl.ds(0, MIX_TILE), :], ada_b, n1w_ref)
        proj_a[...] = _dot(h0, w_in_ref[...])
        col = lax.broadcasted_iota(jnp.int32, (BLOCK, BLOCK), 1)
        rowb = lax.broadcasted_iota(jnp.int32, (BLOCK, BLOCK), 0)
        gate_mask = (rowb // GATE_CHUNK) >= (col // GATE_CHUNK)
        for hd in range(HEADS):
            wsm_sc[hd] = jnp.where(gate_mask, ws_ref[hd], 0.0).astype(_BF16)

    @pl.when(g % steps_per_seq == 0)
    def _():
        state_sc[...] = jnp.zeros_like(state_sc)

    lbraw = lbraw_ref[...]
    lbe = jnp.exp(lbraw - jnp.max(lbraw, axis=0, keepdims=True))
    lb = (lbe / jnp.sum(lbe, axis=0, keepdims=True))[0:1]

    row = lax.broadcasted_iota(jnp.int32, (BLOCK, HEAD_DIM), 0)
    lvl = lvl_ref[...]
    upper_rows = [(row % n) >= (n // 2) for n in LEVELS]

    def head_tail(hd, yrows, b_blk, attn, q_h, k_h, v_h, g_h):
        c0 = hd * HEAD_DIM
        b_h = b_blk[:, pl.ds(c0, HEAD_DIM)]
        diag = jnp.sum(q_h * k_h, axis=-1, keepdims=True)

        s_t = state_sc[hd]
        q_hat = (q_h * jnp.exp2(b_h)).astype(_BF16)
        v_t = v_h.T.astype(_BF16)
        lhs = jnp.concatenate([attn.astype(_BF16), q_hat], axis=1)
        rhs_t = jnp.concatenate([v_t, s_t.astype(_BF16)], axis=1)
        o_h = _dot_nt(lhs, rhs_t) + diag * v_h

        b_last = b_blk[pl.ds(BLOCK - 1, 1), pl.ds(c0, HEAD_DIM)]
        k_hat = (k_h * jnp.exp2(b_last - b_h)).astype(_BF16)
        state_sc[hd] = s_t * jnp.exp2(b_last) + _dot(v_t, k_hat)

        oms = jnp.mean(o_h * o_h, axis=-1, keepdims=True)
        y_b = o_h * lax.rsqrt(oms + EPS) * gnw_ref[...] * _silu(g_h)
        y_sc[yrows, pl.ds(D_GMLP + c0, HEAD_DIM)] = y_b.astype(_BF16)

    def mix_block(proj_sc, prow0, yrow0, b_blk, background, pending):
        rows = pl.ds(prow0, BLOCK)
        yrows = pl.ds(yrow0, BLOCK)

        u = proj_sc[rows, pl.ds(0, D_GMLP)]
        v = proj_sc[rows, pl.ds(D_GMLP, D_GMLP)]
        gv = _gelu(v)
        mu = jnp.mean(gv, axis=-1, keepdims=True)
        cen = gv - mu
        var = jnp.mean(cen * cen, axis=-1, keepdims=True)
        vn = (cen * lax.rsqrt(var + EPS) * lnw_ref[...] + lnb_ref[...]).astype(_BF16)
        gu = _gelu(u)
        background(PRELUDE_WEIGHT)

        o0 = 2 * D_GMLP
        fl = proj_sc[rows, pl.ds(o0 + D_HGRN, D_HGRN)]
        f = lb + (1.0 - lb) * jax.nn.sigmoid(fl)
        logf = jnp.log2(f)
        p_hi = logf.astype(_BF16)
        p_lo = (logf - p_hi.astype(_F32)).astype(_BF16)
        background(PRELUDE_WEIGHT)
        tri = tri_ref[...]
        b_blk[...] = _dot(tri, p_hi) + _dot(tri, p_lo)
        mixed = []
        for hd in range(HEADS):
            mixed.append(_dot(wsm_sc[hd], vn[:, hd * HEAD_DIM:(hd + 1) * HEAD_DIM]) + bs_ref[hd])
        y_a = gu * jnp.concatenate(mixed, axis=-1)
        y_sc[yrows, pl.ds(0, D_GMLP)] = y_a.astype(_BF16)

        for hd in range(HEADS):
            c0 = hd * HEAD_DIM
            q_h = _silu(proj_sc[rows, pl.ds(o0 + c0, HEAD_DIM)])
            f_h = f[:, c0:c0 + HEAD_DIM]
            k_h = 1.0 - f_h
            v_h = proj_sc[rows, pl.ds(o0 + 2 * D_HGRN + c0, HEAD_DIM)]
            g_h = proj_sc[rows, pl.ds(o0 + 3 * D_HGRN + c0, HEAD_DIM)]

            attn = [jnp.zeros((SUBLANES, BLOCK), _F32)] * (BLOCK // SUBLANES)
            for idx, n in enumerate(LEVELS):
                h = n // 2
                if n == 2:
                    z = jnp.where(upper_rows[idx], q_h * f_h, k_h)
                else:
                    z = _half_select(n, q_h, k_h, upper_rows[idx]) * _decay_factor(n, b_blk, row, c0)
                if h % SUBLANES == 0:
                    slab_rows = [r for r0 in range(0, BLOCK, n) for r in range(r0 + h, r0 + n, SUBLANES)]
                else:
                    slab_rows = list(range(0, BLOCK, SUBLANES))
                lhs = jnp.concatenate([z[r:r + SUBLANES] for r in slab_rows], axis=0)
                res = _dot(lhs.astype(_BF16), z.T.astype(_BF16))
                for j, r in enumerate(slab_rows):
                    valid = lvl[r:r + SUBLANES] == idx
                    attn[r // SUBLANES] = jnp.where(valid, res[j * SUBLANES:(j + 1) * SUBLANES],
                                                    attn[r // SUBLANES])
                if idx % 2 == 1:
                    background()
            attn = jnp.concatenate(attn, axis=0)
            if pending is not None:
                pending()
            pending = functools.partial(head_tail, hd, yrows, b_blk, attn, q_h, k_h, v_h, g_h)
            background()
        return pending

    blocks_per_tile = MIX_TILE // BLOCK
    g1 = ada_b[2:3]

    def projection_tasks(h_sc, dst):
        def chunk(c):
            cols = pl.ds(c * MXU_CHUNK, MXU_CHUNK)
            dst[:, cols] = _dot(h_sc[...], w_in_ref[:, cols])
        return [functools.partial(chunk, c) for c in range(D_IN // MXU_CHUNK)]

    def finish_tasks(t):
        trow = pl.ds(t * MIX_TILE, MIX_TILE)

        def chunk(c):
            cols = pl.ds(c * MXU_CHUNK, MXU_CHUNK)
            mix = _dot(y_sc[trow, :], w_out_ref[:, cols])
            o_ref[trow, cols] = x_ref[trow, cols] + g1[:, c * MXU_CHUNK:(c + 1) * MXU_CHUNK] * mix
        return [functools.partial(chunk, c) for c in range(D_MODEL // MXU_CHUNK)]

    def mix_tile(proj_sc, t, tasks):
        background = _Interleaver(tasks, blocks_per_tile * WEIGHT_PER_BLOCK)
        pending = None
        for i in range(blocks_per_tile):
            pending = mix_block(proj_sc, i * BLOCK, t * MIX_TILE + i * BLOCK,
                                b_sc.at[t * blocks_per_tile + i], background, pending)
        pending()
        background.flush()

    h_b[...] = _modulated_input(x_ref[pl.ds(MIX_TILE, MIX_TILE), :], ada_b, n1w_ref)
    mix_tile(proj_a, 0, projection_tasks(h_b, proj_b))
    h_a[...] = _modulated_input(xn_ref[...], ada_ref[batch_next], n1w_ref)
    mix_tile(proj_b, 1, finish_tasks(0) + projection_tasks(h_a, proj_a))
    for task in finish_tasks(1):
        task()


def _mixer_call(x, ada, n1w, w_in, w_s, b_s, lnw, lnb, lbraw, gnw, w_out, tri, lvl, w1, w2):
    bsz, seq, _ = x.shape
    step_rows = 2 * MIX_TILE
    steps_per_seq = seq // step_rows
    n_steps = bsz * steps_per_seq
    n_tiles = 2 * n_steps
    x2 = x.reshape(bsz * seq, D_MODEL)
    const2 = lambda g: (0, 0)
    const3 = lambda g: (0, 0, 0)
    resident = dict(pipeline_mode=pl.Buffered(1))
    w1_rows = _cast_rows(w1.shape[0], n_steps)
    w2_rows = _cast_rows(w2.shape[0], n_steps)
    w1_map = lambda g: (jnp.minimum(g, w1.shape[0] // w1_rows - 1), 0)
    w2_map = lambda g: (jnp.minimum(g, w2.shape[0] // w2_rows - 1), 0)
    kernel = functools.partial(_mixer_kernel, steps_per_seq=steps_per_seq, n_steps=n_steps)
    out, w1_bf, w2_bf = pl.pallas_call(
        kernel,
        grid=(n_steps,),
        in_specs=[
            pl.BlockSpec((step_rows, D_MODEL), lambda g: (g, 0)),
            pl.BlockSpec((MIX_TILE, D_MODEL), lambda g: (jnp.minimum(2 * g + 2, n_tiles - 1), 0)),
            pl.BlockSpec((bsz, N_ADA, D_MODEL), const3),
            pl.BlockSpec((1, D_MODEL), const2),
            pl.BlockSpec((D_MODEL, D_IN), const2, **resident),
            pl.BlockSpec((HEADS, BLOCK, BLOCK), const3),
            pl.BlockSpec((HEADS, BLOCK, 1), const3),
            pl.BlockSpec((1, D_GMLP), const2),
            pl.BlockSpec((1, D_GMLP), const2),
            pl.BlockSpec((2, D_HGRN), const2),
            pl.BlockSpec((1, HEAD_DIM), const2),
            pl.BlockSpec((D_MODEL, D_MODEL), const2, **resident),
            pl.BlockSpec((BLOCK, BLOCK), const2),
            pl.BlockSpec((BLOCK, BLOCK), const2),
            pl.BlockSpec((w1_rows, w1.shape[1]), w1_map),
            pl.BlockSpec((w2_rows, w2.shape[1]), w2_map),
        ],
        out_specs=[
            pl.BlockSpec((step_rows, D_MODEL), lambda g: (g, 0)),
            pl.BlockSpec((w1_rows, w1.shape[1]), w1_map),
            pl.BlockSpec((w2_rows, w2.shape[1]), w2_map),
        ],
        out_shape=[
            jax.ShapeDtypeStruct(x2.shape, _F32),
            jax.ShapeDtypeStruct(w1.shape, _BF16),
            jax.ShapeDtypeStruct(w2.shape, _BF16),
        ],
        scratch_shapes=[
            pltpu.VMEM((HEADS, HEAD_DIM, HEAD_DIM), _F32),
            pltpu.VMEM((MIX_TILE, D_IN), _F32),
            pltpu.VMEM((MIX_TILE, D_IN), _F32),
            pltpu.VMEM((MIX_TILE, D_MODEL), _BF16),
            pltpu.VMEM((MIX_TILE, D_MODEL), _BF16),
            pltpu.VMEM((step_rows, D_MODEL), _BF16),
            pltpu.VMEM((step_rows // BLOCK, BLOCK, D_HGRN), _F32),
            pltpu.VMEM((HEADS, BLOCK, BLOCK), _BF16),
            pltpu.VMEM((D_MODEL, D_IN), _BF16),
            pltpu.VMEM((D_MODEL, D_MODEL), _BF16),
        ],
        compiler_params=pltpu.CompilerParams(
            dimension_semantics=("arbitrary",),
            vmem_limit_bytes=VMEM_LIMIT),
        name="mixer",
    )(x2, x2, ada, n1w, w_in, w_s, b_s, lnw, lnb, lbraw, gnw, w_out, tri, lvl, w1, w2)
    return out.reshape(x.shape), w1_bf, w2_bf


def _ffn_kernel(x_ref, ada_ref, n2w_ref, w1_ref, w2_ref, fnw_ref, o_ref):
    ada = ada_ref[0]
    sh2, sc2, g2 = ada[3:4], ada[4:5], ada[5:6]

    def hidden(p):
        x = x_ref[0, pl.ds(p * FFN_PART, FFN_PART), :]
        ms = jnp.mean(x * x, axis=-1, keepdims=True)
        hmod = x * lax.rsqrt(ms + EPS) * n2w_ref[...]
        hmod = (hmod * (1.0 + sc2) + sh2).astype(_BF16)
        gu = _dot(hmod, w1_ref[...])
        return (_silu(gu[:, :D_FF]) * gu[:, D_FF:]).astype(_BF16)

    def finish(p, act):
        rows = pl.ds(p * FFN_PART, FFN_PART)
        x2 = x_ref[0, rows, :] + g2 * _dot(act, w2_ref[...])
        ms2 = jnp.mean(x2 * x2, axis=-1, keepdims=True)
        o_ref[0, rows, :] = x2 * lax.rsqrt(ms2 + EPS) * fnw_ref[...]

    act = hidden(0)
    for p in range(1, FFN_PARTS):
        act_next = hidden(p)
        finish(p - 1, act)
        act = act_next
    finish(FFN_PARTS - 1, act)


def _ffn_call(x, ada, n2w, w1, w2, fnw):
    bsz, seq, _ = x.shape
    tile = FFN_PART * FFN_PARTS
    const2 = lambda b, s: (0, 0)
    resident = dict(pipeline_mode=pl.Buffered(1))
    return pl.pallas_call(
        _ffn_kernel,
        grid=(bsz, seq // tile),
        in_specs=[
            pl.BlockSpec((1, tile, D_MODEL), lambda b, s: (b, s, 0)),
            pl.BlockSpec((1, N_ADA, D_MODEL), lambda b, s: (b, 0, 0)),
            pl.BlockSpec((1, D_MODEL), const2),
            pl.BlockSpec((D_MODEL, 2 * D_FF), const2, **resident),
            pl.BlockSpec((D_FF, D_MODEL), const2, **resident),
            pl.BlockSpec((1, D_MODEL), const2),
        ],
        out_specs=pl.BlockSpec((1, tile, D_MODEL), lambda b, s: (b, s, 0)),
        out_shape=jax.ShapeDtypeStruct(x.shape, _F32),
        compiler_params=pltpu.CompilerParams(
            dimension_semantics=("arbitrary", "arbitrary"),
            vmem_limit_bytes=VMEM_LIMIT),
        name="ffn",
    )(x, ada, n2w, w1, w2, fnw)


def kernel(x, c, w_ada, b_ada, norm1_w, w_in, w_s, b_s, v_ln_w, v_ln_b, lower_bounds,
           gn_w, w_out, norm2_w, w_ffn_in, w_ffn_out, final_norm_w):
    bsz = x.shape[0]
    depth = w_in.shape[0]
    assert depth == 1 and lower_bounds.shape[0] == 2

    tri = jnp.asarray(np.tril(np.ones((BLOCK, BLOCK), np.float32)), _BF16)
    lvl = jnp.asarray(_level_table())

    for l in range(depth):
        ada = _ada_call(c, w_ada[l], b_ada[l][None, :])
        ada = ada.reshape(bsz, N_ADA, D_MODEL)
        x, w1_bf, w2_bf = _mixer_call(
            x, ada, norm1_w[l][None, :], w_in[l], w_s[l],
            b_s[l][:, :, None], v_ln_w[l][None, :], v_ln_b[l][None, :],
            lower_bounds, gn_w[l][None, :], w_out[l], tri, lvl, w_ffn_in[l], w_ffn_out[l])
        x = _ffn_call(x, ada, norm2_w[l][None, :], w1_bf, w2_bf, final_norm_w[None, :])
    return x
```

```python
import functools

import numpy as np
import jax
import jax.numpy as jnp
from jax import lax
from jax.experimental import pallas as pl
from jax.experimental.pallas import tpu as pltpu

D_MODEL = 1024
D_GMLP = 512
D_HGRN = 512
HEADS = 4
HEAD_DIM = 128
BLOCK = 128
GATE_CHUNK = 64
D_IN = 2 * D_GMLP + 4 * D_HGRN
D_FF = 2816
N_ADA = 6
EPS = 1e-6

LEVELS = (128, 64, 32, 16, 8, 4, 2)
GUARD_LEVELS = (128, 64)
GUARD_BLOCK = 32
GUARD_LOG2_RANGE = 100.0

MIX_TILE = 256
MXU_CHUNK = 256
FFN_PART = 256
FFN_PARTS = 4
VMEM_LIMIT = 56 * 1024 * 1024

_F32 = jnp.float32
_BF16 = jnp.bfloat16


def _dot(a, b):
    return jnp.dot(a, b, preferred_element_type=_F32)


def _dot_nt(a, b):
    return lax.dot_general(a, b, (((1,), (1,)), ((), ())), preferred_element_type=_F32)


def _silu(x):
    return x * jax.nn.sigmoid(x)


def _gelu(x):
    half = 0.5 * x
    return half + half * lax.erf(x * np.float32(1.0 / np.sqrt(2.0)))


def _level_table(levels, guard_block=None):
    t = np.arange(BLOCK)[:, None]
    s = np.arange(BLOCK)[None, :]
    table = np.full((BLOCK, BLOCK), -1, np.int32)
    if guard_block:
        table[((t // guard_block) == (s // guard_block)) & (s <= t)] = len(levels)
    for idx, n in enumerate(levels):
        h = n // 2
        same = (t // n) == (s // n)
        table[same & ((t % n) >= h) & ((s % n) < h)] = idx
    return table


def _ada_kernel(c_ref, w_ref, b_ref, o_ref):
    c = c_ref[...]
    c_act = _silu(c).astype(_BF16)
    o_ref[...] = _dot(c_act, w_ref[...].astype(_BF16)) + b_ref[...]


def _ada_call(c_pad, w_ada, b_ada):
    n_out = w_ada.shape[1]
    tn = 1536
    return pl.pallas_call(
        _ada_kernel,
        grid=(n_out // tn,),
        in_specs=[
            pl.BlockSpec((c_pad.shape[0], D_MODEL), lambda j: (0, 0)),
            pl.BlockSpec((D_MODEL, tn), lambda j: (0, j)),
            pl.BlockSpec((1, tn), lambda j: (0, j)),
        ],
        out_specs=pl.BlockSpec((c_pad.shape[0], tn), lambda j: (0, j)),
        out_shape=jax.ShapeDtypeStruct((c_pad.shape[0], n_out), _F32),
        compiler_params=pltpu.CompilerParams(dimension_semantics=("arbitrary",)),
        name="ada",
    )(c_pad, w_ada, b_ada)


def _neg_abs(x):
    bits = pltpu.bitcast(x, jnp.uint32) | jnp.uint32(0x80000000)
    return pltpu.bitcast(bits, _F32)


SUBLANES = 8


def _half_select(level_n, q, k, upper):
    h = level_n // 2
    if h % SUBLANES != 0:
        return jnp.where(upper, q, k)
    pieces = []
    for r0 in range(0, BLOCK, level_n):
        pieces += [k[r0:r0 + h], q[r0 + h:r0 + level_n]]
    return jnp.concatenate(pieces, axis=0)


def _decay_exponent(level_n, b_sc, row, col0):
    h = level_n // 2

    def ref_rows(r, n_rows):
        return jnp.broadcast_to(b_sc[pl.ds(r, 1), pl.ds(col0, HEAD_DIM)], (n_rows, HEAD_DIM))

    if level_n < SUBLANES:
        lower_block = (row[:SUBLANES] % SUBLANES) < level_n
        pieces = [jnp.where(lower_block, ref_rows(r0 + h - 1, SUBLANES),
                            ref_rows(r0 + level_n + h - 1, SUBLANES))
                  for r0 in range(0, BLOCK, SUBLANES)]
    else:
        pieces = [ref_rows(r0 + h - 1, level_n) for r0 in range(0, BLOCK, level_n)]
    b_ref = pieces[0] if len(pieces) == 1 else jnp.concatenate(pieces, axis=0)
    b_h = b_sc[:, pl.ds(col0, HEAD_DIM)]
    return b_h - b_ref


def _decay_factor(level_n, b_sc, row, col0):
    return jnp.exp2(_neg_abs(_decay_exponent(level_n, b_sc, row, col0)))


def _modulated_input(x, ada_b, n1w_ref):
    sh1, sc1 = ada_b[0:1], ada_b[1:2]
    ms = jnp.mean(x * x, axis=-1, keepdims=True)
    hmod = x * lax.rsqrt(ms + EPS) * n1w_ref[...]
    return (hmod * (1.0 + sc1) + sh1).astype(_BF16)


BF16_ROWS = 16


def _cast_rows(n_rows, n_steps):
    rows = -(-n_rows // n_steps)
    rows = -(-rows // BF16_ROWS) * BF16_ROWS
    while n_rows % rows:
        rows += BF16_ROWS
    return rows


class _Interleaver:
    def __init__(self, tasks, total_weight):
        self._tasks = list(tasks)
        self._total = total_weight
        self._weight = 0
        self._done = 0

    def __call__(self, weight=1):
        self._weight += weight
        due = min(len(self._tasks), self._weight * len(self._tasks) // self._total)
        while self._done < due:
            self._tasks[self._done]()
            self._done += 1

    def flush(self):
        assert self._weight == self._total, (self._weight, self._total)
        while self._done < len(self._tasks):
            self._tasks[self._done]()
            self._done += 1


PRELUDE_WEIGHT = 4
WEIGHT_PER_BLOCK = 2 * PRELUDE_WEIGHT + HEADS * (len(LEVELS) // 2 + 1)


def _mixer_kernel(x_ref, xn_ref, ada_ref, n1w_ref, w_in_f32, ws_ref, bs_ref, lnw_ref, lnb_ref,
                  lbraw_ref, gnw_ref, w_out_f32, tri_ref, lvl_ref, w1_f32, w2_f32,
                  o_ref, w1_bf, w2_bf, range_ref,
                  state_sc, proj_a, proj_b, h_a, h_b, y_sc, b_sc, wsm_sc, w_in_ref, w_out_ref,
                  pre_bf, pre_f32, zl_sc, zt_sc,
                  *, steps_per_seq, n_steps, guarded):
    g = pl.program_id(0)
    batch = g // steps_per_seq
    batch_next = jnp.minimum(g + 1, n_steps - 1) // steps_per_seq
    ada_b = ada_ref[batch]

    w1_bf[...] = w1_f32[...].astype(_BF16)
    w2_bf[...] = w2_f32[...].astype(_BF16)

    @pl.when(g == 0)
    def _():
        w_in_ref[...] = w_in_f32[...].astype(_BF16)
        w_out_ref[...] = w_out_f32[...].astype(_BF16)
        h0 = _modulated_input(x_ref[pl.ds(0, MIX_TILE), :], ada_b, n1w_ref)
        proj_a[...] = _dot(h0, w_in_ref[...])
        col = lax.broadcasted_iota(jnp.int32, (BLOCK, BLOCK), 1)
        rowb = lax.broadcasted_iota(jnp.int32, (BLOCK, BLOCK), 0)
        gate_mask = (rowb // GATE_CHUNK) >= (col // GATE_CHUNK)
        for hd in range(HEADS):
            wsm_sc[hd] = jnp.where(gate_mask, ws_ref[hd], 0.0).astype(_BF16)

    @pl.when(g % steps_per_seq == 0)
    def _():
        state_sc[...] = jnp.zeros_like(state_sc)

    lbraw = lbraw_ref[...]
    lbe = jnp.exp(lbraw - jnp.max(lbraw, axis=0, keepdims=True))
    lb = (lbe / jnp.sum(lbe, axis=0, keepdims=True))[0:1]

    levels = GUARD_LEVELS if guarded else LEVELS
    n_products = len(levels) + (1 if guarded else 0)
    row = lax.broadcasted_iota(jnp.int32, (BLOCK, HEAD_DIM), 0)
    lvl = lvl_ref[...]
    upper_rows = [(row % n) >= (n // 2) for n in levels]
    log2_range = [jnp.zeros((SUBLANES, HEAD_DIM), _F32)]

    o0 = 2 * D_GMLP

    def prelude_elementwise(proj_sc, prow0, par, background):
        rows = pl.ds(prow0, BLOCK)
        u = proj_sc[rows, pl.ds(0, D_GMLP)]
        v = proj_sc[rows, pl.ds(D_GMLP, D_GMLP)]
        gv = _gelu(v)
        mu = jnp.mean(gv, axis=-1, keepdims=True)
        cen = gv - mu
        var = jnp.mean(cen * cen, axis=-1, keepdims=True)
        vn = cen * lax.rsqrt(var + EPS) * lnw_ref[...] + lnb_ref[...]
        pre_bf[par, 0] = vn.astype(_BF16)
        pre_f32[par, 0] = _gelu(u)
        background(PRELUDE_WEIGHT)
        fl = proj_sc[rows, pl.ds(o0 + D_HGRN, D_HGRN)]
        f = lb + (1.0 - lb) * jax.nn.sigmoid(fl)
        logf = jnp.log2(f)
        p_hi = logf.astype(_BF16)
        pre_bf[par, 1] = p_hi
        pre_bf[par, 2] = (logf - p_hi.astype(_F32)).astype(_BF16)
        pre_f32[par, 1] = f
        background(PRELUDE_WEIGHT)

    def prelude_matmuls(yrow0, par, b_blk):
        tri = tri_ref[...]
        b_blk[...] = _dot(tri, pre_bf[par, 1]) + _dot(tri, pre_bf[par, 2])
        mixed = []
        for hd in range(HEADS):
            mixed.append(_dot(wsm_sc[hd], pre_bf[par, 0, :, pl.ds(hd * HEAD_DIM, HEAD_DIM)]) + bs_ref[hd])
        y_a = pre_f32[par, 0] * jnp.concatenate(mixed, axis=-1)
        y_sc[pl.ds(yrow0, BLOCK), pl.ds(0, D_GMLP)] = y_a.astype(_BF16)

    def level_slabs(n):
        h = n // 2
        if h % SUBLANES or n == 1:
            return list(range(0, BLOCK, SUBLANES))
        return [r for r0 in range(0, BLOCK, n) for r in range(r0 + h, r0 + n, SUBLANES)]

    def stage_head(proj_sc, prow0, par, b_blk, hd, slot):
        rows = pl.ds(prow0, BLOCK)
        c0 = hd * HEAD_DIM
        q_h = _silu(proj_sc[rows, pl.ds(o0 + c0, HEAD_DIM)])
        f_h = pre_f32[par, 1, :, pl.ds(c0, HEAD_DIM)]
        k_h = 1.0 - f_h
        for idx, n in enumerate(levels):
            if n == 2:
                z = jnp.where(upper_rows[idx], q_h * f_h, k_h)
            else:
                z = _half_select(n, q_h, k_h, upper_rows[idx]) * _decay_factor(n, b_blk, row, c0)
            slabs = level_slabs(n)
            lhs = jnp.concatenate([z[r:r + SUBLANES] for r in slabs], axis=0)
            zl_sc[slot, idx, pl.ds(0, len(slabs) * SUBLANES), :] = lhs.astype(_BF16)
            zt_sc[slot, idx] = z.T.astype(_BF16)
        if guarded:
            d = _decay_exponent(GUARD_BLOCK, b_blk, row, c0)
            zl_sc[slot, len(levels)] = (q_h * jnp.exp2(d)).astype(_BF16)
            zt_sc[slot, len(levels)] = (k_h * jnp.exp2(-d)).T.astype(_BF16)
            d_abs = jnp.abs(d)
            for r in range(0, BLOCK, SUBLANES):
                log2_range[0] = jnp.maximum(log2_range[0], d_abs[r:r + SUBLANES])
        return q_h, k_h

    def head_levels(slot, background):
        attn = [jnp.zeros((SUBLANES, BLOCK), _F32)] * (BLOCK // SUBLANES)
        for idx in range(n_products):
            slabs = level_slabs(levels[idx]) if idx < len(levels) else level_slabs(1)
            res = _dot(zl_sc[slot, idx, pl.ds(0, len(slabs) * SUBLANES), :], zt_sc[slot, idx])
            for j, r in enumerate(slabs):
                valid = lvl[r:r + SUBLANES] == idx
                attn[r // SUBLANES] = jnp.where(valid, res[j * SUBLANES:(j + 1) * SUBLANES],
                                                attn[r // SUBLANES])
            if idx % 2 == 1:
                background()
        for _ in range(n_products // 2, len(LEVELS) // 2):
            background()
        return jnp.concatenate(attn, axis=0)

    def head_tail(proj_sc, prow0, yrow0, b_blk, hd, attn, q_h, k_h):
        rows = pl.ds(prow0, BLOCK)
        c0 = hd * HEAD_DIM
        v_h = proj_sc[rows, pl.ds(o0 + 2 * D_HGRN + c0, HEAD_DIM)]
        g_h = proj_sc[rows, pl.ds(o0 + 3 * D_HGRN + c0, HEAD_DIM)]
        b_h = b_blk[:, pl.ds(c0, HEAD_DIM)]

        s_t = state_sc[hd]
        q_hat = (q_h * jnp.exp2(b_h)).astype(_BF16)
        v_t = v_h.T.astype(_BF16)
        lhs = jnp.concatenate([attn.astype(_BF16), q_hat], axis=1)
        rhs_t = jnp.concatenate([v_t, s_t.astype(_BF16)], axis=1)
        o_h = _dot_nt(lhs, rhs_t)
        if not guarded:
            o_h = o_h + jnp.sum(q_h * k_h, axis=-1, keepdims=True) * v_h

        b_last = b_blk[pl.ds(BLOCK - 1, 1), pl.ds(c0, HEAD_DIM)]
        k_hat = (k_h * jnp.exp2(b_last - b_h)).astype(_BF16)
        state_sc[hd] = s_t * jnp.exp2(b_last) + _dot(v_t, k_hat)

        oms = jnp.mean(o_h * o_h, axis=-1, keepdims=True)
        y_b = o_h * lax.rsqrt(oms + EPS) * gnw_ref[...] * _silu(g_h)
        y_sc[pl.ds(yrow0, BLOCK), pl.ds(D_GMLP + c0, HEAD_DIM)] = y_b.astype(_BF16)

    blocks_per_tile = MIX_TILE // BLOCK
    g1 = ada_b[2:3]

    def projection_tasks(h_sc, dst):
        def chunk(c):
            cols = pl.ds(c * MXU_CHUNK, MXU_CHUNK)
            dst[:, cols] = _dot(h_sc[...], w_in_ref[:, cols])
        return [functools.partial(chunk, c) for c in range(D_IN // MXU_CHUNK)]

    def finish_tasks(t):
        trow = pl.ds(t * MIX_TILE, MIX_TILE)

        def chunk(c):
            cols = pl.ds(c * MXU_CHUNK, MXU_CHUNK)
            mix = _dot(y_sc[trow, :], w_out_ref[:, cols])
            o_ref[trow, cols] = x_ref[trow, cols] + g1[:, c * MXU_CHUNK:(c + 1) * MXU_CHUNK] * mix
        return [functools.partial(chunk, c) for c in range(D_MODEL // MXU_CHUNK)]

    def mix_tile(proj_sc, t, tasks):
        background = _Interleaver(tasks, blocks_per_tile * WEIGHT_PER_BLOCK)
        heads = [(i, hd) for i in range(blocks_per_tile) for hd in range(HEADS)]

        def blk(i):
            return i * BLOCK, t * MIX_TILE + i * BLOCK, i % 2, b_sc.at[t * blocks_per_tile + i]

        prow0, yrow0, par, b_blk = blk(0)
        prelude_elementwise(proj_sc, prow0, par, background)
        prelude_matmuls(yrow0, par, b_blk)
        staged = stage_head(proj_sc, prow0, par, b_blk, 0, 0)
        pending = None
        for j, (i, hd) in enumerate(heads):
            prow0, yrow0, par, b_blk = blk(i)
            if hd == 1 and i + 1 < blocks_per_tile:
                prelude_elementwise(proj_sc, blk(i + 1)[0], blk(i + 1)[2], background)
            if hd == 2 and i + 1 < blocks_per_tile:
                prelude_matmuls(blk(i + 1)[1], blk(i + 1)[2], blk(i + 1)[3])
            staged_next = None
            if j + 1 < len(heads):
                ni, nhd = heads[j + 1]
                staged_next = stage_head(proj_sc, blk(ni)[0], blk(ni)[2], blk(ni)[3], nhd, (j + 1) % 2)
            attn = head_levels(j % 2, background)
            if pending is not None:
                pending()
            pending = functools.partial(head_tail, proj_sc, prow0, yrow0, b_blk, hd, attn, *staged)
            background()
            staged = staged_next
        pending()
        background.flush()

    h_b[...] = _modulated_input(x_ref[pl.ds(MIX_TILE, MIX_TILE), :], ada_b, n1w_ref)
    mix_tile(proj_a, 0, projection_tasks(h_b, proj_b))
    h_a[...] = _modulated_input(xn_ref[...], ada_ref[batch_next], n1w_ref)
    mix_tile(proj_b, 1, finish_tasks(0) + projection_tasks(h_a, proj_a))
    for task in finish_tasks(1):
        task()
    range_ref[0] = log2_range[0]


def _mixer_call(x, ada, n1w, w_in, w_s, b_s, lnw, lnb, lbraw, gnw, w_out, tri, w1, w2, guarded):
    bsz, seq, _ = x.shape
    step_rows = 2 * MIX_TILE
    steps_per_seq = seq // step_rows
    n_steps = bsz * steps_per_seq
    n_tiles = 2 * n_steps
    x2 = x.reshape(bsz * seq, D_MODEL)
    const2 = lambda g: (0, 0)
    const3 = lambda g: (0, 0, 0)
    resident = dict(pipeline_mode=pl.Buffered(1))
    w1_rows = _cast_rows(w1.shape[0], n_steps)
    w2_rows = _cast_rows(w2.shape[0], n_steps)
    w1_map = lambda g: (jnp.minimum(g, w1.shape[0] // w1_rows - 1), 0)
    w2_map = lambda g: (jnp.minimum(g, w2.shape[0] // w2_rows - 1), 0)
    if guarded:
        lvl = jnp.asarray(_level_table(GUARD_LEVELS, GUARD_BLOCK))
    else:
        lvl = jnp.asarray(_level_table(LEVELS))
    kernel = functools.partial(_mixer_kernel, steps_per_seq=steps_per_seq, n_steps=n_steps,
                               guarded=guarded)
    out, w1_bf, w2_bf, log2_range = pl.pallas_call(
        kernel,
        grid=(n_steps,),
        in_specs=[
            pl.BlockSpec((step_rows, D_MODEL), lambda g: (g, 0)),
            pl.BlockSpec((MIX_TILE, D_MODEL), lambda g: (jnp.minimum(2 * g + 2, n_tiles - 1), 0)),
            pl.BlockSpec((bsz, N_ADA, D_MODEL), const3),
            pl.BlockSpec((1, D_MODEL), const2),
            pl.BlockSpec((D_MODEL, D_IN), const2, **resident),
            pl.BlockSpec((HEADS, BLOCK, BLOCK), const3),
            pl.BlockSpec((HEADS, BLOCK, 1), const3),
            pl.BlockSpec((1, D_GMLP), const2),
            pl.BlockSpec((1, D_GMLP), const2),
            pl.BlockSpec((2, D_HGRN), const2),
            pl.BlockSpec((1, HEAD_DIM), const2),
            pl.BlockSpec((D_MODEL, D_MODEL), const2, **resident),
            pl.BlockSpec((BLOCK, BLOCK), const2),
            pl.BlockSpec((BLOCK, BLOCK), const2),
            pl.BlockSpec((w1_rows, w1.shape[1]), w1_map),
            pl.BlockSpec((w2_rows, w2.shape[1]), w2_map),
        ],
        out_specs=[
            pl.BlockSpec((step_rows, D_MODEL), lambda g: (g, 0)),
            pl.BlockSpec((w1_rows, w1.shape[1]), w1_map),
            pl.BlockSpec((w2_rows, w2.shape[1]), w2_map),
            pl.BlockSpec((1, SUBLANES, HEAD_DIM), lambda g: (g, 0, 0)),
        ],
        out_shape=[
            jax.ShapeDtypeStruct(x2.shape, _F32),
            jax.ShapeDtypeStruct(w1.shape, _BF16),
            jax.ShapeDtypeStruct(w2.shape, _BF16),
            jax.ShapeDtypeStruct((n_steps, SUBLANES, HEAD_DIM), _F32),
        ],
        scratch_shapes=[
            pltpu.VMEM((HEADS, HEAD_DIM, HEAD_DIM), _F32),
            pltpu.VMEM((MIX_TILE, D_IN), _F32),
            pltpu.VMEM((MIX_TILE, D_IN), _F32),
            pltpu.VMEM((MIX_TILE, D_MODEL), _BF16),
            pltpu.VMEM((MIX_TILE, D_MODEL), _BF16),
            pltpu.VMEM((step_rows, D_MODEL), _BF16),
            pltpu.VMEM((step_rows // BLOCK, BLOCK, D_HGRN), _F32),
            pltpu.VMEM((HEADS, BLOCK, BLOCK), _BF16),
            pltpu.VMEM((D_MODEL, D_IN), _BF16),
            pltpu.VMEM((D_MODEL, D_MODEL), _BF16),
            pltpu.VMEM((2, 3, BLOCK, D_HGRN), _BF16),
            pltpu.VMEM((2, 2, BLOCK, D_HGRN), _F32),
            pltpu.VMEM((2, len(LEVELS), BLOCK, HEAD_DIM), _BF16),
            pltpu.VMEM((2, len(LEVELS), HEAD_DIM, BLOCK), _BF16),
        ],
        compiler_params=pltpu.CompilerParams(
            dimension_semantics=("arbitrary",),
            vmem_limit_bytes=VMEM_LIMIT),
        name="mixer",
    )(x2, x2, ada, n1w, w_in, w_s, b_s, lnw, lnb, lbraw, gnw, w_out, tri, lvl, w1, w2)
    return out.reshape(x.shape), w1_bf, w2_bf, log2_range


def _ffn_kernel(x_ref, ada_ref, n2w_ref, w1_ref, w2_ref, fnw_ref, o_ref):
    ada = ada_ref[0]
    sh2, sc2, g2 = ada[3:4], ada[4:5], ada[5:6]

    def hidden(p):
        x = x_ref[0, pl.ds(p * FFN_PART, FFN_PART), :]
        ms = jnp.mean(x * x, axis=-1, keepdims=True)
        hmod = x * lax.rsqrt(ms + EPS) * n2w_ref[...]
        hmod = (hmod * (1.0 + sc2) + sh2).astype(_BF16)
        gu = _dot(hmod, w1_ref[...])
        return (_silu(gu[:, :D_FF]) * gu[:, D_FF:]).astype(_BF16)

    def finish(p, act):
        rows = pl.ds(p * FFN_PART, FFN_PART)
        x2 = x_ref[0, rows, :] + g2 * _dot(act, w2_ref[...])
        ms2 = jnp.mean(x2 * x2, axis=-1, keepdims=True)
        o_ref[0, rows, :] = x2 * lax.rsqrt(ms2 + EPS) * fnw_ref[...]

    act = hidden(0)
    for p in range(1, FFN_PARTS):
        act_next = hidden(p)
        finish(p - 1, act)
        act = act_next
    finish(FFN_PARTS - 1, act)


def _ffn_call(x, ada, n2w, w1, w2, fnw):
    bsz, seq, _ = x.shape
    tile = FFN_PART * FFN_PARTS
    const2 = lambda b, s: (0, 0)
    resident = dict(pipeline_mode=pl.Buffered(1))
    return pl.pallas_call(
        _ffn_kernel,
        grid=(bsz, seq // tile),
        in_specs=[
            pl.BlockSpec((1, tile, D_MODEL), lambda b, s: (b, s, 0)),
            pl.BlockSpec((1, N_ADA, D_MODEL), lambda b, s: (b, 0, 0)),
            pl.BlockSpec((1, D_MODEL), const2),
            pl.BlockSpec((D_MODEL, 2 * D_FF), const2, **resident),
            pl.BlockSpec((D_FF, D_MODEL), const2, **resident),
            pl.BlockSpec((1, D_MODEL), const2),
        ],
        out_specs=pl.BlockSpec((1, tile, D_MODEL), lambda b, s: (b, s, 0)),
        out_shape=jax.ShapeDtypeStruct(x.shape, _F32),
        compiler_params=pltpu.CompilerParams(
            dimension_semantics=("arbitrary", "arbitrary"),
            vmem_limit_bytes=VMEM_LIMIT),
        name="ffn",
    )(x, ada, n2w, w1, w2, fnw)


def kernel(x, c, w_ada, b_ada, norm1_w, w_in, w_s, b_s, v_ln_w, v_ln_b, lower_bounds,
           gn_w, w_out, norm2_w, w_ffn_in, w_ffn_out, final_norm_w):
    bsz = x.shape[0]
    depth = w_in.shape[0]
    assert depth == 1 and lower_bounds.shape[0] == 2

    tri = jnp.asarray(np.tril(np.ones((BLOCK, BLOCK), np.float32)), _BF16)

    for l in range(depth):
        ada = _ada_call(c, w_ada[l], b_ada[l][None, :])
        ada = ada.reshape(bsz, N_ADA, D_MODEL)
        mixer = functools.partial(
            _mixer_call, x, ada, norm1_w[l][None, :], w_in[l], w_s[l],
            b_s[l][:, :, None], v_ln_w[l][None, :], v_ln_b[l][None, :],
            lower_bounds, gn_w[l][None, :], w_out[l], tri, w_ffn_in[l], w_ffn_out[l])
        ffn = functools.partial(_ffn_call, ada=ada, n2w=norm2_w[l][None, :], fnw=final_norm_w[None, :])
        x_guarded, w1_bf, w2_bf, log2_range = mixer(guarded=True)
        in_range = jnp.max(log2_range) <= GUARD_LOG2_RANGE
        x = lax.cond(in_range, lambda: x_guarded, lambda: mixer(guarded=False)[0])
        x = ffn(x, w1=w1_bf, w2=w2_bf)
    return x
```

```python
import functools

import numpy as np
import jax
import jax.numpy as jnp
from jax import lax
from jax.experimental import pallas as pl
from jax.experimental.pallas import tpu as pltpu

D_MODEL = 1024
D_GMLP = 512
D_HGRN = 512
HEADS = 4
HEAD_DIM = 128
BLOCK = 128
GATE_CHUNK = 64
D_IN = 2 * D_GMLP + 4 * D_HGRN
D_FF = 2816
N_ADA = 6
EPS = 1e-6

LEVELS = (128, 64, 32, 16, 8, 4, 2)
GUARD_LEVELS = (128, 64)
GUARD_BLOCK = 32
GUARD_LOG2_RANGE = 100.0

MIX_TILE = 256
MXU_CHUNK = 256
FFN_PART = 256
FFN_PARTS = 4
VMEM_LIMIT = 56 * 1024 * 1024

_F32 = jnp.float32
_BF16 = jnp.bfloat16


def _dot(a, b):
    return jnp.dot(a, b, preferred_element_type=_F32)


def _dot_nt(a, b):
    return lax.dot_general(a, b, (((1,), (1,)), ((), ())), preferred_element_type=_F32)


def _silu(x):
    return x * jax.nn.sigmoid(x)


def _gelu(x):
    half = 0.5 * x
    return half + half * lax.erf(x * np.float32(1.0 / np.sqrt(2.0)))


def _level_table(levels, guard_block=None):
    t = np.arange(BLOCK)[:, None]
    s = np.arange(BLOCK)[None, :]
    table = np.full((BLOCK, BLOCK), -1, np.int32)
    if guard_block:
        table[((t // guard_block) == (s // guard_block)) & (s <= t)] = len(levels)
    for idx, n in enumerate(levels):
        h = n // 2
        same = (t // n) == (s // n)
        table[same & ((t % n) >= h) & ((s % n) < h)] = idx
    return table


def _ada_kernel(c_ref, w_ref, b_ref, o_ref):
    c = c_ref[...]
    c_act = _silu(c).astype(_BF16)
    o_ref[...] = _dot(c_act, w_ref[...].astype(_BF16)) + b_ref[...]


def _ada_call(c_pad, w_ada, b_ada):
    n_out = w_ada.shape[1]
    tn = 1536
    return pl.pallas_call(
        _ada_kernel,
        grid=(n_out // tn,),
        in_specs=[
            pl.BlockSpec((c_pad.shape[0], D_MODEL), lambda j: (0, 0)),
            pl.BlockSpec((D_MODEL, tn), lambda j: (0, j)),
            pl.BlockSpec((1, tn), lambda j: (0, j)),
        ],
        out_specs=pl.BlockSpec((c_pad.shape[0], tn), lambda j: (0, j)),
        out_shape=jax.ShapeDtypeStruct((c_pad.shape[0], n_out), _F32),
        compiler_params=pltpu.CompilerParams(dimension_semantics=("arbitrary",)),
        name="ada",
    )(c_pad, w_ada, b_ada)


def _neg_abs(x):
    bits = pltpu.bitcast(x, jnp.uint32) | jnp.uint32(0x80000000)
    return pltpu.bitcast(bits, _F32)


SUBLANES = 8


def _half_select(level_n, q, k, upper):
    h = level_n // 2
    if h % SUBLANES != 0:
        return jnp.where(upper, q, k)
    pieces = []
    for r0 in range(0, BLOCK, level_n):
        pieces += [k[r0:r0 + h], q[r0 + h:r0 + level_n]]
    return jnp.concatenate(pieces, axis=0)


def _decay_exponent(level_n, b_sc, row, col0):
    h = level_n // 2

    def ref_rows(r, n_rows):
        return jnp.broadcast_to(b_sc[pl.ds(r, 1), pl.ds(col0, HEAD_DIM)], (n_rows, HEAD_DIM))

    if level_n < SUBLANES:
        lower_block = (row[:SUBLANES] % SUBLANES) < level_n
        pieces = [jnp.where(lower_block, ref_rows(r0 + h - 1, SUBLANES),
                            ref_rows(r0 + level_n + h - 1, SUBLANES))
                  for r0 in range(0, BLOCK, SUBLANES)]
    else:
        pieces = [ref_rows(r0 + h - 1, level_n) for r0 in range(0, BLOCK, level_n)]
    b_ref = pieces[0] if len(pieces) == 1 else jnp.concatenate(pieces, axis=0)
    b_h = b_sc[:, pl.ds(col0, HEAD_DIM)]
    return b_h - b_ref


def _decay_factor(level_n, b_sc, row, col0):
    return jnp.exp2(_neg_abs(_decay_exponent(level_n, b_sc, row, col0)))


def _modulated_input(x, ada_b, n1w_ref):
    sh1, sc1 = ada_b[0:1], ada_b[1:2]
    ms = jnp.mean(x * x, axis=-1, keepdims=True)
    hmod = x * lax.rsqrt(ms + EPS) * n1w_ref[...]
    return (hmod * (1.0 + sc1) + sh1).astype(_BF16)


BF16_ROWS = 16


def _cast_rows(n_rows, n_steps):
    rows = -(-n_rows // n_steps)
    rows = -(-rows // BF16_ROWS) * BF16_ROWS
    while n_rows % rows:
        rows += BF16_ROWS
    return rows


class _Interleaver:
    def __init__(self, tasks, total_weight):
        self._tasks = list(tasks)
        self._total = total_weight
        self._weight = 0
        self._done = 0

    def __call__(self, weight=1):
        self._weight += weight
        due = min(len(self._tasks), self._weight * len(self._tasks) // self._total)
        while self._done < due:
            self._tasks[self._done]()
            self._done += 1

    def flush(self):
        assert self._weight == self._total, (self._weight, self._total)
        while self._done < len(self._tasks):
            self._tasks[self._done]()
            self._done += 1


PRELUDE_WEIGHT = 4
WEIGHT_PER_BLOCK = 2 * PRELUDE_WEIGHT + HEADS * (len(LEVELS) // 2 + 1)


def _mixer_kernel(x_ref, xn_ref, ada_ref, n1w_ref, w_in_f32, ws_ref, bs_ref, lnw_ref, lnb_ref,
                  lbraw_ref, gnw_ref, w_out_f32, tri_ref, lvl_ref, w1_f32, w2_f32,
                  o_ref, w1_bf, w2_bf, range_ref,
                  state_sc, proj_a, proj_b, h_a, h_b, y_sc, b_sc, wsm_sc, w_in_ref, w_out_ref,
                  pre_bf, pre_f32, zl_sc, zt_sc,
                  *, steps_per_seq, n_steps, guarded):
    g = pl.program_id(0)
    batch = g // steps_per_seq
    batch_next = jnp.minimum(g + 1, n_steps - 1) // steps_per_seq
    ada_b = ada_ref[batch]

    w1_bf[...] = w1_f32[...].astype(_BF16)
    w2_bf[...] = w2_f32[...].astype(_BF16)

    @pl.when(g == 0)
    def _():
        w_in_ref[...] = w_in_f32[...].astype(_BF16)
        w_out_ref[...] = w_out_f32[...].astype(_BF16)
        h0 = _modulated_input(x_ref[pl.ds(0, MIX_TILE), :], ada_b, n1w_ref)
        proj_a[...] = _dot(h0, w_in_ref[...])
        h_b[...] = _modulated_input(x_ref[pl.ds(MIX_TILE, MIX_TILE), :], ada_b, n1w_ref)
        col = lax.broadcasted_iota(jnp.int32, (BLOCK, BLOCK), 1)
        rowb = lax.broadcasted_iota(jnp.int32, (BLOCK, BLOCK), 0)
        gate_mask = (rowb // GATE_CHUNK) >= (col // GATE_CHUNK)
        for hd in range(HEADS):
            wsm_sc[hd] = jnp.where(gate_mask, ws_ref[hd], 0.0).astype(_BF16)

    @pl.when(g % steps_per_seq == 0)
    def _():
        state_sc[...] = jnp.zeros_like(state_sc)

    lbraw = lbraw_ref[...]
    lbe = jnp.exp(lbraw - jnp.max(lbraw, axis=0, keepdims=True))
    lb = (lbe / jnp.sum(lbe, axis=0, keepdims=True))[0:1]

    levels = GUARD_LEVELS if guarded else LEVELS
    n_products = len(levels) + (1 if guarded else 0)
    row = lax.broadcasted_iota(jnp.int32, (BLOCK, HEAD_DIM), 0)
    lvl = lvl_ref[...]
    upper_rows = [(row % n) >= (n // 2) for n in levels]
    log2_range = [jnp.zeros((SUBLANES, HEAD_DIM), _F32)]

    o0 = 2 * D_GMLP

    def prelude_elementwise(proj_sc, prow0, par, background):
        rows = pl.ds(prow0, BLOCK)
        u = proj_sc[rows, pl.ds(0, D_GMLP)]
        v = proj_sc[rows, pl.ds(D_GMLP, D_GMLP)]
        gv = _gelu(v)
        mu = jnp.mean(gv, axis=-1, keepdims=True)
        cen = gv - mu
        var = jnp.mean(cen * cen, axis=-1, keepdims=True)
        vn = cen * lax.rsqrt(var + EPS) * lnw_ref[...] + lnb_ref[...]
        pre_bf[par, 0] = vn.astype(_BF16)
        pre_f32[par, 0] = _gelu(u)
        background(PRELUDE_WEIGHT)
        fl = proj_sc[rows, pl.ds(o0 + D_HGRN, D_HGRN)]
        f = lb + (1.0 - lb) * jax.nn.sigmoid(fl)
        logf = jnp.log2(f)
        p_hi = logf.astype(_BF16)
        pre_bf[par, 1] = p_hi
        pre_bf[par, 2] = (logf - p_hi.astype(_F32)).astype(_BF16)
        pre_f32[par, 1] = f
        background(PRELUDE_WEIGHT)

    def prelude_matmuls(yrow0, par, b_blk):
        tri = tri_ref[...]
        b_blk[...] = _dot(tri, pre_bf[par, 1]) + _dot(tri, pre_bf[par, 2])
        mixed = []
        for hd in range(HEADS):
            mixed.append(_dot(wsm_sc[hd], pre_bf[par, 0, :, pl.ds(hd * HEAD_DIM, HEAD_DIM)]) + bs_ref[hd])
        y_a = pre_f32[par, 0] * jnp.concatenate(mixed, axis=-1)
        y_sc[pl.ds(yrow0, BLOCK), pl.ds(0, D_GMLP)] = y_a.astype(_BF16)

    def level_slabs(n):
        h = n // 2
        if h % SUBLANES or n == 1:
            return list(range(0, BLOCK, SUBLANES))
        return [r for r0 in range(0, BLOCK, n) for r in range(r0 + h, r0 + n, SUBLANES)]

    def stage_head(proj_sc, prow0, par, b_blk, hd, slot):
        rows = pl.ds(prow0, BLOCK)
        c0 = hd * HEAD_DIM
        q_h = _silu(proj_sc[rows, pl.ds(o0 + c0, HEAD_DIM)])
        f_h = pre_f32[par, 1, :, pl.ds(c0, HEAD_DIM)]
        k_h = 1.0 - f_h
        for idx, n in enumerate(levels):
            if n == 2:
                z = jnp.where(upper_rows[idx], q_h * f_h, k_h)
            else:
                z = _half_select(n, q_h, k_h, upper_rows[idx]) * _decay_factor(n, b_blk, row, c0)
            slabs = level_slabs(n)
            lhs = jnp.concatenate([z[r:r + SUBLANES] for r in slabs], axis=0)
            zl_sc[slot, idx, pl.ds(0, len(slabs) * SUBLANES), :] = lhs.astype(_BF16)
            zt_sc[slot, idx] = z.T.astype(_BF16)
        if guarded:
            d = _decay_exponent(GUARD_BLOCK, b_blk, row, c0)
            zl_sc[slot, len(levels)] = (q_h * jnp.exp2(d)).astype(_BF16)
            zt_sc[slot, len(levels)] = (k_h * jnp.exp2(-d)).T.astype(_BF16)
            d_abs = jnp.abs(d)
            for r in range(0, BLOCK, SUBLANES):
                log2_range[0] = jnp.maximum(log2_range[0], d_abs[r:r + SUBLANES])
        return q_h, k_h

    def head_levels(slot, background):
        attn = [jnp.zeros((SUBLANES, BLOCK), _F32)] * (BLOCK // SUBLANES)
        for idx in range(n_products):
            slabs = level_slabs(levels[idx]) if idx < len(levels) else level_slabs(1)
            res = _dot(zl_sc[slot, idx, pl.ds(0, len(slabs) * SUBLANES), :], zt_sc[slot, idx])
            for j, r in enumerate(slabs):
                valid = lvl[r:r + SUBLANES] == idx
                attn[r // SUBLANES] = jnp.where(valid, res[j * SUBLANES:(j + 1) * SUBLANES],
                                                attn[r // SUBLANES])
            if idx % 2 == 1:
                background()
        for _ in range(n_products // 2, len(LEVELS) // 2):
            background()
        return jnp.concatenate(attn, axis=0)

    def head_tail(proj_sc, prow0, yrow0, b_blk, hd, attn, q_h, k_h):
        rows = pl.ds(prow0, BLOCK)
        c0 = hd * HEAD_DIM
        v_h = proj_sc[rows, pl.ds(o0 + 2 * D_HGRN + c0, HEAD_DIM)]
        g_h = proj_sc[rows, pl.ds(o0 + 3 * D_HGRN + c0, HEAD_DIM)]
        b_h = b_blk[:, pl.ds(c0, HEAD_DIM)]

        s_t = state_sc[hd]
        q_hat = (q_h * jnp.exp2(b_h)).astype(_BF16)
        v_t = v_h.T.astype(_BF16)
        lhs = jnp.concatenate([attn.astype(_BF16), q_hat], axis=1)
        rhs_t = jnp.concatenate([v_t, s_t.astype(_BF16)], axis=1)
        o_h = _dot_nt(lhs, rhs_t)
        if not guarded:
            o_h = o_h + jnp.sum(q_h * k_h, axis=-1, keepdims=True) * v_h

        b_last = b_blk[pl.ds(BLOCK - 1, 1), pl.ds(c0, HEAD_DIM)]
        k_hat = (k_h * jnp.exp2(b_last - b_h)).astype(_BF16)
        state_sc[hd] = s_t * jnp.exp2(b_last) + _dot(v_t, k_hat)

        oms = jnp.mean(o_h * o_h, axis=-1, keepdims=True)
        y_b = o_h * lax.rsqrt(oms + EPS) * gnw_ref[...] * _silu(g_h)
        y_sc[pl.ds(yrow0, BLOCK), pl.ds(D_GMLP + c0, HEAD_DIM)] = y_b.astype(_BF16)

    blocks_per_tile = MIX_TILE // BLOCK
    g1 = ada_b[2:3]

    def projection_tasks(h_sc, dst):
        def chunk(c):
            cols = pl.ds(c * MXU_CHUNK, MXU_CHUNK)
            dst[:, cols] = _dot(h_sc[...], w_in_ref[:, cols])
        return [functools.partial(chunk, c) for c in range(D_IN // MXU_CHUNK)]

    def finish_tasks(t):
        trow = pl.ds(t * MIX_TILE, MIX_TILE)

        def chunk(c):
            cols = pl.ds(c * MXU_CHUNK, MXU_CHUNK)
            mix = _dot(y_sc[trow, :], w_out_ref[:, cols])
            o_ref[trow, cols] = x_ref[trow, cols] + g1[:, c * MXU_CHUNK:(c + 1) * MXU_CHUNK] * mix
        return [functools.partial(chunk, c) for c in range(D_MODEL // MXU_CHUNK)]

    def mix_tile(proj_sc, t, tasks):
        background = _Interleaver(tasks, blocks_per_tile * WEIGHT_PER_BLOCK)
        heads = [(i, hd) for i in range(blocks_per_tile) for hd in range(HEADS)]

        def blk(i):
            return i * BLOCK, t * MIX_TILE + i * BLOCK, i % 2, b_sc.at[t * blocks_per_tile + i]

        prow0, yrow0, par, b_blk = blk(0)
        prelude_elementwise(proj_sc, prow0, par, background)
        prelude_matmuls(yrow0, par, b_blk)
        staged = stage_head(proj_sc, prow0, par, b_blk, 0, 0)
        pending = None
        for j, (i, hd) in enumerate(heads):
            prow0, yrow0, par, b_blk = blk(i)
            if hd == 1 and i + 1 < blocks_per_tile:
                prelude_elementwise(proj_sc, blk(i + 1)[0], blk(i + 1)[2], background)
            if hd == 2 and i + 1 < blocks_per_tile:
                prelude_matmuls(blk(i + 1)[1], blk(i + 1)[2], blk(i + 1)[3])
            staged_next = None
            if j + 1 < len(heads):
                ni, nhd = heads[j + 1]
                staged_next = stage_head(proj_sc, blk(ni)[0], blk(ni)[2], blk(ni)[3], nhd, (j + 1) % 2)
            attn = head_levels(j % 2, background)
            if pending is not None:
                pending()
            pending = functools.partial(head_tail, proj_sc, prow0, yrow0, b_blk, hd, attn, *staged)
            background()
            staged = staged_next
        pending()
        background.flush()

    mix_tile(proj_a, 0, projection_tasks(h_b, proj_b))
    ada_next = ada_ref[batch_next]
    h_a[...] = _modulated_input(xn_ref[pl.ds(0, MIX_TILE), :], ada_next, n1w_ref)
    mix_tile(proj_b, 1, finish_tasks(0) + projection_tasks(h_a, proj_a))
    h_b[...] = _modulated_input(xn_ref[pl.ds(MIX_TILE, MIX_TILE), :], ada_next, n1w_ref)
    for task in finish_tasks(1):
        task()
    range_ref[0] = log2_range[0]


def _mixer_call(x, ada, n1w, w_in, w_s, b_s, lnw, lnb, lbraw, gnw, w_out, tri, w1, w2, guarded):
    bsz, seq, _ = x.shape
    step_rows = 2 * MIX_TILE
    steps_per_seq = seq // step_rows
    n_steps = bsz * steps_per_seq
    x2 = x.reshape(bsz * seq, D_MODEL)
    const2 = lambda g: (0, 0)
    const3 = lambda g: (0, 0, 0)
    resident = dict(pipeline_mode=pl.Buffered(1))
    w1_rows = _cast_rows(w1.shape[0], n_steps)
    w2_rows = _cast_rows(w2.shape[0], n_steps)
    w1_map = lambda g: (jnp.minimum(g, w1.shape[0] // w1_rows - 1), 0)
    w2_map = lambda g: (jnp.minimum(g, w2.shape[0] // w2_rows - 1), 0)
    if guarded:
        lvl = jnp.asarray(_level_table(GUARD_LEVELS, GUARD_BLOCK))
    else:
        lvl = jnp.asarray(_level_table(LEVELS))
    kernel = functools.partial(_mixer_kernel, steps_per_seq=steps_per_seq, n_steps=n_steps,
                               guarded=guarded)
    out, w1_bf, w2_bf, log2_range = pl.pallas_call(
        kernel,
        grid=(n_steps,),
        in_specs=[
            pl.BlockSpec((step_rows, D_MODEL), lambda g: (g, 0)),
            pl.BlockSpec((step_rows, D_MODEL), lambda g: (jnp.minimum(g + 1, n_steps - 1), 0)),
            pl.BlockSpec((bsz, N_ADA, D_MODEL), const3),
            pl.BlockSpec((1, D_MODEL), const2),
            pl.BlockSpec((D_MODEL, D_IN), const2, **resident),
            pl.BlockSpec((HEADS, BLOCK, BLOCK), const3),
            pl.BlockSpec((HEADS, BLOCK, 1), const3),
            pl.BlockSpec((1, D_GMLP), const2),
            pl.BlockSpec((1, D_GMLP), const2),
            pl.BlockSpec((2, D_HGRN), const2),
            pl.BlockSpec((1, HEAD_DIM), const2),
            pl.BlockSpec((D_MODEL, D_MODEL), const2, **resident),
            pl.BlockSpec((BLOCK, BLOCK), const2),
            pl.BlockSpec((BLOCK, BLOCK), const2),
            pl.BlockSpec((w1_rows, w1.shape[1]), w1_map),
            pl.BlockSpec((w2_rows, w2.shape[1]), w2_map),
        ],
        out_specs=[
            pl.BlockSpec((step_rows, D_MODEL), lambda g: (g, 0)),
            pl.BlockSpec((w1_rows, w1.shape[1]), w1_map),
            pl.BlockSpec((w2_rows, w2.shape[1]), w2_map),
            pl.BlockSpec((1, SUBLANES, HEAD_DIM), lambda g: (g, 0, 0)),
        ],
        out_shape=[
            jax.ShapeDtypeStruct(x2.shape, _F32),
            jax.ShapeDtypeStruct(w1.shape, _BF16),
            jax.ShapeDtypeStruct(w2.shape, _BF16),
            jax.ShapeDtypeStruct((n_steps, SUBLANES, HEAD_DIM), _F32),
        ],
        scratch_shapes=[
            pltpu.VMEM((HEADS, HEAD_DIM, HEAD_DIM), _F32),
            pltpu.VMEM((MIX_TILE, D_IN), _F32),
            pltpu.VMEM((MIX_TILE, D_IN), _F32),
            pltpu.VMEM((MIX_TILE, D_MODEL), _BF16),
            pltpu.VMEM((MIX_TILE, D_MODEL), _BF16),
            pltpu.VMEM((step_rows, D_MODEL), _BF16),
            pltpu.VMEM((step_rows // BLOCK, BLOCK, D_HGRN), _F32),
            pltpu.VMEM((HEADS, BLOCK, BLOCK), _BF16),
            pltpu.VMEM((D_MODEL, D_IN), _BF16),
            pltpu.VMEM((D_MODEL, D_MODEL), _BF16),
            pltpu.VMEM((2, 3, BLOCK, D_HGRN), _BF16),
            pltpu.VMEM((2, 2, BLOCK, D_HGRN), _F32),
            pltpu.VMEM((2, len(LEVELS), BLOCK, HEAD_DIM), _BF16),
            pltpu.VMEM((2, len(LEVELS), HEAD_DIM, BLOCK), _BF16),
        ],
        compiler_params=pltpu.CompilerParams(
            dimension_semantics=("arbitrary",),
            vmem_limit_bytes=VMEM_LIMIT),
        name="mixer",
    )(x2, x2, ada, n1w, w_in, w_s, b_s, lnw, lnb, lbraw, gnw, w_out, tri, lvl, w1, w2)
    return out.reshape(x.shape), w1_bf, w2_bf, log2_range


def _ffn_kernel(x_ref, ada_ref, n2w_ref, w1_ref, w2_ref, fnw_ref, o_ref):
    ada = ada_ref[0]
    sh2, sc2, g2 = ada[3:4], ada[4:5], ada[5:6]

    def hidden(p):
        x = x_ref[0, pl.ds(p * FFN_PART, FFN_PART), :]
        ms = jnp.mean(x * x, axis=-1, keepdims=True)
        hmod = x * lax.rsqrt(ms + EPS) * n2w_ref[...]
        hmod = (hmod * (1.0 + sc2) + sh2).astype(_BF16)
        gu = _dot(hmod, w1_ref[...])
        return (_silu(gu[:, :D_FF]) * gu[:, D_FF:]).astype(_BF16)

    def finish(p, act):
        rows = pl.ds(p * FFN_PART, FFN_PART)
        x2 = x_ref[0, rows, :] + g2 * _dot(act, w2_ref[...])
        ms2 = jnp.mean(x2 * x2, axis=-1, keepdims=True)
        o_ref[0, rows, :] = x2 * lax.rsqrt(ms2 + EPS) * fnw_ref[...]

    act = hidden(0)
    for p in range(1, FFN_PARTS):
        act_next = hidden(p)
        finish(p - 1, act)
        act = act_next
    finish(FFN_PARTS - 1, act)


def _ffn_call(x, ada, n2w, w1, w2, fnw):
    bsz, seq, _ = x.shape
    tile = FFN_PART * FFN_PARTS
    const2 = lambda b, s: (0, 0)
    resident = dict(pipeline_mode=pl.Buffered(1))
    return pl.pallas_call(
        _ffn_kernel,
        grid=(bsz, seq // tile),
        in_specs=[
            pl.BlockSpec((1, tile, D_MODEL), lambda b, s: (b, s, 0)),
            pl.BlockSpec((1, N_ADA, D_MODEL), lambda b, s: (b, 0, 0)),
            pl.BlockSpec((1, D_MODEL), const2),
            pl.BlockSpec((D_MODEL, 2 * D_FF), const2, **resident),
            pl.BlockSpec((D_FF, D_MODEL), const2, **resident),
            pl.BlockSpec((1, D_MODEL), const2),
        ],
        out_specs=pl.BlockSpec((1, tile, D_MODEL), lambda b, s: (b, s, 0)),
        out_shape=jax.ShapeDtypeStruct(x.shape, _F32),
        compiler_params=pltpu.CompilerParams(
            dimension_semantics=("arbitrary", "arbitrary"),
            vmem_limit_bytes=VMEM_LIMIT),
        name="ffn",
    )(x, ada, n2w, w1, w2, fnw)


def kernel(x, c, w_ada, b_ada, norm1_w, w_in, w_s, b_s, v_ln_w, v_ln_b, lower_bounds,
           gn_w, w_out, norm2_w, w_ffn_in, w_ffn_out, final_norm_w):
    bsz = x.shape[0]
    depth = w_in.shape[0]
    assert depth == 1 and lower_bounds.shape[0] == 2

    tri = jnp.asarray(np.tril(np.ones((BLOCK, BLOCK), np.float32)), _BF16)

    for l in range(depth):
        ada = _ada_call(c, w_ada[l], b_ada[l][None, :])
        ada = ada.reshape(bsz, N_ADA, D_MODEL)
        mixer = functools.partial(
            _mixer_call, x, ada, norm1_w[l][None, :], w_in[l], w_s[l],
            b_s[l][:, :, None], v_ln_w[l][None, :], v_ln_b[l][None, :],
            lower_bounds, gn_w[l][None, :], w_out[l], tri, w_ffn_in[l], w_ffn_out[l])
        ffn = functools.partial(_ffn_call, ada=ada, n2w=norm2_w[l][None, :], fnw=final_norm_w[None, :])
        x_guarded, w1_bf, w2_bf, log2_range = mixer(guarded=True)
        in_range = jnp.max(log2_range) <= GUARD_LOG2_RANGE
        x = lax.cond(in_range, lambda: x_guarded, lambda: mixer(guarded=False)[0])
        x = ffn(x, w1=w1_bf, w2=w2_bf)
    return x
```

```python
import functools

import numpy as np
import jax
import jax.numpy as jnp
from jax import lax
from jax.experimental import pallas as pl
from jax.experimental.pallas import tpu as pltpu

D_MODEL = 1024
D_GMLP = 512
D_HGRN = 512
HEADS = 4
HEAD_DIM = 128
BLOCK = 128
GATE_CHUNK = 64
D_IN = 2 * D_GMLP + 4 * D_HGRN
D_FF = 2816
N_ADA = 6
EPS = 1e-6

LEVELS = (128, 64, 32, 16, 8, 4, 2)
GUARD_LEVELS = (128, 64)
GUARD_BLOCK = 32
GUARD_LOG2_RANGE = 100.0

MIX_TILE = 256
MXU_CHUNK = 256
FFN_PART = 256
FFN_PARTS = 4
VMEM_LIMIT = 56 * 1024 * 1024

_F32 = jnp.float32
_BF16 = jnp.bfloat16


def _dot(a, b):
    return jnp.dot(a, b, preferred_element_type=_F32)


def _dot_nt(a, b):
    return lax.dot_general(a, b, (((1,), (1,)), ((), ())), preferred_element_type=_F32)


def _silu(x):
    return x * jax.nn.sigmoid(x)


def _gelu(x):
    half = 0.5 * x
    return half + half * lax.erf(x * np.float32(1.0 / np.sqrt(2.0)))


def _level_table(levels, guard_block=None):
    t = np.arange(BLOCK)[:, None]
    s = np.arange(BLOCK)[None, :]
    table = np.full((BLOCK, BLOCK), -1, np.int32)
    if guard_block:
        table[((t // guard_block) == (s // guard_block)) & (s <= t)] = len(levels)
    for idx, n in enumerate(levels):
        h = n // 2
        same = (t // n) == (s // n)
        table[same & ((t % n) >= h) & ((s % n) < h)] = idx
    return table


def _ada_kernel(c_ref, w_ref, b_ref, o_ref):
    c = c_ref[...]
    c_act = _silu(c).astype(_BF16)
    o_ref[...] = _dot(c_act, w_ref[...].astype(_BF16)) + b_ref[...]


def _ada_call(c_pad, w_ada, b_ada):
    n_out = w_ada.shape[1]
    tn = 1536
    return pl.pallas_call(
        _ada_kernel,
        grid=(n_out // tn,),
        in_specs=[
            pl.BlockSpec((c_pad.shape[0], D_MODEL), lambda j: (0, 0)),
            pl.BlockSpec((D_MODEL, tn), lambda j: (0, j)),
            pl.BlockSpec((1, tn), lambda j: (0, j)),
        ],
        out_specs=pl.BlockSpec((c_pad.shape[0], tn), lambda j: (0, j)),
        out_shape=jax.ShapeDtypeStruct((c_pad.shape[0], n_out), _F32),
        compiler_params=pltpu.CompilerParams(dimension_semantics=("arbitrary",)),
        name="ada",
    )(c_pad, w_ada, b_ada)


def _neg_abs(x):
    bits = pltpu.bitcast(x, jnp.uint32) | jnp.uint32(0x80000000)
    return pltpu.bitcast(bits, _F32)


SUBLANES = 8


def _half_select(level_n, q, k, upper):
    h = level_n // 2
    if h % SUBLANES != 0:
        return jnp.where(upper, q, k)
    pieces = []
    for r0 in range(0, BLOCK, level_n):
        pieces += [k[r0:r0 + h], q[r0 + h:r0 + level_n]]
    return jnp.concatenate(pieces, axis=0)


def _decay_exponent(level_n, b_sc, row, col0):
    h = level_n // 2

    def ref_rows(r, n_rows):
        return jnp.broadcast_to(b_sc[pl.ds(r, 1), pl.ds(col0, HEAD_DIM)], (n_rows, HEAD_DIM))

    if level_n < SUBLANES:
        lower_block = (row[:SUBLANES] % SUBLANES) < level_n
        pieces = [jnp.where(lower_block, ref_rows(r0 + h - 1, SUBLANES),
                            ref_rows(r0 + level_n + h - 1, SUBLANES))
                  for r0 in range(0, BLOCK, SUBLANES)]
    else:
        pieces = [ref_rows(r0 + h - 1, level_n) for r0 in range(0, BLOCK, level_n)]
    b_ref = pieces[0] if len(pieces) == 1 else jnp.concatenate(pieces, axis=0)
    b_h = b_sc[:, pl.ds(col0, HEAD_DIM)]
    return b_h - b_ref


def _decay_factor(level_n, b_sc, row, col0):
    return jnp.exp2(_neg_abs(_decay_exponent(level_n, b_sc, row, col0)))


def _modulated_input(x, ada_b, n1w_ref):
    sh1, sc1 = ada_b[0:1], ada_b[1:2]
    ms = jnp.mean(x * x, axis=-1, keepdims=True)
    hmod = x * lax.rsqrt(ms + EPS) * n1w_ref[...]
    return (hmod * (1.0 + sc1) + sh1).astype(_BF16)


BF16_ROWS = 16


def _cast_rows(n_rows, n_steps):
    rows = -(-n_rows // n_steps)
    rows = -(-rows // BF16_ROWS) * BF16_ROWS
    while n_rows % rows:
        rows += BF16_ROWS
    return rows


class _Interleaver:
    def __init__(self, tasks, total_weight):
        self._tasks = list(tasks)
        self._total = total_weight
        self._weight = 0
        self._done = 0

    def __call__(self, weight=1):
        self._weight += weight
        due = min(len(self._tasks), self._weight * len(self._tasks) // self._total)
        while self._done < due:
            self._tasks[self._done]()
            self._done += 1

    def flush(self):
        assert self._weight == self._total, (self._weight, self._total)
        while self._done < len(self._tasks):
            self._tasks[self._done]()
            self._done += 1


PRELUDE_WEIGHT = 4
WEIGHT_PER_BLOCK = 2 * PRELUDE_WEIGHT + HEADS * (len(LEVELS) // 2 + 1)


def _mixer_kernel(x_ref, xn_ref, ada_ref, n1w_ref, w_in_f32, ws_ref, bs_ref, lnw_ref, lnb_ref,
                  lbraw_ref, gnw_ref, w_out_f32, tri_ref, lvl_ref, w1_f32, w2_f32,
                  o_ref, w1_bf, w2_bf,
                  state_sc, proj_a, proj_b, h_a, h_b, y_sc, b_sc, wsm_sc, w_in_ref, w_out_ref,
                  pre_bf, pre_f32, zl_sc, zt_sc,
                  *, steps_per_seq, n_steps, guarded):
    g = pl.program_id(0)
    batch = g // steps_per_seq
    batch_next = jnp.minimum(g + 1, n_steps - 1) // steps_per_seq
    ada_b = ada_ref[batch]

    w1_bf[...] = w1_f32[...].astype(_BF16)
    w2_bf[...] = w2_f32[...].astype(_BF16)

    @pl.when(g == 0)
    def _():
        w_in_ref[...] = w_in_f32[...].astype(_BF16)
        w_out_ref[...] = w_out_f32[...].astype(_BF16)
        h0 = _modulated_input(x_ref[pl.ds(0, MIX_TILE), :], ada_b, n1w_ref)
        proj_a[...] = _dot(h0, w_in_ref[...])
        h_b[...] = _modulated_input(x_ref[pl.ds(MIX_TILE, MIX_TILE), :], ada_b, n1w_ref)
        col = lax.broadcasted_iota(jnp.int32, (BLOCK, BLOCK), 1)
        rowb = lax.broadcasted_iota(jnp.int32, (BLOCK, BLOCK), 0)
        gate_mask = (rowb // GATE_CHUNK) >= (col // GATE_CHUNK)
        for hd in range(HEADS):
            wsm_sc[hd] = jnp.where(gate_mask, ws_ref[hd], 0.0).astype(_BF16)

    @pl.when(g % steps_per_seq == 0)
    def _():
        state_sc[...] = jnp.zeros_like(state_sc)

    lbraw = lbraw_ref[...]
    lbe = jnp.exp(lbraw - jnp.max(lbraw, axis=0, keepdims=True))
    lb = (lbe / jnp.sum(lbe, axis=0, keepdims=True))[0:1]

    levels = GUARD_LEVELS if guarded else LEVELS
    n_products = len(levels) + (1 if guarded else 0)
    row = lax.broadcasted_iota(jnp.int32, (BLOCK, HEAD_DIM), 0)
    lvl = lvl_ref[...]
    upper_rows = [(row % n) >= (n // 2) for n in levels]

    o0 = 2 * D_GMLP

    def prelude_elementwise(proj_sc, prow0, par, background):
        rows = pl.ds(prow0, BLOCK)
        u = proj_sc[rows, pl.ds(0, D_GMLP)]
        v = proj_sc[rows, pl.ds(D_GMLP, D_GMLP)]
        gv = _gelu(v)
        mu = jnp.mean(gv, axis=-1, keepdims=True)
        cen = gv - mu
        var = jnp.mean(cen * cen, axis=-1, keepdims=True)
        vn = cen * lax.rsqrt(var + EPS) * lnw_ref[...] + lnb_ref[...]
        pre_bf[par, 0] = vn.astype(_BF16)
        pre_f32[par, 0] = _gelu(u)
        background(PRELUDE_WEIGHT)
        fl = proj_sc[rows, pl.ds(o0 + D_HGRN, D_HGRN)]
        f = lb + (1.0 - lb) * jax.nn.sigmoid(fl)
        logf = jnp.log2(f)
        p_hi = logf.astype(_BF16)
        pre_bf[par, 1] = p_hi
        pre_bf[par, 2] = (logf - p_hi.astype(_F32)).astype(_BF16)
        pre_f32[par, 1] = f
        background(PRELUDE_WEIGHT)

    def prelude_matmuls(yrow0, par, b_blk):
        tri = tri_ref[...]
        b_blk[...] = _dot(tri, pre_bf[par, 1]) + _dot(tri, pre_bf[par, 2])
        mixed = []
        for hd in range(HEADS):
            mixed.append(_dot(wsm_sc[hd], pre_bf[par, 0, :, pl.ds(hd * HEAD_DIM, HEAD_DIM)]) + bs_ref[hd])
        y_a = pre_f32[par, 0] * jnp.concatenate(mixed, axis=-1)
        y_sc[pl.ds(yrow0, BLOCK), pl.ds(0, D_GMLP)] = y_a.astype(_BF16)

    def level_slabs(n):
        h = n // 2
        if h % SUBLANES or n == 1:
            return list(range(0, BLOCK, SUBLANES))
        return [r for r0 in range(0, BLOCK, n) for r in range(r0 + h, r0 + n, SUBLANES)]

    def stage_head(proj_sc, prow0, par, b_blk, hd, slot):
        rows = pl.ds(prow0, BLOCK)
        c0 = hd * HEAD_DIM
        q_h = _silu(proj_sc[rows, pl.ds(o0 + c0, HEAD_DIM)])
        f_h = pre_f32[par, 1, :, pl.ds(c0, HEAD_DIM)]
        k_h = 1.0 - f_h
        for idx, n in enumerate(levels):
            if n == 2:
                z = jnp.where(upper_rows[idx], q_h * f_h, k_h)
            else:
                z = _half_select(n, q_h, k_h, upper_rows[idx]) * _decay_factor(n, b_blk, row, c0)
            slabs = level_slabs(n)
            lhs = jnp.concatenate([z[r:r + SUBLANES] for r in slabs], axis=0)
            zl_sc[slot, idx, pl.ds(0, len(slabs) * SUBLANES), :] = lhs.astype(_BF16)
            zt_sc[slot, idx] = z.T.astype(_BF16)
        if guarded:
            d = _decay_exponent(GUARD_BLOCK, b_blk, row, c0)
            zl_sc[slot, len(levels)] = (q_h * jnp.exp2(d)).astype(_BF16)
            zt_sc[slot, len(levels)] = (k_h * jnp.exp2(-d)).T.astype(_BF16)
        return q_h, k_h

    def head_levels(slot, background):
        attn = [jnp.zeros((SUBLANES, BLOCK), _F32)] * (BLOCK // SUBLANES)
        for idx in range(n_products):
            slabs = level_slabs(levels[idx]) if idx < len(levels) else level_slabs(1)
            res = _dot(zl_sc[slot, idx, pl.ds(0, len(slabs) * SUBLANES), :], zt_sc[slot, idx])
            for j, r in enumerate(slabs):
                valid = lvl[r:r + SUBLANES] == idx
                attn[r // SUBLANES] = jnp.where(valid, res[j * SUBLANES:(j + 1) * SUBLANES],
                                                attn[r // SUBLANES])
            if idx % 2 == 1:
                background()
        for _ in range(n_products // 2, len(LEVELS) // 2):
            background()
        return jnp.concatenate(attn, axis=0)

    def head_tail(proj_sc, prow0, yrow0, b_blk, hd, attn, q_h, k_h):
        rows = pl.ds(prow0, BLOCK)
        c0 = hd * HEAD_DIM
        v_h = proj_sc[rows, pl.ds(o0 + 2 * D_HGRN + c0, HEAD_DIM)]
        g_h = proj_sc[rows, pl.ds(o0 + 3 * D_HGRN + c0, HEAD_DIM)]
        b_h = b_blk[:, pl.ds(c0, HEAD_DIM)]

        s_t = state_sc[hd]
        q_hat = (q_h * jnp.exp2(b_h)).astype(_BF16)
        v_t = v_h.T.astype(_BF16)
        lhs = jnp.concatenate([attn.astype(_BF16), q_hat], axis=1)
        rhs_t = jnp.concatenate([v_t, s_t.astype(_BF16)], axis=1)
        o_h = _dot_nt(lhs, rhs_t)
        if not guarded:
            o_h = o_h + jnp.sum(q_h * k_h, axis=-1, keepdims=True) * v_h

        b_last = b_blk[pl.ds(BLOCK - 1, 1), pl.ds(c0, HEAD_DIM)]
        k_hat = (k_h * jnp.exp2(b_last - b_h)).astype(_BF16)
        state_sc[hd] = s_t * jnp.exp2(b_last) + _dot(v_t, k_hat)

        oms = jnp.mean(o_h * o_h, axis=-1, keepdims=True)
        y_b = o_h * lax.rsqrt(oms + EPS) * gnw_ref[...] * _silu(g_h)
        y_sc[pl.ds(yrow0, BLOCK), pl.ds(D_GMLP + c0, HEAD_DIM)] = y_b.astype(_BF16)

    blocks_per_tile = MIX_TILE // BLOCK
    g1 = ada_b[2:3]

    def projection_tasks(h_sc, dst):
        def chunk(c):
            cols = pl.ds(c * MXU_CHUNK, MXU_CHUNK)
            dst[:, cols] = _dot(h_sc[...], w_in_ref[:, cols])
        return [functools.partial(chunk, c) for c in range(D_IN // MXU_CHUNK)]

    def finish_tasks(t):
        trow = pl.ds(t * MIX_TILE, MIX_TILE)

        def chunk(c):
            cols = pl.ds(c * MXU_CHUNK, MXU_CHUNK)
            mix = _dot(y_sc[trow, :], w_out_ref[:, cols])
            o_ref[trow, cols] = x_ref[trow, cols] + g1[:, c * MXU_CHUNK:(c + 1) * MXU_CHUNK] * mix
        return [functools.partial(chunk, c) for c in range(D_MODEL // MXU_CHUNK)]

    def mix_tile(proj_sc, t, tasks):
        background = _Interleaver(tasks, blocks_per_tile * WEIGHT_PER_BLOCK)
        heads = [(i, hd) for i in range(blocks_per_tile) for hd in range(HEADS)]

        def blk(i):
            return i * BLOCK, t * MIX_TILE + i * BLOCK, i % 2, b_sc.at[t * blocks_per_tile + i]

        prow0, yrow0, par, b_blk = blk(0)
        prelude_elementwise(proj_sc, prow0, par, background)
        prelude_matmuls(yrow0, par, b_blk)
        staged = stage_head(proj_sc, prow0, par, b_blk, 0, 0)
        pending = None
        for j, (i, hd) in enumerate(heads):
            prow0, yrow0, par, b_blk = blk(i)
            if hd == 1 and i + 1 < blocks_per_tile:
                prelude_elementwise(proj_sc, blk(i + 1)[0], blk(i + 1)[2], background)
            if hd == 2 and i + 1 < blocks_per_tile:
                prelude_matmuls(blk(i + 1)[1], blk(i + 1)[2], blk(i + 1)[3])
            staged_next = None
            if j + 1 < len(heads):
                ni, nhd = heads[j + 1]
                staged_next = stage_head(proj_sc, blk(ni)[0], blk(ni)[2], blk(ni)[3], nhd, (j + 1) % 2)
            attn = head_levels(j % 2, background)
            if pending is not None:
                pending()
            pending = functools.partial(head_tail, proj_sc, prow0, yrow0, b_blk, hd, attn, *staged)
            background()
            staged = staged_next
        pending()
        background.flush()

    mix_tile(proj_a, 0, projection_tasks(h_b, proj_b))
    ada_next = ada_ref[batch_next]
    h_a[...] = _modulated_input(xn_ref[pl.ds(0, MIX_TILE), :], ada_next, n1w_ref)
    mix_tile(proj_b, 1, finish_tasks(0) + projection_tasks(h_a, proj_a))
    h_b[...] = _modulated_input(xn_ref[pl.ds(MIX_TILE, MIX_TILE), :], ada_next, n1w_ref)
    for task in finish_tasks(1):
        task()


def _mixer_call(x, ada, n1w, w_in, w_s, b_s, lnw, lnb, lbraw, gnw, w_out, tri, w1, w2, guarded):
    bsz, seq, _ = x.shape
    step_rows = 2 * MIX_TILE
    steps_per_seq = seq // step_rows
    n_steps = bsz * steps_per_seq
    x2 = x.reshape(bsz * seq, D_MODEL)
    const2 = lambda g: (0, 0)
    const3 = lambda g: (0, 0, 0)
    resident = dict(pipeline_mode=pl.Buffered(1))
    w1_rows = _cast_rows(w1.shape[0], n_steps)
    w2_rows = _cast_rows(w2.shape[0], n_steps)
    w1_map = lambda g: (jnp.minimum(g, w1.shape[0] // w1_rows - 1), 0)
    w2_map = lambda g: (jnp.minimum(g, w2.shape[0] // w2_rows - 1), 0)
    if guarded:
        lvl = jnp.asarray(_level_table(GUARD_LEVELS, GUARD_BLOCK))
    else:
        lvl = jnp.asarray(_level_table(LEVELS))
    kernel = functools.partial(_mixer_kernel, steps_per_seq=steps_per_seq, n_steps=n_steps,
                               guarded=guarded)
    out, w1_bf, w2_bf = pl.pallas_call(
        kernel,
        grid=(n_steps,),
        in_specs=[
            pl.BlockSpec((step_rows, D_MODEL), lambda g: (g, 0)),
            pl.BlockSpec((step_rows, D_MODEL), lambda g: (jnp.minimum(g + 1, n_steps - 1), 0)),
            pl.BlockSpec((bsz, N_ADA, D_MODEL), const3),
            pl.BlockSpec((1, D_MODEL), const2),
            pl.BlockSpec((D_MODEL, D_IN), const2, **resident),
            pl.BlockSpec((HEADS, BLOCK, BLOCK), const3),
            pl.BlockSpec((HEADS, BLOCK, 1), const3),
            pl.BlockSpec((1, D_GMLP), const2),
            pl.BlockSpec((1, D_GMLP), const2),
            pl.BlockSpec((2, D_HGRN), const2),
            pl.BlockSpec((1, HEAD_DIM), const2),
            pl.BlockSpec((D_MODEL, D_MODEL), const2, **resident),
            pl.BlockSpec((BLOCK, BLOCK), const2),
            pl.BlockSpec((BLOCK, BLOCK), const2),
            pl.BlockSpec((w1_rows, w1.shape[1]), w1_map),
            pl.BlockSpec((w2_rows, w2.shape[1]), w2_map),
        ],
        out_specs=[
            pl.BlockSpec((step_rows, D_MODEL), lambda g: (g, 0)),
            pl.BlockSpec((w1_rows, w1.shape[1]), w1_map),
            pl.BlockSpec((w2_rows, w2.shape[1]), w2_map),
        ],
        out_shape=[
            jax.ShapeDtypeStruct(x2.shape, _F32),
            jax.ShapeDtypeStruct(w1.shape, _BF16),
            jax.ShapeDtypeStruct(w2.shape, _BF16),
        ],
        scratch_shapes=[
            pltpu.VMEM((HEADS, HEAD_DIM, HEAD_DIM), _F32),
            pltpu.VMEM((MIX_TILE, D_IN), _F32),
            pltpu.VMEM((MIX_TILE, D_IN), _F32),
            pltpu.VMEM((MIX_TILE, D_MODEL), _BF16),
            pltpu.VMEM((MIX_TILE, D_MODEL), _BF16),
            pltpu.VMEM((step_rows, D_MODEL), _BF16),
            pltpu.VMEM((step_rows // BLOCK, BLOCK, D_HGRN), _F32),
            pltpu.VMEM((HEADS, BLOCK, BLOCK), _BF16),
            pltpu.VMEM((D_MODEL, D_IN), _BF16),
            pltpu.VMEM((D_MODEL, D_MODEL), _BF16),
            pltpu.VMEM((2, 3, BLOCK, D_HGRN), _BF16),
            pltpu.VMEM((2, 2, BLOCK, D_HGRN), _F32),
            pltpu.VMEM((2, len(LEVELS), BLOCK, HEAD_DIM), _BF16),
            pltpu.VMEM((2, len(LEVELS), HEAD_DIM, BLOCK), _BF16),
        ],
        compiler_params=pltpu.CompilerParams(
            dimension_semantics=("arbitrary",),
            vmem_limit_bytes=VMEM_LIMIT),
        name="mixer",
    )(x2, x2, ada, n1w, w_in, w_s, b_s, lnw, lnb, lbraw, gnw, w_out, tri, lvl, w1, w2)
    return out.reshape(x.shape), w1_bf, w2_bf


def _ffn_kernel(x_ref, ada_ref, n2w_ref, w1_ref, w2_ref, fnw_ref, o_ref):
    ada = ada_ref[0]
    sh2, sc2, g2 = ada[3:4], ada[4:5], ada[5:6]

    def hidden(p):
        x = x_ref[0, pl.ds(p * FFN_PART, FFN_PART), :]
        ms = jnp.mean(x * x, axis=-1, keepdims=True)
        hmod = x * lax.rsqrt(ms + EPS) * n2w_ref[...]
        hmod = (hmod * (1.0 + sc2) + sh2).astype(_BF16)
        gu = _dot(hmod, w1_ref[...])
        return (_silu(gu[:, :D_FF]) * gu[:, D_FF:]).astype(_BF16)

    def finish(p, act):
        rows = pl.ds(p * FFN_PART, FFN_PART)
        x2 = x_ref[0, rows, :] + g2 * _dot(act, w2_ref[...])
        ms2 = jnp.mean(x2 * x2, axis=-1, keepdims=True)
        o_ref[0, rows, :] = x2 * lax.rsqrt(ms2 + EPS) * fnw_ref[...]

    act = hidden(0)
    for p in range(1, FFN_PARTS):
        act_next = hidden(p)
        finish(p - 1, act)
        act = act_next
    finish(FFN_PARTS - 1, act)


def _ffn_call(x, ada, n2w, w1, w2, fnw):
    bsz, seq, _ = x.shape
    tile = FFN_PART * FFN_PARTS
    const2 = lambda b, s: (0, 0)
    resident = dict(pipeline_mode=pl.Buffered(1))
    return pl.pallas_call(
        _ffn_kernel,
        grid=(bsz, seq // tile),
        in_specs=[
            pl.BlockSpec((1, tile, D_MODEL), lambda b, s: (b, s, 0)),
            pl.BlockSpec((1, N_ADA, D_MODEL), lambda b, s: (b, 0, 0)),
            pl.BlockSpec((1, D_MODEL), const2),
            pl.BlockSpec((D_MODEL, 2 * D_FF), const2, **resident),
            pl.BlockSpec((D_FF, D_MODEL), const2, **resident),
            pl.BlockSpec((1, D_MODEL), const2),
        ],
        out_specs=pl.BlockSpec((1, tile, D_MODEL), lambda b, s: (b, s, 0)),
        out_shape=jax.ShapeDtypeStruct(x.shape, _F32),
        compiler_params=pltpu.CompilerParams(
            dimension_semantics=("arbitrary", "arbitrary"),
            vmem_limit_bytes=VMEM_LIMIT),
        name="ffn",
    )(x, ada, n2w, w1, w2, fnw)


def kernel(x, c, w_ada, b_ada, norm1_w, w_in, w_s, b_s, v_ln_w, v_ln_b, lower_bounds,
           gn_w, w_out, norm2_w, w_ffn_in, w_ffn_out, final_norm_w):
    bsz = x.shape[0]
    depth = w_in.shape[0]
    assert depth == 1 and lower_bounds.shape[0] == 2

    tri = jnp.asarray(np.tril(np.ones((BLOCK, BLOCK), np.float32)), _BF16)

    for l in range(depth):
        ada = _ada_call(c, w_ada[l], b_ada[l][None, :])
        ada = ada.reshape(bsz, N_ADA, D_MODEL)
        mixer = functools.partial(
            _mixer_call, x, ada, norm1_w[l][None, :], w_in[l], w_s[l],
            b_s[l][:, :, None], v_ln_w[l][None, :], v_ln_b[l][None, :],
            lower_bounds, gn_w[l][None, :], w_out[l], tri, w_ffn_in[l], w_ffn_out[l])
        lb = jnp.cumsum(jax.nn.softmax(lower_bounds.astype(_F32), axis=0), axis=0)[l]
        log2_bound = (GUARD_BLOCK // 2) * jnp.max(-jnp.log2(lb))
        x, w1_bf, w2_bf = lax.cond(log2_bound <= GUARD_LOG2_RANGE,
                                   lambda: mixer(guarded=True), lambda: mixer(guarded=False))
        x = _ffn_call(x, ada, norm2_w[l][None, :], w1_bf, w2_bf, final_norm_w[None, :])
    return x
```

```python
import functools

import numpy as np
import jax
import jax.numpy as jnp
from jax import lax
from jax.experimental import pallas as pl
from jax.experimental.pallas import tpu as pltpu

D_MODEL = 1024
D_GMLP = 512
D_HGRN = 512
HEADS = 4
HEAD_DIM = 128
BLOCK = 128
GATE_CHUNK = 64
D_IN = 2 * D_GMLP + 4 * D_HGRN
D_FF = 2816
N_ADA = 6
EPS = 1e-6

LEVELS = (128, 64, 32, 16, 8, 4, 2)
GUARD_LEVELS = (128, 64)
GUARD_BLOCK = 32
GUARD_LOG2_RANGE = 100.0

MIX_TILE = 256
MXU_CHUNK = 256
FFN_PART = 256
FFN_PARTS = 4
VMEM_LIMIT = 56 * 1024 * 1024

_F32 = jnp.float32
_BF16 = jnp.bfloat16


def _dot(a, b):
    return jnp.dot(a, b, preferred_element_type=_F32)


def _dot_nt(a, b):
    return lax.dot_general(a, b, (((1,), (1,)), ((), ())), preferred_element_type=_F32)


def _silu(x):
    return x * jax.nn.sigmoid(x)


def _gelu(x):
    half = 0.5 * x
    return half + half * lax.erf(x * np.float32(1.0 / np.sqrt(2.0)))


def _level_table(levels, guard_block=None):
    t = np.arange(BLOCK)[:, None]
    s = np.arange(BLOCK)[None, :]
    table = np.full((BLOCK, BLOCK), -1, np.int32)
    if guard_block:
        table[((t // guard_block) == (s // guard_block)) & (s <= t)] = len(levels)
    for idx, n in enumerate(levels):
        h = n // 2
        same = (t // n) == (s // n)
        table[same & ((t % n) >= h) & ((s % n) < h)] = idx
    return table


def _ada_kernel(c_ref, w_ref, b_ref, o_ref):
    c = c_ref[...]
    c_act = _silu(c).astype(_BF16)
    o_ref[...] = _dot(c_act, w_ref[...].astype(_BF16)) + b_ref[...]


def _ada_call(c_pad, w_ada, b_ada):
    n_out = w_ada.shape[1]
    tn = 1536
    return pl.pallas_call(
        _ada_kernel,
        grid=(n_out // tn,),
        in_specs=[
            pl.BlockSpec((c_pad.shape[0], D_MODEL), lambda j: (0, 0)),
            pl.BlockSpec((D_MODEL, tn), lambda j: (0, j)),
            pl.BlockSpec((1, tn), lambda j: (0, j)),
        ],
        out_specs=pl.BlockSpec((c_pad.shape[0], tn), lambda j: (0, j)),
        out_shape=jax.ShapeDtypeStruct((c_pad.shape[0], n_out), _F32),
        compiler_params=pltpu.CompilerParams(dimension_semantics=("arbitrary",)),
        name="ada",
    )(c_pad, w_ada, b_ada)


def _neg_abs(x):
    bits = pltpu.bitcast(x, jnp.uint32) | jnp.uint32(0x80000000)
    return pltpu.bitcast(bits, _F32)


SUBLANES = 8


def _half_select(level_n, q, k, upper):
    h = level_n // 2
    if h % SUBLANES != 0:
        return jnp.where(upper, q, k)
    pieces = []
    for r0 in range(0, BLOCK, level_n):
        pieces += [k[r0:r0 + h], q[r0 + h:r0 + level_n]]
    return jnp.concatenate(pieces, axis=0)


def _decay_exponent(level_n, b_sc, row, col0):
    h = level_n // 2

    def ref_rows(r, n_rows):
        return jnp.broadcast_to(b_sc[pl.ds(r, 1), pl.ds(col0, HEAD_DIM)], (n_rows, HEAD_DIM))

    if level_n < SUBLANES:
        lower_block = (row[:SUBLANES] % SUBLANES) < level_n
        pieces = [jnp.where(lower_block, ref_rows(r0 + h - 1, SUBLANES),
                            ref_rows(r0 + level_n + h - 1, SUBLANES))
                  for r0 in range(0, BLOCK, SUBLANES)]
    else:
        pieces = [ref_rows(r0 + h - 1, level_n) for r0 in range(0, BLOCK, level_n)]
    b_ref = pieces[0] if len(pieces) == 1 else jnp.concatenate(pieces, axis=0)
    b_h = b_sc[:, pl.ds(col0, HEAD_DIM)]
    return b_h - b_ref


def _decay_factor(level_n, b_sc, row, col0):
    return jnp.exp2(_neg_abs(_decay_exponent(level_n, b_sc, row, col0)))


def _modulated_input(x, ada_b, n1w_ref):
    sh1, sc1 = ada_b[0:1], ada_b[1:2]
    ms = jnp.mean(x * x, axis=-1, keepdims=True)
    hmod = x * lax.rsqrt(ms + EPS) * n1w_ref[...]
    return (hmod * (1.0 + sc1) + sh1).astype(_BF16)


BF16_ROWS = 16


def _cast_rows(n_rows, n_steps):
    rows = -(-n_rows // n_steps)
    rows = -(-rows // BF16_ROWS) * BF16_ROWS
    while n_rows % rows:
        rows += BF16_ROWS
    return rows


class _Interleaver:
    def __init__(self, tasks, total_weight):
        self._tasks = list(tasks)
        self._total = total_weight
        self._weight = 0
        self._done = 0

    def __call__(self, weight=1):
        self._weight += weight
        due = min(len(self._tasks), self._weight * len(self._tasks) // self._total)
        while self._done < due:
            self._tasks[self._done]()
            self._done += 1

    def flush(self):
        assert self._weight == self._total, (self._weight, self._total)
        while self._done < len(self._tasks):
            self._tasks[self._done]()
            self._done += 1


PRELUDE_WEIGHT = 4
HEAD_LEVELS_WEIGHT = 3
WEIGHT_PER_BLOCK = 2 * PRELUDE_WEIGHT + HEADS * (HEAD_LEVELS_WEIGHT + 1)


def _mixer_kernel(x_ref, xn_ref, ada_ref, n1w_ref, w_in_f32, ws_ref, bs_ref, lnw_ref, lnb_ref,
                  lbraw_ref, gnw_ref, w_out_f32, tri_ref, lvl_ref, w1_f32, w2_f32,
                  o_ref, w1_bf, w2_bf,
                  state_sc, proj_a, proj_b, h_a, h_b, y_sc, b_sc, wsm_sc, w_in_ref, w_out_ref,
                  pre_bf, pre_f32, zl_sc, zt_sc,
                  *, steps_per_seq, n_steps, guarded):
    g = pl.program_id(0)
    batch = g // steps_per_seq
    batch_next = jnp.minimum(g + 1, n_steps - 1) // steps_per_seq
    ada_b = ada_ref[batch]

    w1_bf[...] = w1_f32[...].astype(_BF16)
    w2_bf[...] = w2_f32[...].astype(_BF16)

    @pl.when(g == 0)
    def _():
        w_in_ref[...] = w_in_f32[...].astype(_BF16)
        w_out_ref[...] = w_out_f32[...].astype(_BF16)
        h0 = _modulated_input(x_ref[pl.ds(0, MIX_TILE), :], ada_b, n1w_ref)
        proj_a[...] = _dot(h0, w_in_ref[...])
        h_b[...] = _modulated_input(x_ref[pl.ds(MIX_TILE, MIX_TILE), :], ada_b, n1w_ref)
        col = lax.broadcasted_iota(jnp.int32, (BLOCK, BLOCK), 1)
        rowb = lax.broadcasted_iota(jnp.int32, (BLOCK, BLOCK), 0)
        gate_mask = (rowb // GATE_CHUNK) >= (col // GATE_CHUNK)
        for hd in range(HEADS):
            wsm_sc[hd] = jnp.where(gate_mask, ws_ref[hd], 0.0).astype(_BF16)

    @pl.when(g % steps_per_seq == 0)
    def _():
        state_sc[...] = jnp.zeros_like(state_sc)

    lbraw = lbraw_ref[...]
    lbe = jnp.exp(lbraw - jnp.max(lbraw, axis=0, keepdims=True))
    lb = (lbe / jnp.sum(lbe, axis=0, keepdims=True))[0:1]

    levels = GUARD_LEVELS if guarded else LEVELS
    n_products = len(levels) + (1 if guarded else 0)
    row = lax.broadcasted_iota(jnp.int32, (BLOCK, HEAD_DIM), 0)
    lvl = lvl_ref[...]
    upper_rows = [(row % n) >= (n // 2) for n in levels]

    o0 = 2 * D_GMLP

    def prelude_elementwise(proj_sc, prow0, par, background):
        rows = pl.ds(prow0, BLOCK)
        u = proj_sc[rows, pl.ds(0, D_GMLP)]
        v = proj_sc[rows, pl.ds(D_GMLP, D_GMLP)]
        gv = _gelu(v)
        mu = jnp.mean(gv, axis=-1, keepdims=True)
        cen = gv - mu
        var = jnp.mean(cen * cen, axis=-1, keepdims=True)
        vn = cen * lax.rsqrt(var + EPS) * lnw_ref[...] + lnb_ref[...]
        pre_bf[par, 0] = vn.astype(_BF16)
        pre_f32[par, 0] = _gelu(u)
        background(PRELUDE_WEIGHT)
        fl = proj_sc[rows, pl.ds(o0 + D_HGRN, D_HGRN)]
        f = lb + (1.0 - lb) * jax.nn.sigmoid(fl)
        logf = jnp.log2(f)
        p_hi = logf.astype(_BF16)
        pre_bf[par, 1] = p_hi
        pre_bf[par, 2] = (logf - p_hi.astype(_F32)).astype(_BF16)
        pre_f32[par, 1] = f
        background(PRELUDE_WEIGHT)

    def prelude_matmuls(yrow0, par, b_blk):
        tri = tri_ref[...]
        b_blk[...] = _dot(tri, pre_bf[par, 1]) + _dot(tri, pre_bf[par, 2])
        mixed = []
        for hd in range(HEADS):
            mixed.append(_dot(wsm_sc[hd], pre_bf[par, 0, :, pl.ds(hd * HEAD_DIM, HEAD_DIM)]) + bs_ref[hd])
        y_a = pre_f32[par, 0] * jnp.concatenate(mixed, axis=-1)
        y_sc[pl.ds(yrow0, BLOCK), pl.ds(0, D_GMLP)] = y_a.astype(_BF16)

    def level_slabs(n):
        h = n // 2
        if h % SUBLANES or n == 1:
            return list(range(0, BLOCK, SUBLANES))
        return [r for r0 in range(0, BLOCK, n) for r in range(r0 + h, r0 + n, SUBLANES)]

    def stage_head(proj_sc, prow0, par, b_blk, hd, slot):
        rows = pl.ds(prow0, BLOCK)
        c0 = hd * HEAD_DIM
        q_h = _silu(proj_sc[rows, pl.ds(o0 + c0, HEAD_DIM)])
        f_h = pre_f32[par, 1, :, pl.ds(c0, HEAD_DIM)]
        k_h = 1.0 - f_h
        for idx, n in enumerate(levels):
            if n == 2:
                z = jnp.where(upper_rows[idx], q_h * f_h, k_h)
            else:
                z = _half_select(n, q_h, k_h, upper_rows[idx]) * _decay_factor(n, b_blk, row, c0)
            slabs = level_slabs(n)
            lhs = jnp.concatenate([z[r:r + SUBLANES] for r in slabs], axis=0)
            zl_sc[slot, idx, pl.ds(0, len(slabs) * SUBLANES), :] = lhs.astype(_BF16)
            zt_sc[slot, idx] = z.T.astype(_BF16)
        if guarded:
            d = _decay_exponent(GUARD_BLOCK, b_blk, row, c0)
            zl_sc[slot, len(levels)] = (q_h * jnp.exp2(d)).astype(_BF16)
            zt_sc[slot, len(levels)] = (k_h * jnp.exp2(-d)).T.astype(_BF16)
        return q_h, k_h

    def head_levels(slot, background):
        attn = [jnp.zeros((SUBLANES, BLOCK), _F32)] * (BLOCK // SUBLANES)
        for idx in range(n_products):
            slabs = level_slabs(levels[idx]) if idx < len(levels) else level_slabs(1)
            res = _dot(zl_sc[slot, idx, pl.ds(0, len(slabs) * SUBLANES), :], zt_sc[slot, idx])
            for j, r in enumerate(slabs):
                valid = lvl[r:r + SUBLANES] == idx
                attn[r // SUBLANES] = jnp.where(valid, res[j * SUBLANES:(j + 1) * SUBLANES],
                                                attn[r // SUBLANES])
        background(HEAD_LEVELS_WEIGHT)
        return jnp.concatenate(attn, axis=0)

    def head_tail(proj_sc, prow0, yrow0, b_blk, hd, attn, q_h, k_h):
        rows = pl.ds(prow0, BLOCK)
        c0 = hd * HEAD_DIM
        v_h = proj_sc[rows, pl.ds(o0 + 2 * D_HGRN + c0, HEAD_DIM)]
        g_h = proj_sc[rows, pl.ds(o0 + 3 * D_HGRN + c0, HEAD_DIM)]
        b_h = b_blk[:, pl.ds(c0, HEAD_DIM)]

        s_t = state_sc[hd]
        q_hat = (q_h * jnp.exp2(b_h)).astype(_BF16)
        v_t = v_h.T.astype(_BF16)
        lhs = jnp.concatenate([attn.astype(_BF16), q_hat], axis=1)
        rhs_t = jnp.concatenate([v_t, s_t.astype(_BF16)], axis=1)
        o_h = _dot_nt(lhs, rhs_t)
        if not guarded:
            o_h = o_h + jnp.sum(q_h * k_h, axis=-1, keepdims=True) * v_h

        b_last = b_blk[pl.ds(BLOCK - 1, 1), pl.ds(c0, HEAD_DIM)]
        k_hat = (k_h * jnp.exp2(b_last - b_h)).astype(_BF16)
        state_sc[hd] = s_t * jnp.exp2(b_last) + _dot(v_t, k_hat)

        oms = jnp.mean(o_h * o_h, axis=-1, keepdims=True)
        y_b = o_h * lax.rsqrt(oms + EPS) * gnw_ref[...] * _silu(g_h)
        y_sc[pl.ds(yrow0, BLOCK), pl.ds(D_GMLP + c0, HEAD_DIM)] = y_b.astype(_BF16)

    blocks_per_tile = MIX_TILE // BLOCK
    g1 = ada_b[2:3]

    def projection_tasks(h_sc, dst):
        def chunk(c):
            cols = pl.ds(c * MXU_CHUNK, MXU_CHUNK)
            dst[:, cols] = _dot(h_sc[...], w_in_ref[:, cols])
        return [functools.partial(chunk, c) for c in range(D_IN // MXU_CHUNK)]

    def finish_tasks(t):
        trow = pl.ds(t * MIX_TILE, MIX_TILE)

        def chunk(c):
            cols = pl.ds(c * MXU_CHUNK, MXU_CHUNK)
            mix = _dot(y_sc[trow, :], w_out_ref[:, cols])
            o_ref[trow, cols] = x_ref[trow, cols] + g1[:, c * MXU_CHUNK:(c + 1) * MXU_CHUNK] * mix
        return [functools.partial(chunk, c) for c in range(D_MODEL // MXU_CHUNK)]

    def mix_tile(proj_sc, t, tasks):
        background = _Interleaver(tasks, blocks_per_tile * WEIGHT_PER_BLOCK)
        heads = [(i, hd) for i in range(blocks_per_tile) for hd in range(HEADS)]

        def blk(i):
            return i * BLOCK, t * MIX_TILE + i * BLOCK, i % 2, b_sc.at[t * blocks_per_tile + i]

        prow0, yrow0, par, b_blk = blk(0)
        prelude_elementwise(proj_sc, prow0, par, background)
        prelude_matmuls(yrow0, par, b_blk)
        staged = stage_head(proj_sc, prow0, par, b_blk, 0, 0)
        pending = None
        for j, (i, hd) in enumerate(heads):
            prow0, yrow0, par, b_blk = blk(i)
            if hd == 1 and i + 1 < blocks_per_tile:
                prelude_elementwise(proj_sc, blk(i + 1)[0], blk(i + 1)[2], background)
            if hd == 2 and i + 1 < blocks_per_tile:
                prelude_matmuls(blk(i + 1)[1], blk(i + 1)[2], blk(i + 1)[3])
            staged_next = None
            if j + 1 < len(heads):
                ni, nhd = heads[j + 1]
                staged_next = stage_head(proj_sc, blk(ni)[0], blk(ni)[2], blk(ni)[3], nhd, (j + 1) % 2)
            attn = head_levels(j % 2, background)
            if pending is not None:
                pending()
            pending = functools.partial(head_tail, proj_sc, prow0, yrow0, b_blk, hd, attn, *staged)
            background()
            staged = staged_next
        pending()
        background.flush()

    mix_tile(proj_a, 0, projection_tasks(h_b, proj_b))
    ada_next = ada_ref[batch_next]
    h_a[...] = _modulated_input(xn_ref[pl.ds(0, MIX_TILE), :], ada_next, n1w_ref)
    mix_tile(proj_b, 1, finish_tasks(0) + projection_tasks(h_a, proj_a))
    h_b[...] = _modulated_input(xn_ref[pl.ds(MIX_TILE, MIX_TILE), :], ada_next, n1w_ref)
    for task in finish_tasks(1):
        task()


def _mixer_call(x, ada, n1w, w_in, w_s, b_s, lnw, lnb, lbraw, gnw, w_out, tri, w1, w2, guarded):
    bsz, seq, _ = x.shape
    step_rows = 2 * MIX_TILE
    steps_per_seq = seq // step_rows
    n_steps = bsz * steps_per_seq
    x2 = x.reshape(bsz * seq, D_MODEL)
    const2 = lambda g: (0, 0)
    const3 = lambda g: (0, 0, 0)
    resident = dict(pipeline_mode=pl.Buffered(1))
    w1_rows = _cast_rows(w1.shape[0], n_steps)
    w2_rows = _cast_rows(w2.shape[0], n_steps)
    w1_map = lambda g: (jnp.minimum(g, w1.shape[0] // w1_rows - 1), 0)
    w2_map = lambda g: (jnp.minimum(g, w2.shape[0] // w2_rows - 1), 0)
    if guarded:
        lvl = jnp.asarray(_level_table(GUARD_LEVELS, GUARD_BLOCK))
    else:
        lvl = jnp.asarray(_level_table(LEVELS))
    kernel = functools.partial(_mixer_kernel, steps_per_seq=steps_per_seq, n_steps=n_steps,
                               guarded=guarded)
    out, w1_bf, w2_bf = pl.pallas_call(
        kernel,
        grid=(n_steps,),
        in_specs=[
            pl.BlockSpec((step_rows, D_MODEL), lambda g: (g, 0)),
            pl.BlockSpec((step_rows, D_MODEL), lambda g: (jnp.minimum(g + 1, n_steps - 1), 0)),
            pl.BlockSpec((bsz, N_ADA, D_MODEL), const3),
            pl.BlockSpec((1, D_MODEL), const2),
            pl.BlockSpec((D_MODEL, D_IN), const2, **resident),
            pl.BlockSpec((HEADS, BLOCK, BLOCK), const3),
            pl.BlockSpec((HEADS, BLOCK, 1), const3),
            pl.BlockSpec((1, D_GMLP), const2),
            pl.BlockSpec((1, D_GMLP), const2),
            pl.BlockSpec((2, D_HGRN), const2),
            pl.BlockSpec((1, HEAD_DIM), const2),
            pl.BlockSpec((D_MODEL, D_MODEL), const2, **resident),
            pl.BlockSpec((BLOCK, BLOCK), const2),
            pl.BlockSpec((BLOCK, BLOCK), const2),
            pl.BlockSpec((w1_rows, w1.shape[1]), w1_map),
            pl.BlockSpec((w2_rows, w2.shape[1]), w2_map),
        ],
        out_specs=[
            pl.BlockSpec((step_rows, D_MODEL), lambda g: (g, 0)),
            pl.BlockSpec((w1_rows, w1.shape[1]), w1_map),
            pl.BlockSpec((w2_rows, w2.shape[1]), w2_map),
        ],
        out_shape=[
            jax.ShapeDtypeStruct(x2.shape, _F32),
            jax.ShapeDtypeStruct(w1.shape, _BF16),
            jax.ShapeDtypeStruct(w2.shape, _BF16),
        ],
        scratch_shapes=[
            pltpu.VMEM((HEADS, HEAD_DIM, HEAD_DIM), _F32),
            pltpu.VMEM((MIX_TILE, D_IN), _F32),
            pltpu.VMEM((MIX_TILE, D_IN), _F32),
            pltpu.VMEM((MIX_TILE, D_MODEL), _BF16),
            pltpu.VMEM((MIX_TILE, D_MODEL), _BF16),
            pltpu.VMEM((step_rows, D_MODEL), _BF16),
            pltpu.VMEM((step_rows // BLOCK, BLOCK, D_HGRN), _F32),
            pltpu.VMEM((HEADS, BLOCK, BLOCK), _BF16),
            pltpu.VMEM((D_MODEL, D_IN), _BF16),
            pltpu.VMEM((D_MODEL, D_MODEL), _BF16),
            pltpu.VMEM((2, 3, BLOCK, D_HGRN), _BF16),
            pltpu.VMEM((2, 2, BLOCK, D_HGRN), _F32),
            pltpu.VMEM((2, len(LEVELS), BLOCK, HEAD_DIM), _BF16),
            pltpu.VMEM((2, len(LEVELS), HEAD_DIM, BLOCK), _BF16),
        ],
        compiler_params=pltpu.CompilerParams(
            dimension_semantics=("arbitrary",),
            vmem_limit_bytes=VMEM_LIMIT),
        name="mixer",
    )(x2, x2, ada, n1w, w_in, w_s, b_s, lnw, lnb, lbraw, gnw, w_out, tri, lvl, w1, w2)
    return out.reshape(x.shape), w1_bf, w2_bf


def _ffn_kernel(x_ref, ada_ref, n2w_ref, w1_ref, w2_ref, fnw_ref, o_ref):
    ada = ada_ref[0]
    sh2, sc2, g2 = ada[3:4], ada[4:5], ada[5:6]

    def hidden(p):
        x = x_ref[0, pl.ds(p * FFN_PART, FFN_PART), :]
        ms = jnp.mean(x * x, axis=-1, keepdims=True)
        hmod = x * lax.rsqrt(ms + EPS) * n2w_ref[...]
        hmod = (hmod * (1.0 + sc2) + sh2).astype(_BF16)
        gu = _dot(hmod, w1_ref[...])
        return (_silu(gu[:, :D_FF]) * gu[:, D_FF:]).astype(_BF16)

    def finish(p, act):
        rows = pl.ds(p * FFN_PART, FFN_PART)
        x2 = x_ref[0, rows, :] + g2 * _dot(act, w2_ref[...])
        ms2 = jnp.mean(x2 * x2, axis=-1, keepdims=True)
        o_ref[0, rows, :] = x2 * lax.rsqrt(ms2 + EPS) * fnw_ref[...]

    act = hidden(0)
    for p in range(1, FFN_PARTS):
        act_next = hidden(p)
        finish(p - 1, act)
        act = act_next
    finish(FFN_PARTS - 1, act)


def _ffn_call(x, ada, n2w, w1, w2, fnw):
    bsz, seq, _ = x.shape
    tile = FFN_PART * FFN_PARTS
    const2 = lambda b, s: (0, 0)
    resident = dict(pipeline_mode=pl.Buffered(1))
    return pl.pallas_call(
        _ffn_kernel,
        grid=(bsz, seq // tile),
        in_specs=[
            pl.BlockSpec((1, tile, D_MODEL), lambda b, s: (b, s, 0)),
            pl.BlockSpec((1, N_ADA, D_MODEL), lambda b, s: (b, 0, 0)),
            pl.BlockSpec((1, D_MODEL), const2),
            pl.BlockSpec((D_MODEL, 2 * D_FF), const2, **resident),
            pl.BlockSpec((D_FF, D_MODEL), const2, **resident),
            pl.BlockSpec((1, D_MODEL), const2),
        ],
        out_specs=pl.BlockSpec((1, tile, D_MODEL), lambda b, s: (b, s, 0)),
        out_shape=jax.ShapeDtypeStruct(x.shape, _F32),
        compiler_params=pltpu.CompilerParams(
            dimension_semantics=("arbitrary", "arbitrary"),
            vmem_limit_bytes=VMEM_LIMIT),
        name="ffn",
    )(x, ada, n2w, w1, w2, fnw)


def kernel(x, c, w_ada, b_ada, norm1_w, w_in, w_s, b_s, v_ln_w, v_ln_b, lower_bounds,
           gn_w, w_out, norm2_w, w_ffn_in, w_ffn_out, final_norm_w):
    bsz = x.shape[0]
    depth = w_in.shape[0]
    assert depth == 1 and lower_bounds.shape[0] == 2

    tri = jnp.asarray(np.tril(np.ones((BLOCK, BLOCK), np.float32)), _BF16)

    for l in range(depth):
        ada = _ada_call(c, w_ada[l], b_ada[l][None, :])
        ada = ada.reshape(bsz, N_ADA, D_MODEL)
        mixer = functools.partial(
            _mixer_call, x, ada, norm1_w[l][None, :], w_in[l], w_s[l],
            b_s[l][:, :, None], v_ln_w[l][None, :], v_ln_b[l][None, :],
            lower_bounds, gn_w[l][None, :], w_out[l], tri, w_ffn_in[l], w_ffn_out[l])
        lb = jnp.cumsum(jax.nn.softmax(lower_bounds.astype(_F32), axis=0), axis=0)[l]
        log2_bound = (GUARD_BLOCK // 2) * jnp.max(-jnp.log2(lb))
        x, w1_bf, w2_bf = lax.cond(log2_bound <= GUARD_LOG2_RANGE,
                                   lambda: mixer(guarded=True), lambda: mixer(guarded=False))
        x = _ffn_call(x, ada, norm2_w[l][None, :], w1_bf, w2_bf, final_norm_w[None, :])
    return x
```

```python
import functools

import numpy as np
import jax
import jax.numpy as jnp
from jax import lax
from jax.experimental import pallas as pl
from jax.experimental.pallas import tpu as pltpu

D_MODEL = 1024
D_GMLP = 512
D_HGRN = 512
HEADS = 4
HEAD_DIM = 128
BLOCK = 128
GATE_CHUNK = 64
D_IN = 2 * D_GMLP + 4 * D_HGRN
D_FF = 2816
N_ADA = 6
EPS = 1e-6

LEVELS = (128, 64, 32, 16, 8, 4, 2)
GUARD_LEVELS = (128, 64)
GUARD_BLOCK = 32
GUARD_LOG2_RANGE = 100.0

MIX_TILE = 256
MXU_CHUNK = 256
FFN_PART = 256
FFN_PARTS = 4
VMEM_LIMIT = 56 * 1024 * 1024

SUBLANES = 8

_F32 = jnp.float32
_BF16 = jnp.bfloat16


def _dot(a, b):
    return jnp.dot(a, b, preferred_element_type=_F32)


def _dot_nt(a, b):
    return lax.dot_general(a, b, (((1,), (1,)), ((), ())), preferred_element_type=_F32)


def _silu(x):
    return x * jax.nn.sigmoid(x)


def _gelu(x):
    half = 0.5 * x
    return half + half * lax.erf(x * np.float32(1.0 / np.sqrt(2.0)))


def _level_table(levels, guard_block=None):
    t = np.arange(BLOCK)[:, None]
    s = np.arange(BLOCK)[None, :]
    table = np.full((BLOCK, BLOCK), -1, np.int32)
    if guard_block:
        table[((t // guard_block) == (s // guard_block)) & (s <= t)] = len(levels)
    for idx, n in enumerate(levels):
        h = n // 2
        same = (t // n) == (s // n)
        table[same & ((t % n) >= h) & ((s % n) < h)] = idx
    return table


ADA_ROWS = 256


def _ada_kernel(c_ref, w_ref, b_ref, lbraw_ref, o_ref, bound_ref):
    @pl.when(pl.program_id(0) == 0)
    def _():
        o_ref[...] = jnp.broadcast_to(b_ref[...], o_ref.shape)
        lbraw = lbraw_ref[...]
        lbe = jnp.exp(lbraw - jnp.max(lbraw, axis=0, keepdims=True))
        lb = (lbe / jnp.sum(lbe, axis=0, keepdims=True))[0:1]
        worst = jnp.max(-jnp.log2(lb), axis=1, keepdims=True) * (GUARD_BLOCK // 2)
        bound_ref[...] = jnp.broadcast_to(worst, bound_ref.shape)

    c_act = _silu(c_ref[...]).astype(_BF16)
    o_ref[...] += _dot(c_act, w_ref[...].astype(_BF16))


def _ada_call(c, w_ada, b_ada, lower_bounds):
    bsz = c.shape[0]
    n_out = w_ada.shape[1]
    const2 = lambda k: (0, 0)
    return pl.pallas_call(
        _ada_kernel,
        grid=(D_MODEL // ADA_ROWS,),
        in_specs=[
            pl.BlockSpec((bsz, ADA_ROWS), lambda k: (0, k)),
            pl.BlockSpec((ADA_ROWS, n_out), lambda k: (k, 0)),
            pl.BlockSpec((1, n_out), const2),
            pl.BlockSpec(lower_bounds.shape, const2),
        ],
        out_specs=[
            pl.BlockSpec((bsz, n_out), const2),
            pl.BlockSpec((SUBLANES, HEAD_DIM), const2),
        ],
        out_shape=[
            jax.ShapeDtypeStruct((bsz, n_out), _F32),
            jax.ShapeDtypeStruct((SUBLANES, HEAD_DIM), _F32),
        ],
        compiler_params=pltpu.CompilerParams(dimension_semantics=("arbitrary",)),
        name="ada",
    )(c, w_ada, b_ada, lower_bounds)


def _neg_abs(x):
    bits = pltpu.bitcast(x, jnp.uint32) | jnp.uint32(0x80000000)
    return pltpu.bitcast(bits, _F32)


def _half_select(level_n, q, k, upper):
    h = level_n // 2
    if h % SUBLANES != 0:
        return jnp.where(upper, q, k)
    pieces = []
    for r0 in range(0, BLOCK, level_n):
        pieces += [k[r0:r0 + h], q[r0 + h:r0 + level_n]]
    return jnp.concatenate(pieces, axis=0)


def _decay_exponent(level_n, b_sc, row, col0):
    h = level_n // 2

    def ref_rows(r, n_rows):
        return jnp.broadcast_to(b_sc[pl.ds(r, 1), pl.ds(col0, HEAD_DIM)], (n_rows, HEAD_DIM))

    if level_n < SUBLANES:
        lower_block = (row[:SUBLANES] % SUBLANES) < level_n
        pieces = [jnp.where(lower_block, ref_rows(r0 + h - 1, SUBLANES),
                            ref_rows(r0 + level_n + h - 1, SUBLANES))
                  for r0 in range(0, BLOCK, SUBLANES)]
    else:
        pieces = [ref_rows(r0 + h - 1, level_n) for r0 in range(0, BLOCK, level_n)]
    b_ref = pieces[0] if len(pieces) == 1 else jnp.concatenate(pieces, axis=0)
    b_h = b_sc[:, pl.ds(col0, HEAD_DIM)]
    return b_h - b_ref


def _decay_factor(level_n, b_sc, row, col0):
    return jnp.exp2(_neg_abs(_decay_exponent(level_n, b_sc, row, col0)))


def _modulated_input(x, ada_b, n1w_ref):
    sh1, sc1 = ada_b[:, 0:D_MODEL], ada_b[:, D_MODEL:2 * D_MODEL]
    ms = jnp.mean(x * x, axis=-1, keepdims=True)
    hmod = x * lax.rsqrt(ms + EPS) * n1w_ref[...]
    return (hmod * (1.0 + sc1) + sh1).astype(_BF16)


BF16_ROWS = 16


def _cast_rows(n_rows, n_steps):
    rows = -(-n_rows // n_steps)
    rows = -(-rows // BF16_ROWS) * BF16_ROWS
    while n_rows % rows:
        rows += BF16_ROWS
    return rows


class _Interleaver:
    def __init__(self, tasks, total_weight):
        self._tasks = list(tasks)
        self._total = total_weight
        self._weight = 0
        self._done = 0

    def __call__(self, weight=1):
        self._weight += weight
        due = min(len(self._tasks), self._weight * len(self._tasks) // self._total)
        while self._done < due:
            self._tasks[self._done]()
            self._done += 1

    def flush(self):
        assert self._weight == self._total, (self._weight, self._total)
        while self._done < len(self._tasks):
            self._tasks[self._done]()
            self._done += 1


PRELUDE_WEIGHT = 4
HEAD_LEVELS_WEIGHT = 3
WEIGHT_PER_BLOCK = 2 * PRELUDE_WEIGHT + HEADS * (HEAD_LEVELS_WEIGHT + 1)


def _mixer_kernel(x_ref, xn_ref, ada_ref, n1w_ref, w_in_f32, ws_ref, bs_ref, lnw_ref, lnb_ref,
                  lbraw_ref, gnw_ref, w_out_f32, tri_ref, lvl_ref, w1_f32, w2_f32,
                  o_ref, w1_bf, w2_bf,
                  state_sc, proj_a, proj_b, h_a, h_b, y_sc, b_sc, wsm_sc, w_in_ref, w_out_ref,
                  pre_bf, pre_f32, zl_sc, zt_sc,
                  *, steps_per_seq, n_steps, guarded):
    g = pl.program_id(0)
    batch = g // steps_per_seq
    batch_next = jnp.minimum(g + 1, n_steps - 1) // steps_per_seq
    ada_b = ada_ref[pl.ds(batch, 1), :]

    w1_bf[...] = w1_f32[...].astype(_BF16)
    w2_bf[...] = w2_f32[...].astype(_BF16)

    @pl.when(g == 0)
    def _():
        w_in_ref[...] = w_in_f32[...].astype(_BF16)
        w_out_ref[...] = w_out_f32[...].astype(_BF16)
        h0 = _modulated_input(x_ref[pl.ds(0, MIX_TILE), :], ada_b, n1w_ref)
        proj_a[...] = _dot(h0, w_in_ref[...])
        h_b[...] = _modulated_input(x_ref[pl.ds(MIX_TILE, MIX_TILE), :], ada_b, n1w_ref)
        col = lax.broadcasted_iota(jnp.int32, (BLOCK, BLOCK), 1)
        rowb = lax.broadcasted_iota(jnp.int32, (BLOCK, BLOCK), 0)
        gate_mask = (rowb // GATE_CHUNK) >= (col // GATE_CHUNK)
        for hd in range(HEADS):
            wsm_sc[hd] = jnp.where(gate_mask, ws_ref[hd], 0.0).astype(_BF16)

    @pl.when(g % steps_per_seq == 0)
    def _():
        state_sc[...] = jnp.zeros_like(state_sc)

    lbraw = lbraw_ref[...]
    lbe = jnp.exp(lbraw - jnp.max(lbraw, axis=0, keepdims=True))
    lb = (lbe / jnp.sum(lbe, axis=0, keepdims=True))[0:1]

    levels = GUARD_LEVELS if guarded else LEVELS
    n_products = len(levels) + (1 if guarded else 0)
    row = lax.broadcasted_iota(jnp.int32, (BLOCK, HEAD_DIM), 0)
    lvl = lvl_ref[...]
    upper_rows = [(row % n) >= (n // 2) for n in levels]

    o0 = 2 * D_GMLP

    def prelude_elementwise(proj_sc, prow0, par, background):
        rows = pl.ds(prow0, BLOCK)
        u = proj_sc[rows, pl.ds(0, D_GMLP)]
        v = proj_sc[rows, pl.ds(D_GMLP, D_GMLP)]
        gv = _gelu(v)
        mu = jnp.mean(gv, axis=-1, keepdims=True)
        cen = gv - mu
        var = jnp.mean(cen * cen, axis=-1, keepdims=True)
        vn = cen * lax.rsqrt(var + EPS) * lnw_ref[...] + lnb_ref[...]
        pre_bf[par, 0] = vn.astype(_BF16)
        pre_f32[par, 0] = _gelu(u)
        background(PRELUDE_WEIGHT)
        fl = proj_sc[rows, pl.ds(o0 + D_HGRN, D_HGRN)]
        f = lb + (1.0 - lb) * jax.nn.sigmoid(fl)
        logf = jnp.log2(f)
        p_hi = logf.astype(_BF16)
        pre_bf[par, 1] = p_hi
        pre_bf[par, 2] = (logf - p_hi.astype(_F32)).astype(_BF16)
        pre_f32[par, 1] = f
        background(PRELUDE_WEIGHT)

    def prelude_matmuls(yrow0, par, b_blk):
        tri = tri_ref[...]
        b_blk[...] = _dot(tri, pre_bf[par, 1]) + _dot(tri, pre_bf[par, 2])
        mixed = []
        for hd in range(HEADS):
            mixed.append(_dot(wsm_sc[hd], pre_bf[par, 0, :, pl.ds(hd * HEAD_DIM, HEAD_DIM)]) + bs_ref[hd])
        y_a = pre_f32[par, 0] * jnp.concatenate(mixed, axis=-1)
        y_sc[pl.ds(yrow0, BLOCK), pl.ds(0, D_GMLP)] = y_a.astype(_BF16)

    def level_slabs(n):
        h = n // 2
        if h % SUBLANES or n == 1:
            return list(range(0, BLOCK, SUBLANES))
        return [r for r0 in range(0, BLOCK, n) for r in range(r0 + h, r0 + n, SUBLANES)]

    def stage_head(proj_sc, prow0, par, b_blk, hd, slot):
        rows = pl.ds(prow0, BLOCK)
        c0 = hd * HEAD_DIM
        q_h = _silu(proj_sc[rows, pl.ds(o0 + c0, HEAD_DIM)])
        f_h = pre_f32[par, 1, :, pl.ds(c0, HEAD_DIM)]
        k_h = 1.0 - f_h
        for idx, n in enumerate(levels):
            if n == 2:
                z = jnp.where(upper_rows[idx], q_h * f_h, k_h)
            else:
                z = _half_select(n, q_h, k_h, upper_rows[idx]) * _decay_factor(n, b_blk, row, c0)
            slabs = level_slabs(n)
            lhs = jnp.concatenate([z[r:r + SUBLANES] for r in slabs], axis=0)
            zl_sc[slot, idx, pl.ds(0, len(slabs) * SUBLANES), :] = lhs.astype(_BF16)
            zt_sc[slot, idx] = z.T.astype(_BF16)
        if guarded:
            d = _decay_exponent(GUARD_BLOCK, b_blk, row, c0)
            zl_sc[slot, len(levels)] = (q_h * jnp.exp2(d)).astype(_BF16)
            zt_sc[slot, len(levels)] = (k_h * jnp.exp2(-d)).T.astype(_BF16)
        return q_h, k_h

    def head_levels(slot, background):
        attn = [jnp.zeros((SUBLANES, BLOCK), _F32)] * (BLOCK // SUBLANES)
        for idx in range(n_products):
            slabs = level_slabs(levels[idx]) if idx < len(levels) else level_slabs(1)
            res = _dot(zl_sc[slot, idx, pl.ds(0, len(slabs) * SUBLANES), :], zt_sc[slot, idx])
            for j, r in enumerate(slabs):
                valid = lvl[r:r + SUBLANES] == idx
                attn[r // SUBLANES] = jnp.where(valid, res[j * SUBLANES:(j + 1) * SUBLANES],
                                                attn[r // SUBLANES])
        background(HEAD_LEVELS_WEIGHT)
        return jnp.concatenate(attn, axis=0)

    def head_tail(proj_sc, prow0, yrow0, b_blk, hd, attn, q_h, k_h):
        rows = pl.ds(prow0, BLOCK)
        c0 = hd * HEAD_DIM
        v_h = proj_sc[rows, pl.ds(o0 + 2 * D_HGRN + c0, HEAD_DIM)]
        g_h = proj_sc[rows, pl.ds(o0 + 3 * D_HGRN + c0, HEAD_DIM)]
        b_h = b_blk[:, pl.ds(c0, HEAD_DIM)]

        s_t = state_sc[hd]
        q_hat = (q_h * jnp.exp2(b_h)).astype(_BF16)
        v_t = v_h.T.astype(_BF16)
        lhs = jnp.concatenate([attn.astype(_BF16), q_hat], axis=1)
        rhs_t = jnp.concatenate([v_t, s_t.astype(_BF16)], axis=1)
        o_h = _dot_nt(lhs, rhs_t)
        if not guarded:
            o_h = o_h + jnp.sum(q_h * k_h, axis=-1, keepdims=True) * v_h

        b_last = b_blk[pl.ds(BLOCK - 1, 1), pl.ds(c0, HEAD_DIM)]
        k_hat = (k_h * jnp.exp2(b_last - b_h)).astype(_BF16)
        state_sc[hd] = s_t * jnp.exp2(b_last) + _dot(v_t, k_hat)

        oms = jnp.mean(o_h * o_h, axis=-1, keepdims=True)
        y_b = o_h * lax.rsqrt(oms + EPS) * gnw_ref[...] * _silu(g_h)
        y_sc[pl.ds(yrow0, BLOCK), pl.ds(D_GMLP + c0, HEAD_DIM)] = y_b.astype(_BF16)

    blocks_per_tile = MIX_TILE // BLOCK
    g1 = ada_b[:, 2 * D_MODEL:3 * D_MODEL]

    def projection_tasks(h_sc, dst):
        def chunk(c):
            cols = pl.ds(c * MXU_CHUNK, MXU_CHUNK)
            dst[:, cols] = _dot(h_sc[...], w_in_ref[:, cols])
        return [functools.partial(chunk, c) for c in range(D_IN // MXU_CHUNK)]

    def finish_tasks(t):
        trow = pl.ds(t * MIX_TILE, MIX_TILE)

        def chunk(c):
            cols = pl.ds(c * MXU_CHUNK, MXU_CHUNK)
            mix = _dot(y_sc[trow, :], w_out_ref[:, cols])
            o_ref[trow, cols] = x_ref[trow, cols] + g1[:, c * MXU_CHUNK:(c + 1) * MXU_CHUNK] * mix
        return [functools.partial(chunk, c) for c in range(D_MODEL // MXU_CHUNK)]

    def mix_tile(proj_sc, t, tasks):
        background = _Interleaver(tasks, blocks_per_tile * WEIGHT_PER_BLOCK)
        heads = [(i, hd) for i in range(blocks_per_tile) for hd in range(HEADS)]

        def blk(i):
            return i * BLOCK, t * MIX_TILE + i * BLOCK, i % 2, b_sc.at[t * blocks_per_tile + i]

        prow0, yrow0, par, b_blk = blk(0)
        prelude_elementwise(proj_sc, prow0, par, background)
        prelude_matmuls(yrow0, par, b_blk)
        staged = stage_head(proj_sc, prow0, par, b_blk, 0, 0)
        pending = None
        for j, (i, hd) in enumerate(heads):
            prow0, yrow0, par, b_blk = blk(i)
            if hd == 1 and i + 1 < blocks_per_tile:
                prelude_elementwise(proj_sc, blk(i + 1)[0], blk(i + 1)[2], background)
            if hd == 2 and i + 1 < blocks_per_tile:
                prelude_matmuls(blk(i + 1)[1], blk(i + 1)[2], blk(i + 1)[3])
            staged_next = None
            if j + 1 < len(heads):
                ni, nhd = heads[j + 1]
                staged_next = stage_head(proj_sc, blk(ni)[0], blk(ni)[2], blk(ni)[3], nhd, (j + 1) % 2)
            attn = head_levels(j % 2, background)
            if pending is not None:
                pending()
            pending = functools.partial(head_tail, proj_sc, prow0, yrow0, b_blk, hd, attn, *staged)
            background()
            staged = staged_next
        pending()
        background.flush()

    mix_tile(proj_a, 0, projection_tasks(h_b, proj_b))
    ada_next = ada_ref[pl.ds(batch_next, 1), :]
    h_a[...] = _modulated_input(xn_ref[pl.ds(0, MIX_TILE), :], ada_next, n1w_ref)
    mix_tile(proj_b, 1, finish_tasks(0) + projection_tasks(h_a, proj_a))
    h_b[...] = _modulated_input(xn_ref[pl.ds(MIX_TILE, MIX_TILE), :], ada_next, n1w_ref)
    for task in finish_tasks(1):
        task()


def _mixer_call(x, ada, n1w, w_in, w_s, b_s, lnw, lnb, lbraw, gnw, w_out, tri, w1, w2, guarded):
    bsz, seq, _ = x.shape
    step_rows = 2 * MIX_TILE
    steps_per_seq = seq // step_rows
    n_steps = bsz * steps_per_seq
    x2 = x.reshape(bsz * seq, D_MODEL)
    const2 = lambda g: (0, 0)
    const3 = lambda g: (0, 0, 0)
    resident = dict(pipeline_mode=pl.Buffered(1))
    w1_rows = _cast_rows(w1.shape[0], n_steps)
    w2_rows = _cast_rows(w2.shape[0], n_steps)
    w1_map = lambda g: (jnp.minimum(g, w1.shape[0] // w1_rows - 1), 0)
    w2_map = lambda g: (jnp.minimum(g, w2.shape[0] // w2_rows - 1), 0)
    if guarded:
        lvl = jnp.asarray(_level_table(GUARD_LEVELS, GUARD_BLOCK))
    else:
        lvl = jnp.asarray(_level_table(LEVELS))
    kernel = functools.partial(_mixer_kernel, steps_per_seq=steps_per_seq, n_steps=n_steps,
                               guarded=guarded)
    out, w1_bf, w2_bf = pl.pallas_call(
        kernel,
        grid=(n_steps,),
        in_specs=[
            pl.BlockSpec((step_rows, D_MODEL), lambda g: (g, 0)),
            pl.BlockSpec((step_rows, D_MODEL), lambda g: (jnp.minimum(g + 1, n_steps - 1), 0)),
            pl.BlockSpec((bsz, N_ADA * D_MODEL), const2),
            pl.BlockSpec((1, D_MODEL), const2),
            pl.BlockSpec((D_MODEL, D_IN), const2, **resident),
            pl.BlockSpec((HEADS, BLOCK, BLOCK), const3),
            pl.BlockSpec((HEADS, BLOCK, 1), const3),
            pl.BlockSpec((1, D_GMLP), const2),
            pl.BlockSpec((1, D_GMLP), const2),
            pl.BlockSpec((2, D_HGRN), const2),
            pl.BlockSpec((1, HEAD_DIM), const2),
            pl.BlockSpec((D_MODEL, D_MODEL), const2, **resident),
            pl.BlockSpec((BLOCK, BLOCK), const2),
            pl.BlockSpec((BLOCK, BLOCK), const2),
            pl.BlockSpec((w1_rows, w1.shape[1]), w1_map),
            pl.BlockSpec((w2_rows, w2.shape[1]), w2_map),
        ],
        out_specs=[
            pl.BlockSpec((step_rows, D_MODEL), lambda g: (g, 0)),
            pl.BlockSpec((w1_rows, w1.shape[1]), w1_map),
            pl.BlockSpec((w2_rows, w2.shape[1]), w2_map),
        ],
        out_shape=[
            jax.ShapeDtypeStruct(x2.shape, _F32),
            jax.ShapeDtypeStruct(w1.shape, _BF16),
            jax.ShapeDtypeStruct(w2.shape, _BF16),
        ],
        scratch_shapes=[
            pltpu.VMEM((HEADS, HEAD_DIM, HEAD_DIM), _F32),
            pltpu.VMEM((MIX_TILE, D_IN), _F32),
            pltpu.VMEM((MIX_TILE, D_IN), _F32),
            pltpu.VMEM((MIX_TILE, D_MODEL), _BF16),
            pltpu.VMEM((MIX_TILE, D_MODEL), _BF16),
            pltpu.VMEM((step_rows, D_MODEL), _BF16),
            pltpu.VMEM((step_rows // BLOCK, BLOCK, D_HGRN), _F32),
            pltpu.VMEM((HEADS, BLOCK, BLOCK), _BF16),
            pltpu.VMEM((D_MODEL, D_IN), _BF16),
            pltpu.VMEM((D_MODEL, D_MODEL), _BF16),
            pltpu.VMEM((2, 3, BLOCK, D_HGRN), _BF16),
            pltpu.VMEM((2, 2, BLOCK, D_HGRN), _F32),
            pltpu.VMEM((2, len(LEVELS), BLOCK, HEAD_DIM), _BF16),
            pltpu.VMEM((2, len(LEVELS), HEAD_DIM, BLOCK), _BF16),
        ],
        compiler_params=pltpu.CompilerParams(
            dimension_semantics=("arbitrary",),
            vmem_limit_bytes=VMEM_LIMIT),
        name="mixer",
    )(x2, x2, ada, n1w, w_in, w_s, b_s, lnw, lnb, lbraw, gnw, w_out, tri, lvl, w1, w2)
    return out.reshape(x.shape), w1_bf, w2_bf


def _ffn_kernel(x_ref, ada_ref, n2w_ref, w1_ref, w2_ref, fnw_ref, o_ref):
    ada = ada_ref[pl.ds(pl.program_id(0), 1), :]
    sh2, sc2, g2 = (ada[:, k * D_MODEL:(k + 1) * D_MODEL] for k in (3, 4, 5))

    def hidden(p):
        x = x_ref[0, pl.ds(p * FFN_PART, FFN_PART), :]
        ms = jnp.mean(x * x, axis=-1, keepdims=True)
        hmod = x * lax.rsqrt(ms + EPS) * n2w_ref[...]
        hmod = (hmod * (1.0 + sc2) + sh2).astype(_BF16)
        gu = _dot(hmod, w1_ref[...])
        return (_silu(gu[:, :D_FF]) * gu[:, D_FF:]).astype(_BF16)

    def finish(p, act):
        rows = pl.ds(p * FFN_PART, FFN_PART)
        x2 = x_ref[0, rows, :] + g2 * _dot(act, w2_ref[...])
        ms2 = jnp.mean(x2 * x2, axis=-1, keepdims=True)
        o_ref[0, rows, :] = x2 * lax.rsqrt(ms2 + EPS) * fnw_ref[...]

    act = hidden(0)
    for p in range(1, FFN_PARTS):
        act_next = hidden(p)
        finish(p - 1, act)
        act = act_next
    finish(FFN_PARTS - 1, act)


def _ffn_call(x, ada, n2w, w1, w2, fnw):
    bsz, seq, _ = x.shape
    tile = FFN_PART * FFN_PARTS
    const2 = lambda b, s: (0, 0)
    resident = dict(pipeline_mode=pl.Buffered(1))
    return pl.pallas_call(
        _ffn_kernel,
        grid=(bsz, seq // tile),
        in_specs=[
            pl.BlockSpec((1, tile, D_MODEL), lambda b, s: (b, s, 0)),
            pl.BlockSpec((bsz, N_ADA * D_MODEL), const2),
            pl.BlockSpec((1, D_MODEL), const2),
            pl.BlockSpec((D_MODEL, 2 * D_FF), const2, **resident),
            pl.BlockSpec((D_FF, D_MODEL), const2, **resident),
            pl.BlockSpec((1, D_MODEL), const2),
        ],
        out_specs=pl.BlockSpec((1, tile, D_MODEL), lambda b, s: (b, s, 0)),
        out_shape=jax.ShapeDtypeStruct(x.shape, _F32),
        compiler_params=pltpu.CompilerParams(
            dimension_semantics=("arbitrary", "arbitrary"),
            vmem_limit_bytes=VMEM_LIMIT),
        name="ffn",
    )(x, ada, n2w, w1, w2, fnw)


def kernel(x, c, w_ada, b_ada, norm1_w, w_in, w_s, b_s, v_ln_w, v_ln_b, lower_bounds,
           gn_w, w_out, norm2_w, w_ffn_in, w_ffn_out, final_norm_w):
    bsz = x.shape[0]
    depth = w_in.shape[0]
    assert depth == 1 and lower_bounds.shape[0] == 2

    tri = jnp.asarray(np.tril(np.ones((BLOCK, BLOCK), np.float32)), _BF16)

    for l in range(depth):
        ada, log2_bound = _ada_call(c, w_ada[l], b_ada[l][None, :], lower_bounds)
        mixer = functools.partial(
            _mixer_call, x, ada, norm1_w[l][None, :], w_in[l], w_s[l],
            b_s[l][:, :, None], v_ln_w[l][None, :], v_ln_b[l][None, :],
            lower_bounds, gn_w[l][None, :], w_out[l], tri, w_ffn_in[l], w_ffn_out[l])
        x, w1_bf, w2_bf = lax.cond(log2_bound[0, 0] <= GUARD_LOG2_RANGE,
                                   lambda: mixer(guarded=True), lambda: mixer(guarded=False))
        x = _ffn_call(x, ada, norm2_w[l][None, :], w1_bf, w2_bf, final_norm_w[None, :])
    return x
```

```python
import functools

import numpy as np
import jax
import jax.numpy as jnp
from jax import lax
from jax.experimental import pallas as pl
from jax.experimental.pallas import tpu as pltpu

D_MODEL = 1024
D_GMLP = 512
D_HGRN = 512
HEADS = 4
HEAD_DIM = 128
BLOCK = 128
GATE_CHUNK = 64
D_IN = 2 * D_GMLP + 4 * D_HGRN
D_FF = 2816
N_ADA = 6
EPS = 1e-6

LEVELS = (128, 64, 32, 16, 8, 4, 2)
GUARD_LEVELS = (128, 64)
GUARD_BLOCK = 32
GUARD_LOG2_RANGE = 100.0

MIX_TILE = 256
MXU_CHUNK = 256
FFN_PART = 256
FFN_PARTS = 4
VMEM_LIMIT = 56 * 1024 * 1024

SUBLANES = 8

_F32 = jnp.float32
_BF16 = jnp.bfloat16


def _dot(a, b):
    return jnp.dot(a, b, preferred_element_type=_F32)


def _silu(x):
    return x * jax.nn.sigmoid(x)


def _gelu(x):
    half = 0.5 * x
    return half + half * lax.erf(x * np.float32(1.0 / np.sqrt(2.0)))


def _level_table(levels, guard_block=None):
    t = np.arange(BLOCK)[:, None]
    s = np.arange(BLOCK)[None, :]
    table = np.full((BLOCK, BLOCK), -1, np.int32)
    if guard_block:
        table[((t // guard_block) == (s // guard_block)) & (s <= t)] = len(levels)
    for idx, n in enumerate(levels):
        h = n // 2
        same = (t // n) == (s // n)
        table[same & ((t % n) >= h) & ((s % n) < h)] = idx
    return table


ADA_ROWS = 256


def _ada_kernel(c_ref, w_ref, b_ref, lbraw_ref, o_ref, bound_ref):
    @pl.when(pl.program_id(0) == 0)
    def _():
        o_ref[...] = jnp.broadcast_to(b_ref[...], o_ref.shape)
        lbraw = lbraw_ref[...]
        lbe = jnp.exp(lbraw - jnp.max(lbraw, axis=0, keepdims=True))
        lb = (lbe / jnp.sum(lbe, axis=0, keepdims=True))[0:1]
        worst = jnp.max(-jnp.log2(lb), axis=1, keepdims=True) * (GUARD_BLOCK // 2)
        bound_ref[...] = jnp.broadcast_to(worst, bound_ref.shape)

    c_act = _silu(c_ref[...]).astype(_BF16)
    o_ref[...] += _dot(c_act, w_ref[...].astype(_BF16))


def _ada_call(c, w_ada, b_ada, lower_bounds):
    bsz = c.shape[0]
    n_out = w_ada.shape[1]
    const2 = lambda k: (0, 0)
    return pl.pallas_call(
        _ada_kernel,
        grid=(D_MODEL // ADA_ROWS,),
        in_specs=[
            pl.BlockSpec((bsz, ADA_ROWS), lambda k: (0, k)),
            pl.BlockSpec((ADA_ROWS, n_out), lambda k: (k, 0)),
            pl.BlockSpec((1, n_out), const2),
            pl.BlockSpec(lower_bounds.shape, const2),
        ],
        out_specs=[
            pl.BlockSpec((bsz, n_out), const2),
            pl.BlockSpec((SUBLANES, HEAD_DIM), const2),
        ],
        out_shape=[
            jax.ShapeDtypeStruct((bsz, n_out), _F32),
            jax.ShapeDtypeStruct((SUBLANES, HEAD_DIM), _F32),
        ],
        compiler_params=pltpu.CompilerParams(dimension_semantics=("arbitrary",)),
        name="ada",
    )(c, w_ada, b_ada, lower_bounds)


def _neg_abs(x):
    bits = pltpu.bitcast(x, jnp.uint32) | jnp.uint32(0x80000000)
    return pltpu.bitcast(bits, _F32)


def _half_select(level_n, q, k, upper):
    h = level_n // 2
    if h % SUBLANES != 0:
        return jnp.where(upper, q, k)
    pieces = []
    for r0 in range(0, BLOCK, level_n):
        pieces += [k[r0:r0 + h], q[r0 + h:r0 + level_n]]
    return jnp.concatenate(pieces, axis=0)


def _decay_exponent(level_n, b_sc, row, col0):
    h = level_n // 2

    def ref_rows(r, n_rows):
        return jnp.broadcast_to(b_sc[pl.ds(r, 1), pl.ds(col0, HEAD_DIM)], (n_rows, HEAD_DIM))

    if level_n < SUBLANES:
        lower_block = (row[:SUBLANES] % SUBLANES) < level_n
        pieces = [jnp.where(lower_block, ref_rows(r0 + h - 1, SUBLANES),
                            ref_rows(r0 + level_n + h - 1, SUBLANES))
                  for r0 in range(0, BLOCK, SUBLANES)]
    else:
        pieces = [ref_rows(r0 + h - 1, level_n) for r0 in range(0, BLOCK, level_n)]
    b_ref = pieces[0] if len(pieces) == 1 else jnp.concatenate(pieces, axis=0)
    b_h = b_sc[:, pl.ds(col0, HEAD_DIM)]
    return b_h - b_ref


def _decay_factor(level_n, b_sc, row, col0):
    return jnp.exp2(_neg_abs(_decay_exponent(level_n, b_sc, row, col0)))


def _modulated_input(x, ada_b, n1w_ref):
    sh1, sc1 = ada_b[:, 0:D_MODEL], ada_b[:, D_MODEL:2 * D_MODEL]
    ms = jnp.mean(x * x, axis=-1, keepdims=True)
    hmod = x * lax.rsqrt(ms + EPS) * n1w_ref[...]
    return (hmod * (1.0 + sc1) + sh1).astype(_BF16)


BF16_ROWS = 16


def _cast_rows(n_rows, n_steps):
    rows = -(-n_rows // n_steps)
    rows = -(-rows // BF16_ROWS) * BF16_ROWS
    while n_rows % rows:
        rows += BF16_ROWS
    return rows


class _Interleaver:
    def __init__(self, tasks, total_weight):
        self._tasks = list(tasks)
        self._total = total_weight
        self._weight = 0
        self._done = 0

    def __call__(self, weight=1):
        self._weight += weight
        due = min(len(self._tasks), self._weight * len(self._tasks) // self._total)
        while self._done < due:
            self._tasks[self._done]()
            self._done += 1

    def flush(self):
        assert self._weight == self._total, (self._weight, self._total)
        while self._done < len(self._tasks):
            self._tasks[self._done]()
            self._done += 1


PRELUDE_WEIGHT = 4
HEAD_LEVELS_WEIGHT = 3
WEIGHT_PER_BLOCK = 2 * PRELUDE_WEIGHT + HEADS * (HEAD_LEVELS_WEIGHT + 1)


def _mixer_kernel(x_ref, xn_ref, ada_ref, n1w_ref, w_in_f32, ws_ref, bs_ref, lnw_ref, lnb_ref,
                  lbraw_ref, gnw_ref, w_out_f32, tri_ref, lvl_ref, w1_f32, w2_f32,
                  o_ref, w1_bf, w2_bf,
                  state_sc, proj_a, proj_b, h_a, h_b, y_sc, b_sc, wsm_sc, w_in_ref, w_out_ref,
                  pre_bf, pre_f32, zl_sc, zt_sc,
                  *, steps_per_seq, n_steps, guarded):
    g = pl.program_id(0)
    batch = g // steps_per_seq
    batch_next = jnp.minimum(g + 1, n_steps - 1) // steps_per_seq
    ada_b = ada_ref[pl.ds(batch, 1), :]

    w1_bf[...] = w1_f32[...].astype(_BF16)
    w2_bf[...] = w2_f32[...].astype(_BF16)

    @pl.when(g == 0)
    def _():
        w_in_ref[...] = w_in_f32[...].astype(_BF16)
        w_out_ref[...] = w_out_f32[...].astype(_BF16)
        h0 = _modulated_input(x_ref[pl.ds(0, MIX_TILE), :], ada_b, n1w_ref)
        proj_a[...] = _dot(h0, w_in_ref[...])
        h_b[...] = _modulated_input(x_ref[pl.ds(MIX_TILE, MIX_TILE), :], ada_b, n1w_ref)
        col = lax.broadcasted_iota(jnp.int32, (BLOCK, BLOCK), 1)
        rowb = lax.broadcasted_iota(jnp.int32, (BLOCK, BLOCK), 0)
        gate_mask = (rowb // GATE_CHUNK) >= (col // GATE_CHUNK)
        for hd in range(HEADS):
            wsm_sc[hd] = jnp.where(gate_mask, ws_ref[hd], 0.0).astype(_BF16)

    @pl.when(g % steps_per_seq == 0)
    def _():
        state_sc[...] = jnp.zeros_like(state_sc)

    lbraw = lbraw_ref[...]
    lbe = jnp.exp(lbraw - jnp.max(lbraw, axis=0, keepdims=True))
    lb = (lbe / jnp.sum(lbe, axis=0, keepdims=True))[0:1]

    levels = GUARD_LEVELS if guarded else LEVELS
    n_products = len(levels) + (1 if guarded else 0)
    row = lax.broadcasted_iota(jnp.int32, (BLOCK, HEAD_DIM), 0)
    lvl = lvl_ref[...]
    upper_rows = [(row % n) >= (n // 2) for n in levels]

    o0 = 2 * D_GMLP

    def prelude_elementwise(proj_sc, prow0, par, background):
        rows = pl.ds(prow0, BLOCK)
        u = proj_sc[rows, pl.ds(0, D_GMLP)]
        v = proj_sc[rows, pl.ds(D_GMLP, D_GMLP)]
        gv = _gelu(v)
        mu = jnp.mean(gv, axis=-1, keepdims=True)
        cen = gv - mu
        var = jnp.mean(cen * cen, axis=-1, keepdims=True)
        vn = cen * lax.rsqrt(var + EPS) * lnw_ref[...] + lnb_ref[...]
        pre_bf[par, 0] = vn.astype(_BF16)
        pre_f32[par, 0] = _gelu(u)
        background(PRELUDE_WEIGHT)
        fl = proj_sc[rows, pl.ds(o0 + D_HGRN, D_HGRN)]
        f = lb + (1.0 - lb) * jax.nn.sigmoid(fl)
        logf = jnp.log2(f)
        p_hi = logf.astype(_BF16)
        pre_bf[par, 1] = p_hi
        pre_bf[par, 2] = (logf - p_hi.astype(_F32)).astype(_BF16)
        pre_f32[par, 1] = f
        background(PRELUDE_WEIGHT)

    def prelude_matmuls(yrow0, par, b_blk):
        tri = tri_ref[...]
        b_blk[...] = _dot(tri, pre_bf[par, 1]) + _dot(tri, pre_bf[par, 2])
        mixed = []
        for hd in range(HEADS):
            mixed.append(_dot(wsm_sc[hd], pre_bf[par, 0, :, pl.ds(hd * HEAD_DIM, HEAD_DIM)]) + bs_ref[hd])
        y_a = pre_f32[par, 0] * jnp.concatenate(mixed, axis=-1)
        y_sc[pl.ds(yrow0, BLOCK), pl.ds(0, D_GMLP)] = y_a.astype(_BF16)

    def level_slabs(n):
        h = n // 2
        if h % SUBLANES or n == 1:
            return list(range(0, BLOCK, SUBLANES))
        return [r for r0 in range(0, BLOCK, n) for r in range(r0 + h, r0 + n, SUBLANES)]

    def stage_head(proj_sc, prow0, par, b_blk, hd, slot):
        rows = pl.ds(prow0, BLOCK)
        c0 = hd * HEAD_DIM
        q_h = _silu(proj_sc[rows, pl.ds(o0 + c0, HEAD_DIM)])
        f_h = pre_f32[par, 1, :, pl.ds(c0, HEAD_DIM)]
        k_h = 1.0 - f_h
        for idx, n in enumerate(levels):
            if n == 2:
                z = jnp.where(upper_rows[idx], q_h * f_h, k_h)
            else:
                z = _half_select(n, q_h, k_h, upper_rows[idx]) * _decay_factor(n, b_blk, row, c0)
            slabs = level_slabs(n)
            lhs = jnp.concatenate([z[r:r + SUBLANES] for r in slabs], axis=0)
            zl_sc[slot, idx, pl.ds(0, len(slabs) * SUBLANES), :] = lhs.astype(_BF16)
            zt_sc[slot, idx] = z.T.astype(_BF16)
        if guarded:
            d = _decay_exponent(GUARD_BLOCK, b_blk, row, c0)
            zl_sc[slot, len(levels)] = (q_h * jnp.exp2(d)).astype(_BF16)
            zt_sc[slot, len(levels)] = (k_h * jnp.exp2(-d)).T.astype(_BF16)
        return q_h, k_h

    def head_levels(slot, background):
        attn = [jnp.zeros((SUBLANES, BLOCK), _F32)] * (BLOCK // SUBLANES)
        for idx in range(n_products):
            slabs = level_slabs(levels[idx]) if idx < len(levels) else level_slabs(1)
            res = _dot(zl_sc[slot, idx, pl.ds(0, len(slabs) * SUBLANES), :], zt_sc[slot, idx])
            for j, r in enumerate(slabs):
                valid = lvl[r:r + SUBLANES] == idx
                attn[r // SUBLANES] = jnp.where(valid, res[j * SUBLANES:(j + 1) * SUBLANES],
                                                attn[r // SUBLANES])
        background(HEAD_LEVELS_WEIGHT)
        return jnp.concatenate(attn, axis=0)

    def head_tail(proj_sc, prow0, yrow0, b_blk, hd, attn, q_h, k_h):
        rows = pl.ds(prow0, BLOCK)
        c0 = hd * HEAD_DIM
        v_h = proj_sc[rows, pl.ds(o0 + 2 * D_HGRN + c0, HEAD_DIM)]
        g_h = proj_sc[rows, pl.ds(o0 + 3 * D_HGRN + c0, HEAD_DIM)]
        b_h = b_blk[:, pl.ds(c0, HEAD_DIM)]

        s_t = state_sc[hd]
        q_hat = (q_h * jnp.exp2(b_h)).astype(_BF16)
        v_t = v_h.T.astype(_BF16)
        lhs = jnp.concatenate([attn.astype(_BF16), q_hat], axis=1)
        rhs = jnp.concatenate([v_h.astype(_BF16), s_t.T.astype(_BF16)], axis=0)
        o_h = _dot(lhs, rhs)
        if not guarded:
            o_h = o_h + jnp.sum(q_h * k_h, axis=-1, keepdims=True) * v_h

        b_last = b_blk[pl.ds(BLOCK - 1, 1), pl.ds(c0, HEAD_DIM)]
        k_hat = (k_h * jnp.exp2(b_last - b_h)).astype(_BF16)
        state_sc[hd] = s_t * jnp.exp2(b_last) + _dot(v_t, k_hat)

        oms = jnp.mean(o_h * o_h, axis=-1, keepdims=True)
        y_b = o_h * lax.rsqrt(oms + EPS) * gnw_ref[...] * _silu(g_h)
        y_sc[pl.ds(yrow0, BLOCK), pl.ds(D_GMLP + c0, HEAD_DIM)] = y_b.astype(_BF16)

    blocks_per_tile = MIX_TILE // BLOCK
    g1 = ada_b[:, 2 * D_MODEL:3 * D_MODEL]

    def projection_tasks(h_sc, dst):
        def chunk(c):
            cols = pl.ds(c * MXU_CHUNK, MXU_CHUNK)
            dst[:, cols] = _dot(h_sc[...], w_in_ref[:, cols])
        return [functools.partial(chunk, c) for c in range(D_IN // MXU_CHUNK)]

    def finish_tasks(t):
        trow = pl.ds(t * MIX_TILE, MIX_TILE)

        def chunk(c):
            cols = pl.ds(c * MXU_CHUNK, MXU_CHUNK)
            mix = _dot(y_sc[trow, :], w_out_ref[:, cols])
            o_ref[trow, cols] = x_ref[trow, cols] + g1[:, c * MXU_CHUNK:(c + 1) * MXU_CHUNK] * mix
        return [functools.partial(chunk, c) for c in range(D_MODEL // MXU_CHUNK)]

    def mix_tile(proj_sc, t, tasks):
        background = _Interleaver(tasks, blocks_per_tile * WEIGHT_PER_BLOCK)
        heads = [(i, hd) for i in range(blocks_per_tile) for hd in range(HEADS)]

        def blk(i):
            return i * BLOCK, t * MIX_TILE + i * BLOCK, i % 2, b_sc.at[t * blocks_per_tile + i]

        prow0, yrow0, par, b_blk = blk(0)
        prelude_elementwise(proj_sc, prow0, par, background)
        prelude_matmuls(yrow0, par, b_blk)
        staged = stage_head(proj_sc, prow0, par, b_blk, 0, 0)
        pending = None
        for j, (i, hd) in enumerate(heads):
            prow0, yrow0, par, b_blk = blk(i)
            if hd == 1 and i + 1 < blocks_per_tile:
                prelude_elementwise(proj_sc, blk(i + 1)[0], blk(i + 1)[2], background)
            if hd == 2 and i + 1 < blocks_per_tile:
                prelude_matmuls(blk(i + 1)[1], blk(i + 1)[2], blk(i + 1)[3])
            staged_next = None
            if j + 1 < len(heads):
                ni, nhd = heads[j + 1]
                staged_next = stage_head(proj_sc, blk(ni)[0], blk(ni)[2], blk(ni)[3], nhd, (j + 1) % 2)
            attn = head_levels(j % 2, background)
            if pending is not None:
                pending()
            pending = functools.partial(head_tail, proj_sc, prow0, yrow0, b_blk, hd, attn, *staged)
            background()
            staged = staged_next
        pending()
        background.flush()

    mix_tile(proj_a, 0, projection_tasks(h_b, proj_b))
    ada_next = ada_ref[pl.ds(batch_next, 1), :]
    h_a[...] = _modulated_input(xn_ref[pl.ds(0, MIX_TILE), :], ada_next, n1w_ref)
    mix_tile(proj_b, 1, finish_tasks(0) + projection_tasks(h_a, proj_a))
    h_b[...] = _modulated_input(xn_ref[pl.ds(MIX_TILE, MIX_TILE), :], ada_next, n1w_ref)
    for task in finish_tasks(1):
        task()


def _mixer_call(x, ada, n1w, w_in, w_s, b_s, lnw, lnb, lbraw, gnw, w_out, tri, w1, w2, guarded):
    bsz, seq, _ = x.shape
    step_rows = 2 * MIX_TILE
    steps_per_seq = seq // step_rows
    n_steps = bsz * steps_per_seq
    x2 = x.reshape(bsz * seq, D_MODEL)
    const2 = lambda g: (0, 0)
    const3 = lambda g: (0, 0, 0)
    resident = dict(pipeline_mode=pl.Buffered(1))
    w1_rows = _cast_rows(w1.shape[0], n_steps)
    w2_rows = _cast_rows(w2.shape[0], n_steps)
    w1_map = lambda g: (jnp.minimum(g, w1.shape[0] // w1_rows - 1), 0)
    w2_map = lambda g: (jnp.minimum(g, w2.shape[0] // w2_rows - 1), 0)
    if guarded:
        lvl = jnp.asarray(_level_table(GUARD_LEVELS, GUARD_BLOCK))
    else:
        lvl = jnp.asarray(_level_table(LEVELS))
    kernel = functools.partial(_mixer_kernel, steps_per_seq=steps_per_seq, n_steps=n_steps,
                               guarded=guarded)
    out, w1_bf, w2_bf = pl.pallas_call(
        kernel,
        grid=(n_steps,),
        in_specs=[
            pl.BlockSpec((step_rows, D_MODEL), lambda g: (g, 0)),
            pl.BlockSpec((step_rows, D_MODEL), lambda g: (jnp.minimum(g + 1, n_steps - 1), 0)),
            pl.BlockSpec((bsz, N_ADA * D_MODEL), const2),
            pl.BlockSpec((1, D_MODEL), const2),
            pl.BlockSpec((D_MODEL, D_IN), const2, **resident),
            pl.BlockSpec((HEADS, BLOCK, BLOCK), const3),
            pl.BlockSpec((HEADS, BLOCK, 1), const3),
            pl.BlockSpec((1, D_GMLP), const2),
            pl.BlockSpec((1, D_GMLP), const2),
            pl.BlockSpec((2, D_HGRN), const2),
            pl.BlockSpec((1, HEAD_DIM), const2),
            pl.BlockSpec((D_MODEL, D_MODEL), const2, **resident),
            pl.BlockSpec((BLOCK, BLOCK), const2),
            pl.BlockSpec((BLOCK, BLOCK), const2),
            pl.BlockSpec((w1_rows, w1.shape[1]), w1_map),
            pl.BlockSpec((w2_rows, w2.shape[1]), w2_map),
        ],
        out_specs=[
            pl.BlockSpec((step_rows, D_MODEL), lambda g: (g, 0)),
            pl.BlockSpec((w1_rows, w1.shape[1]), w1_map),
            pl.BlockSpec((w2_rows, w2.shape[1]), w2_map),
        ],
        out_shape=[
            jax.ShapeDtypeStruct(x2.shape, _F32),
            jax.ShapeDtypeStruct(w1.shape, _BF16),
            jax.ShapeDtypeStruct(w2.shape, _BF16),
        ],
        scratch_shapes=[
            pltpu.VMEM((HEADS, HEAD_DIM, HEAD_DIM), _F32),
            pltpu.VMEM((MIX_TILE, D_IN), _F32),
            pltpu.VMEM((MIX_TILE, D_IN), _F32),
            pltpu.VMEM((MIX_TILE, D_MODEL), _BF16),
            pltpu.VMEM((MIX_TILE, D_MODEL), _BF16),
            pltpu.VMEM((step_rows, D_MODEL), _BF16),
            pltpu.VMEM((step_rows // BLOCK, BLOCK, D_HGRN), _F32),
            pltpu.VMEM((HEADS, BLOCK, BLOCK), _BF16),
            pltpu.VMEM((D_MODEL, D_IN), _BF16),
            pltpu.VMEM((D_MODEL, D_MODEL), _BF16),
            pltpu.VMEM((2, 3, BLOCK, D_HGRN), _BF16),
            pltpu.VMEM((2, 2, BLOCK, D_HGRN), _F32),
            pltpu.VMEM((2, len(LEVELS), BLOCK, HEAD_DIM), _BF16),
            pltpu.VMEM((2, len(LEVELS), HEAD_DIM, BLOCK), _BF16),
        ],
        compiler_params=pltpu.CompilerParams(
            dimension_semantics=("arbitrary",),
            vmem_limit_bytes=VMEM_LIMIT),
        name="mixer",
    )(x2, x2, ada, n1w, w_in, w_s, b_s, lnw, lnb, lbraw, gnw, w_out, tri, lvl, w1, w2)
    return out.reshape(x.shape), w1_bf, w2_bf


def _ffn_kernel(x_ref, ada_ref, n2w_ref, w1_ref, w2_ref, fnw_ref, o_ref):
    ada = ada_ref[pl.ds(pl.program_id(0), 1), :]
    sh2, sc2, g2 = (ada[:, k * D_MODEL:(k + 1) * D_MODEL] for k in (3, 4, 5))

    def hidden(p):
        x = x_ref[0, pl.ds(p * FFN_PART, FFN_PART), :]
        ms = jnp.mean(x * x, axis=-1, keepdims=True)
        hmod = x * lax.rsqrt(ms + EPS) * n2w_ref[...]
        hmod = (hmod * (1.0 + sc2) + sh2).astype(_BF16)
        gu = _dot(hmod, w1_ref[...])
        return (_silu(gu[:, :D_FF]) * gu[:, D_FF:]).astype(_BF16)

    def finish(p, act):
        rows = pl.ds(p * FFN_PART, FFN_PART)
        x2 = x_ref[0, rows, :] + g2 * _dot(act, w2_ref[...])
        ms2 = jnp.mean(x2 * x2, axis=-1, keepdims=True)
        o_ref[0, rows, :] = x2 * lax.rsqrt(ms2 + EPS) * fnw_ref[...]

    act = hidden(0)
    for p in range(1, FFN_PARTS):
        act_next = hidden(p)
        finish(p - 1, act)
        act = act_next
    finish(FFN_PARTS - 1, act)


def _ffn_call(x, ada, n2w, w1, w2, fnw):
    bsz, seq, _ = x.shape
    tile = FFN_PART * FFN_PARTS
    const2 = lambda b, s: (0, 0)
    resident = dict(pipeline_mode=pl.Buffered(1))
    return pl.pallas_call(
        _ffn_kernel,
        grid=(bsz, seq // tile),
        in_specs=[
            pl.BlockSpec((1, tile, D_MODEL), lambda b, s: (b, s, 0)),
            pl.BlockSpec((bsz, N_ADA * D_MODEL), const2),
            pl.BlockSpec((1, D_MODEL), const2),
            pl.BlockSpec((D_MODEL, 2 * D_FF), const2, **resident),
            pl.BlockSpec((D_FF, D_MODEL), const2, **resident),
            pl.BlockSpec((1, D_MODEL), const2),
        ],
        out_specs=pl.BlockSpec((1, tile, D_MODEL), lambda b, s: (b, s, 0)),
        out_shape=jax.ShapeDtypeStruct(x.shape, _F32),
        compiler_params=pltpu.CompilerParams(
            dimension_semantics=("arbitrary", "arbitrary"),
            vmem_limit_bytes=VMEM_LIMIT),
        name="ffn",
    )(x, ada, n2w, w1, w2, fnw)


def kernel(x, c, w_ada, b_ada, norm1_w, w_in, w_s, b_s, v_ln_w, v_ln_b, lower_bounds,
           gn_w, w_out, norm2_w, w_ffn_in, w_ffn_out, final_norm_w):
    bsz = x.shape[0]
    depth = w_in.shape[0]
    assert depth == 1 and lower_bounds.shape[0] == 2

    tri = jnp.asarray(np.tril(np.ones((BLOCK, BLOCK), np.float32)), _BF16)

    for l in range(depth):
        ada, log2_bound = _ada_call(c, w_ada[l], b_ada[l][None, :], lower_bounds)
        mixer = functools.partial(
            _mixer_call, x, ada, norm1_w[l][None, :], w_in[l], w_s[l],
            b_s[l][:, :, None], v_ln_w[l][None, :], v_ln_b[l][None, :],
            lower_bounds, gn_w[l][None, :], w_out[l], tri, w_ffn_in[l], w_ffn_out[l])
        x, w1_bf, w2_bf = lax.cond(log2_bound[0, 0] <= GUARD_LOG2_RANGE,
                                   lambda: mixer(guarded=True), lambda: mixer(guarded=False))
        x = _ffn_call(x, ada, norm2_w[l][None, :], w1_bf, w2_bf, final_norm_w[None, :])
    return x
```

```python
import functools

import numpy as np
import jax
import jax.numpy as jnp
from jax import lax
from jax.experimental import pallas as pl
from jax.experimental.pallas import tpu as pltpu

D_MODEL = 1024
D_GMLP = 512
D_HGRN = 512
HEADS = 4
HEAD_DIM = 128
BLOCK = 128
GATE_CHUNK = 64
D_IN = 2 * D_GMLP + 4 * D_HGRN
D_FF = 2816
N_ADA = 6
EPS = 1e-6

LEVELS = (128, 64, 32, 16, 8, 4, 2)
GUARD_LEVELS = (128, 64)
GUARD_BLOCK = 32
GUARD_LOG2_RANGE = 100.0

MIX_TILE = 256
MXU_CHUNK = 256
FFN_PART = 256
FFN_PARTS = 4
VMEM_LIMIT = 56 * 1024 * 1024

SUBLANES = 8

_F32 = jnp.float32
_BF16 = jnp.bfloat16


def _dot(a, b):
    return jnp.dot(a, b, preferred_element_type=_F32)


def _silu(x):
    return x * jax.nn.sigmoid(x)


def _gelu(x):
    half = 0.5 * x
    return half + half * lax.erf(x * np.float32(1.0 / np.sqrt(2.0)))


def _level_table(levels, guard_block=None):
    t = np.arange(BLOCK)[:, None]
    s = np.arange(BLOCK)[None, :]
    table = np.full((BLOCK, BLOCK), -1, np.int32)
    if guard_block:
        table[((t // guard_block) == (s // guard_block)) & (s <= t)] = len(levels)
    for idx, n in enumerate(levels):
        h = n // 2
        same = (t // n) == (s // n)
        table[same & ((t % n) >= h) & ((s % n) < h)] = idx
    return table


ADA_ROWS = 256


def _ada_kernel(c_ref, w_ref, b_ref, lbraw_ref, o_ref, bound_ref):
    @pl.when(pl.program_id(0) == 0)
    def _():
        o_ref[...] = jnp.broadcast_to(b_ref[...], o_ref.shape)
        lbraw = lbraw_ref[...]
        lbe = jnp.exp(lbraw - jnp.max(lbraw, axis=0, keepdims=True))
        lb = (lbe / jnp.sum(lbe, axis=0, keepdims=True))[0:1]
        worst = jnp.max(-jnp.log2(lb), axis=1, keepdims=True) * (GUARD_BLOCK // 2)
        bound_ref[...] = jnp.broadcast_to(worst, bound_ref.shape)

    c_act = _silu(c_ref[...]).astype(_BF16)
    o_ref[...] += _dot(c_act, w_ref[...].astype(_BF16))


def _ada_call(c, w_ada, b_ada, lower_bounds):
    bsz = c.shape[0]
    n_out = w_ada.shape[1]
    const2 = lambda k: (0, 0)
    return pl.pallas_call(
        _ada_kernel,
        grid=(D_MODEL // ADA_ROWS,),
        in_specs=[
            pl.BlockSpec((bsz, ADA_ROWS), lambda k: (0, k)),
            pl.BlockSpec((ADA_ROWS, n_out), lambda k: (k, 0)),
            pl.BlockSpec((1, n_out), const2),
            pl.BlockSpec(lower_bounds.shape, const2),
        ],
        out_specs=[
            pl.BlockSpec((bsz, n_out), const2),
            pl.BlockSpec((SUBLANES, HEAD_DIM), const2),
        ],
        out_shape=[
            jax.ShapeDtypeStruct((bsz, n_out), _F32),
            jax.ShapeDtypeStruct((SUBLANES, HEAD_DIM), _F32),
        ],
        compiler_params=pltpu.CompilerParams(dimension_semantics=("arbitrary",)),
        name="ada",
    )(c, w_ada, b_ada, lower_bounds)


def _neg_abs(x):
    bits = pltpu.bitcast(x, jnp.uint32) | jnp.uint32(0x80000000)
    return pltpu.bitcast(bits, _F32)


def _half_select(level_n, q, k, upper):
    h = level_n // 2
    if h % SUBLANES != 0:
        return jnp.where(upper, q, k)
    pieces = []
    for r0 in range(0, BLOCK, level_n):
        pieces += [k[r0:r0 + h], q[r0 + h:r0 + level_n]]
    return jnp.concatenate(pieces, axis=0)


def _decay_exponent(level_n, b_sc, row, col0):
    h = level_n // 2

    def ref_rows(r, n_rows):
        return jnp.broadcast_to(b_sc[pl.ds(r, 1), pl.ds(col0, HEAD_DIM)], (n_rows, HEAD_DIM))

    if level_n < SUBLANES:
        lower_block = (row[:SUBLANES] % SUBLANES) < level_n
        pieces = [jnp.where(lower_block, ref_rows(r0 + h - 1, SUBLANES),
                            ref_rows(r0 + level_n + h - 1, SUBLANES))
                  for r0 in range(0, BLOCK, SUBLANES)]
    else:
        pieces = [ref_rows(r0 + h - 1, level_n) for r0 in range(0, BLOCK, level_n)]
    b_ref = pieces[0] if len(pieces) == 1 else jnp.concatenate(pieces, axis=0)
    b_h = b_sc[:, pl.ds(col0, HEAD_DIM)]
    return b_h - b_ref


def _decay_factor(level_n, b_sc, row, col0):
    return jnp.exp2(_neg_abs(_decay_exponent(level_n, b_sc, row, col0)))


def _modulated_input(x, ada_b, n1w_ref):
    sh1, sc1 = ada_b[:, 0:D_MODEL], ada_b[:, D_MODEL:2 * D_MODEL]
    ms = jnp.mean(x * x, axis=-1, keepdims=True)
    hmod = x * lax.rsqrt(ms + EPS) * n1w_ref[...]
    return (hmod * (1.0 + sc1) + sh1).astype(_BF16)


BF16_ROWS = 16


def _cast_rows(n_rows, n_steps):
    rows = -(-n_rows // n_steps)
    rows = -(-rows // BF16_ROWS) * BF16_ROWS
    while n_rows % rows:
        rows += BF16_ROWS
    return rows


class _Interleaver:
    def __init__(self, tasks, total_weight):
        self._tasks = list(tasks)
        self._total = total_weight
        self._weight = 0
        self._done = 0

    def __call__(self, weight=1):
        self._weight += weight
        due = min(len(self._tasks), self._weight * len(self._tasks) // self._total)
        while self._done < due:
            self._tasks[self._done]()
            self._done += 1

    def flush(self):
        assert self._weight == self._total, (self._weight, self._total)
        while self._done < len(self._tasks):
            self._tasks[self._done]()
            self._done += 1


PRELUDE_WEIGHT = 4
HEAD_LEVELS_WEIGHT = 3
WEIGHT_PER_BLOCK = 2 * PRELUDE_WEIGHT + HEADS * (HEAD_LEVELS_WEIGHT + 1)


def _mixer_kernel(x_ref, xn_ref, ada_ref, n1w_ref, w_in_f32, ws_ref, bs_ref, lnw_ref, lnb_ref,
                  lbraw_ref, gnw_ref, w_out_f32, tri_ref, lvl_ref, w1_f32, w2_f32,
                  o_ref, w1_bf, w2_bf,
                  state_sc, proj_a, proj_b, h_a, h_b, y_sc, b_sc, wsm_sc, w_in_ref, w_out_ref,
                  pre_bf, pre_f32, zl_sc, zt_sc,
                  *, steps_per_seq, n_steps, guarded):
    g = pl.program_id(0)
    batch = g // steps_per_seq
    batch_next = jnp.minimum(g + 1, n_steps - 1) // steps_per_seq
    ada_b = ada_ref[pl.ds(batch, 1), :]

    w1_bf[...] = w1_f32[...].astype(_BF16)
    w2_bf[...] = w2_f32[...].astype(_BF16)

    @pl.when(g == 0)
    def _():
        w_in_ref[...] = w_in_f32[...].astype(_BF16)
        w_out_ref[...] = w_out_f32[...].astype(_BF16)
        h0 = _modulated_input(x_ref[pl.ds(0, MIX_TILE), :], ada_b, n1w_ref)
        proj_a[...] = _dot(h0, w_in_ref[...])
        h_b[...] = _modulated_input(x_ref[pl.ds(MIX_TILE, MIX_TILE), :], ada_b, n1w_ref)
        col = lax.broadcasted_iota(jnp.int32, (BLOCK, BLOCK), 1)
        rowb = lax.broadcasted_iota(jnp.int32, (BLOCK, BLOCK), 0)
        gate_mask = (rowb // GATE_CHUNK) >= (col // GATE_CHUNK)
        for hd in range(HEADS):
            wsm_sc[hd] = jnp.where(gate_mask, ws_ref[hd], 0.0).astype(_BF16)

    @pl.when(g % steps_per_seq == 0)
    def _():
        state_sc[...] = jnp.zeros_like(state_sc)

    lbraw = lbraw_ref[...]
    lbe = jnp.exp(lbraw - jnp.max(lbraw, axis=0, keepdims=True))
    lb = (lbe / jnp.sum(lbe, axis=0, keepdims=True))[0:1]

    levels = GUARD_LEVELS if guarded else LEVELS
    n_products = len(levels) + (1 if guarded else 0)
    row = lax.broadcasted_iota(jnp.int32, (BLOCK, HEAD_DIM), 0)
    lvl = lvl_ref[...]
    upper_rows = [(row % n) >= (n // 2) for n in levels]

    o0 = 2 * D_GMLP

    def prelude_elementwise(proj_sc, prow0, par, background):
        rows = pl.ds(prow0, BLOCK)
        u = proj_sc[rows, pl.ds(0, D_GMLP)]
        v = proj_sc[rows, pl.ds(D_GMLP, D_GMLP)]
        gv = _gelu(v)
        mu = jnp.mean(gv, axis=-1, keepdims=True)
        cen = gv - mu
        var = jnp.mean(cen * cen, axis=-1, keepdims=True)
        vn = cen * lax.rsqrt(var + EPS) * lnw_ref[...] + lnb_ref[...]
        pre_bf[par, 0] = vn.astype(_BF16)
        pre_f32[par, 0] = _gelu(u)
        background(PRELUDE_WEIGHT)
        fl = proj_sc[rows, pl.ds(o0 + D_HGRN, D_HGRN)]
        f = lb + (1.0 - lb) * jax.nn.sigmoid(fl)
        logf = jnp.log2(f)
        p_hi = logf.astype(_BF16)
        pre_bf[par, 1] = p_hi
        pre_bf[par, 2] = (logf - p_hi.astype(_F32)).astype(_BF16)
        pre_f32[par, 1] = f
        background(PRELUDE_WEIGHT)

    def prelude_matmuls(yrow0, par, b_blk):
        b_blk[...] = _dot(tri_ref[...], jnp.concatenate([pre_bf[par, 1], pre_bf[par, 2]], axis=0))
        mixed = []
        for hd in range(HEADS):
            mixed.append(_dot(wsm_sc[hd], pre_bf[par, 0, :, pl.ds(hd * HEAD_DIM, HEAD_DIM)]) + bs_ref[hd])
        y_a = pre_f32[par, 0] * jnp.concatenate(mixed, axis=-1)
        y_sc[pl.ds(yrow0, BLOCK), pl.ds(0, D_GMLP)] = y_a.astype(_BF16)

    def level_slabs(n):
        h = n // 2
        if h % SUBLANES or n == 1:
            return list(range(0, BLOCK, SUBLANES))
        return [r for r0 in range(0, BLOCK, n) for r in range(r0 + h, r0 + n, SUBLANES)]

    def stage_head(proj_sc, prow0, par, b_blk, hd, slot):
        rows = pl.ds(prow0, BLOCK)
        c0 = hd * HEAD_DIM
        q_h = _silu(proj_sc[rows, pl.ds(o0 + c0, HEAD_DIM)])
        f_h = pre_f32[par, 1, :, pl.ds(c0, HEAD_DIM)]
        k_h = 1.0 - f_h
        for idx, n in enumerate(levels):
            if n == 2:
                z = jnp.where(upper_rows[idx], q_h * f_h, k_h)
            else:
                z = _half_select(n, q_h, k_h, upper_rows[idx]) * _decay_factor(n, b_blk, row, c0)
            slabs = level_slabs(n)
            lhs = jnp.concatenate([z[r:r + SUBLANES] for r in slabs], axis=0)
            zl_sc[slot, idx, pl.ds(0, len(slabs) * SUBLANES), :] = lhs.astype(_BF16)
            zt_sc[slot, idx] = z.T.astype(_BF16)
        if guarded:
            d = _decay_exponent(GUARD_BLOCK, b_blk, row, c0)
            zl_sc[slot, len(levels)] = (q_h * jnp.exp2(d)).astype(_BF16)
            zt_sc[slot, len(levels)] = (k_h * jnp.exp2(-d)).T.astype(_BF16)
        return q_h, k_h

    def head_levels(slot, background):
        attn = [jnp.zeros((SUBLANES, BLOCK), _F32)] * (BLOCK // SUBLANES)
        for idx in range(n_products):
            slabs = level_slabs(levels[idx]) if idx < len(levels) else level_slabs(1)
            res = _dot(zl_sc[slot, idx, pl.ds(0, len(slabs) * SUBLANES), :], zt_sc[slot, idx])
            for j, r in enumerate(slabs):
                valid = lvl[r:r + SUBLANES] == idx
                attn[r // SUBLANES] = jnp.where(valid, res[j * SUBLANES:(j + 1) * SUBLANES],
                                                attn[r // SUBLANES])
        background(HEAD_LEVELS_WEIGHT)
        return jnp.concatenate(attn, axis=0)

    def head_tail(proj_sc, prow0, yrow0, b_blk, hd, attn, q_h, k_h):
        rows = pl.ds(prow0, BLOCK)
        c0 = hd * HEAD_DIM
        v_h = proj_sc[rows, pl.ds(o0 + 2 * D_HGRN + c0, HEAD_DIM)]
        g_h = proj_sc[rows, pl.ds(o0 + 3 * D_HGRN + c0, HEAD_DIM)]
        b_h = b_blk[:, pl.ds(c0, HEAD_DIM)]

        s_t = state_sc[hd]
        q_hat = (q_h * jnp.exp2(b_h)).astype(_BF16)
        v_t = v_h.T.astype(_BF16)
        lhs = jnp.concatenate([attn.astype(_BF16), q_hat], axis=1)
        rhs = jnp.concatenate([v_h.astype(_BF16), s_t.T.astype(_BF16)], axis=0)
        o_h = _dot(lhs, rhs)
        if not guarded:
            o_h = o_h + jnp.sum(q_h * k_h, axis=-1, keepdims=True) * v_h

        b_last = b_blk[pl.ds(BLOCK - 1, 1), pl.ds(c0, HEAD_DIM)]
        k_hat = (k_h * jnp.exp2(b_last - b_h)).astype(_BF16)
        state_sc[hd] = s_t * jnp.exp2(b_last) + _dot(v_t, k_hat)

        oms = jnp.mean(o_h * o_h, axis=-1, keepdims=True)
        y_b = o_h * lax.rsqrt(oms + EPS) * gnw_ref[...] * _silu(g_h)
        y_sc[pl.ds(yrow0, BLOCK), pl.ds(D_GMLP + c0, HEAD_DIM)] = y_b.astype(_BF16)

    blocks_per_tile = MIX_TILE // BLOCK
    g1 = ada_b[:, 2 * D_MODEL:3 * D_MODEL]

    def projection_tasks(h_sc, dst):
        def chunk(c):
            cols = pl.ds(c * MXU_CHUNK, MXU_CHUNK)
            dst[:, cols] = _dot(h_sc[...], w_in_ref[:, cols])
        return [functools.partial(chunk, c) for c in range(D_IN // MXU_CHUNK)]

    def finish_tasks(t):
        trow = pl.ds(t * MIX_TILE, MIX_TILE)

        def chunk(c):
            cols = pl.ds(c * MXU_CHUNK, MXU_CHUNK)
            mix = _dot(y_sc[trow, :], w_out_ref[:, cols])
            o_ref[trow, cols] = x_ref[trow, cols] + g1[:, c * MXU_CHUNK:(c + 1) * MXU_CHUNK] * mix
        return [functools.partial(chunk, c) for c in range(D_MODEL // MXU_CHUNK)]

    def mix_tile(proj_sc, t, tasks):
        background = _Interleaver(tasks, blocks_per_tile * WEIGHT_PER_BLOCK)
        heads = [(i, hd) for i in range(blocks_per_tile) for hd in range(HEADS)]

        def blk(i):
            return i * BLOCK, t * MIX_TILE + i * BLOCK, i % 2, b_sc.at[t * blocks_per_tile + i]

        prow0, yrow0, par, b_blk = blk(0)
        prelude_elementwise(proj_sc, prow0, par, background)
        prelude_matmuls(yrow0, par, b_blk)
        staged = stage_head(proj_sc, prow0, par, b_blk, 0, 0)
        pending = None
        for j, (i, hd) in enumerate(heads):
            prow0, yrow0, par, b_blk = blk(i)
            if hd == 1 and i + 1 < blocks_per_tile:
                prelude_elementwise(proj_sc, blk(i + 1)[0], blk(i + 1)[2], background)
            if hd == 2 and i + 1 < blocks_per_tile:
                prelude_matmuls(blk(i + 1)[1], blk(i + 1)[2], blk(i + 1)[3])
            staged_next = None
            if j + 1 < len(heads):
                ni, nhd = heads[j + 1]
                staged_next = stage_head(proj_sc, blk(ni)[0], blk(ni)[2], blk(ni)[3], nhd, (j + 1) % 2)
            attn = head_levels(j % 2, background)
            if pending is not None:
                pending()
            pending = functools.partial(head_tail, proj_sc, prow0, yrow0, b_blk, hd, attn, *staged)
            background()
            staged = staged_next
        pending()
        background.flush()

    mix_tile(proj_a, 0, projection_tasks(h_b, proj_b))
    ada_next = ada_ref[pl.ds(batch_next, 1), :]
    h_a[...] = _modulated_input(xn_ref[pl.ds(0, MIX_TILE), :], ada_next, n1w_ref)
    mix_tile(proj_b, 1, finish_tasks(0) + projection_tasks(h_a, proj_a))
    h_b[...] = _modulated_input(xn_ref[pl.ds(MIX_TILE, MIX_TILE), :], ada_next, n1w_ref)
    for task in finish_tasks(1):
        task()


def _mixer_call(x, ada, n1w, w_in, w_s, b_s, lnw, lnb, lbraw, gnw, w_out, tri, w1, w2, guarded):
    bsz, seq, _ = x.shape
    step_rows = 2 * MIX_TILE
    steps_per_seq = seq // step_rows
    n_steps = bsz * steps_per_seq
    x2 = x.reshape(bsz * seq, D_MODEL)
    const2 = lambda g: (0, 0)
    const3 = lambda g: (0, 0, 0)
    resident = dict(pipeline_mode=pl.Buffered(1))
    w1_rows = _cast_rows(w1.shape[0], n_steps)
    w2_rows = _cast_rows(w2.shape[0], n_steps)
    w1_map = lambda g: (jnp.minimum(g, w1.shape[0] // w1_rows - 1), 0)
    w2_map = lambda g: (jnp.minimum(g, w2.shape[0] // w2_rows - 1), 0)
    if guarded:
        lvl = jnp.asarray(_level_table(GUARD_LEVELS, GUARD_BLOCK))
    else:
        lvl = jnp.asarray(_level_table(LEVELS))
    kernel = functools.partial(_mixer_kernel, steps_per_seq=steps_per_seq, n_steps=n_steps,
                               guarded=guarded)
    out, w1_bf, w2_bf = pl.pallas_call(
        kernel,
        grid=(n_steps,),
        in_specs=[
            pl.BlockSpec((step_rows, D_MODEL), lambda g: (g, 0)),
            pl.BlockSpec((step_rows, D_MODEL), lambda g: (jnp.minimum(g + 1, n_steps - 1), 0)),
            pl.BlockSpec((bsz, N_ADA * D_MODEL), const2),
            pl.BlockSpec((1, D_MODEL), const2),
            pl.BlockSpec((D_MODEL, D_IN), const2, **resident),
            pl.BlockSpec((HEADS, BLOCK, BLOCK), const3),
            pl.BlockSpec((HEADS, BLOCK, 1), const3),
            pl.BlockSpec((1, D_GMLP), const2),
            pl.BlockSpec((1, D_GMLP), const2),
            pl.BlockSpec((2, D_HGRN), const2),
            pl.BlockSpec((1, HEAD_DIM), const2),
            pl.BlockSpec((D_MODEL, D_MODEL), const2, **resident),
            pl.BlockSpec((BLOCK, 2 * BLOCK), const2),
            pl.BlockSpec((BLOCK, BLOCK), const2),
            pl.BlockSpec((w1_rows, w1.shape[1]), w1_map),
            pl.BlockSpec((w2_rows, w2.shape[1]), w2_map),
        ],
        out_specs=[
            pl.BlockSpec((step_rows, D_MODEL), lambda g: (g, 0)),
            pl.BlockSpec((w1_rows, w1.shape[1]), w1_map),
            pl.BlockSpec((w2_rows, w2.shape[1]), w2_map),
        ],
        out_shape=[
            jax.ShapeDtypeStruct(x2.shape, _F32),
            jax.ShapeDtypeStruct(w1.shape, _BF16),
            jax.ShapeDtypeStruct(w2.shape, _BF16),
        ],
        scratch_shapes=[
            pltpu.VMEM((HEADS, HEAD_DIM, HEAD_DIM), _F32),
            pltpu.VMEM((MIX_TILE, D_IN), _F32),
            pltpu.VMEM((MIX_TILE, D_IN), _F32),
            pltpu.VMEM((MIX_TILE, D_MODEL), _BF16),
            pltpu.VMEM((MIX_TILE, D_MODEL), _BF16),
            pltpu.VMEM((step_rows, D_MODEL), _BF16),
            pltpu.VMEM((step_rows // BLOCK, BLOCK, D_HGRN), _F32),
            pltpu.VMEM((HEADS, BLOCK, BLOCK), _BF16),
            pltpu.VMEM((D_MODEL, D_IN), _BF16),
            pltpu.VMEM((D_MODEL, D_MODEL), _BF16),
            pltpu.VMEM((2, 3, BLOCK, D_HGRN), _BF16),
            pltpu.VMEM((2, 2, BLOCK, D_HGRN), _F32),
            pltpu.VMEM((2, len(LEVELS), BLOCK, HEAD_DIM), _BF16),
            pltpu.VMEM((2, len(LEVELS), HEAD_DIM, BLOCK), _BF16),
        ],
        compiler_params=pltpu.CompilerParams(
            dimension_semantics=("arbitrary",),
            vmem_limit_bytes=VMEM_LIMIT),
        name="mixer",
    )(x2, x2, ada, n1w, w_in, w_s, b_s, lnw, lnb, lbraw, gnw, w_out, tri, lvl, w1, w2)
    return out.reshape(x.shape), w1_bf, w2_bf


def _ffn_kernel(x_ref, ada_ref, n2w_ref, w1_ref, w2_ref, fnw_ref, o_ref):
    ada = ada_ref[pl.ds(pl.program_id(0), 1), :]
    sh2, sc2, g2 = (ada[:, k * D_MODEL:(k + 1) * D_MODEL] for k in (3, 4, 5))

    def hidden(p):
        x = x_ref[0, pl.ds(p * FFN_PART, FFN_PART), :]
        ms = jnp.mean(x * x, axis=-1, keepdims=True)
        hmod = x * lax.rsqrt(ms + EPS) * n2w_ref[...]
        hmod = (hmod * (1.0 + sc2) + sh2).astype(_BF16)
        gu = _dot(hmod, w1_ref[...])
        return (_silu(gu[:, :D_FF]) * gu[:, D_FF:]).astype(_BF16)

    def finish(p, act):
        rows = pl.ds(p * FFN_PART, FFN_PART)
        x2 = x_ref[0, rows, :] + g2 * _dot(act, w2_ref[...])
        ms2 = jnp.mean(x2 * x2, axis=-1, keepdims=True)
        o_ref[0, rows, :] = x2 * lax.rsqrt(ms2 + EPS) * fnw_ref[...]

    act = hidden(0)
    for p in range(1, FFN_PARTS):
        act_next = hidden(p)
        finish(p - 1, act)
        act = act_next
    finish(FFN_PARTS - 1, act)


def _ffn_call(x, ada, n2w, w1, w2, fnw):
    bsz, seq, _ = x.shape
    tile = FFN_PART * FFN_PARTS
    const2 = lambda b, s: (0, 0)
    resident = dict(pipeline_mode=pl.Buffered(1))
    return pl.pallas_call(
        _ffn_kernel,
        grid=(bsz, seq // tile),
        in_specs=[
            pl.BlockSpec((1, tile, D_MODEL), lambda b, s: (b, s, 0)),
            pl.BlockSpec((bsz, N_ADA * D_MODEL), const2),
            pl.BlockSpec((1, D_MODEL), const2),
            pl.BlockSpec((D_MODEL, 2 * D_FF), const2, **resident),
            pl.BlockSpec((D_FF, D_MODEL), const2, **resident),
            pl.BlockSpec((1, D_MODEL), const2),
        ],
        out_specs=pl.BlockSpec((1, tile, D_MODEL), lambda b, s: (b, s, 0)),
        out_shape=jax.ShapeDtypeStruct(x.shape, _F32),
        compiler_params=pltpu.CompilerParams(
            dimension_semantics=("arbitrary", "arbitrary"),
            vmem_limit_bytes=VMEM_LIMIT),
        name="ffn",
    )(x, ada, n2w, w1, w2, fnw)


def kernel(x, c, w_ada, b_ada, norm1_w, w_in, w_s, b_s, v_ln_w, v_ln_b, lower_bounds,
           gn_w, w_out, norm2_w, w_ffn_in, w_ffn_out, final_norm_w):
    bsz = x.shape[0]
    depth = w_in.shape[0]
    assert depth == 1 and lower_bounds.shape[0] == 2

    tri = jnp.asarray(np.tile(np.tril(np.ones((BLOCK, BLOCK), np.float32)), (1, 2)), _BF16)

    for l in range(depth):
        ada, log2_bound = _ada_call(c, w_ada[l], b_ada[l][None, :], lower_bounds)
        mixer = functools.partial(
            _mixer_call, x, ada, norm1_w[l][None, :], w_in[l], w_s[l],
            b_s[l][:, :, None], v_ln_w[l][None, :], v_ln_b[l][None, :],
            lower_bounds, gn_w[l][None, :], w_out[l], tri, w_ffn_in[l], w_ffn_out[l])
        x, w1_bf, w2_bf = lax.cond(log2_bound[0, 0] <= GUARD_LOG2_RANGE,
                                   lambda: mixer(guarded=True), lambda: mixer(guarded=False))
        x = _ffn_call(x, ada, norm2_w[l][None, :], w1_bf, w2_bf, final_norm_w[None, :])
    return x
```

```python
import functools

import numpy as np
import jax
import jax.numpy as jnp
from jax import lax
from jax.experimental import pallas as pl
from jax.experimental.pallas import tpu as pltpu

D_MODEL = 1024
D_GMLP = 512
D_HGRN = 512
HEADS = 4
HEAD_DIM = 128
BLOCK = 128
GATE_CHUNK = 64
D_IN = 2 * D_GMLP + 4 * D_HGRN
D_FF = 2816
N_ADA = 6
EPS = 1e-6

LEVELS = (128, 64, 32, 16, 8, 4, 2)
GUARD_LEVELS = (128, 64)
GUARD_BLOCK = 32
GUARD_LOG2_RANGE = 100.0

MIX_TILE = 256
MXU_CHUNK = 256
FFN_PART = 256
FFN_PARTS = 4
VMEM_LIMIT = 56 * 1024 * 1024

SUBLANES = 8

_F32 = jnp.float32
_BF16 = jnp.bfloat16


def _dot(a, b):
    return jnp.dot(a, b, preferred_element_type=_F32)


def _silu(x):
    return x * jax.nn.sigmoid(x)


def _gelu(x):
    half = 0.5 * x
    return half + half * lax.erf(x * np.float32(1.0 / np.sqrt(2.0)))


def _level_table(levels, guard_block=None):
    t = np.arange(BLOCK)[:, None]
    s = np.arange(BLOCK)[None, :]
    table = np.full((BLOCK, BLOCK), -1, np.int32)
    if guard_block:
        table[((t // guard_block) == (s // guard_block)) & (s <= t)] = len(levels)
    for idx, n in enumerate(levels):
        h = n // 2
        same = (t // n) == (s // n)
        table[same & ((t % n) >= h) & ((s % n) < h)] = idx
    return table


ADA_ROWS = 256


def _ada_kernel(c_ref, w_ref, b_ref, lbraw_ref, o_ref, bound_ref):
    @pl.when(pl.program_id(0) == 0)
    def _():
        o_ref[...] = jnp.broadcast_to(b_ref[...], o_ref.shape)
        lbraw = lbraw_ref[...]
        lbe = jnp.exp(lbraw - jnp.max(lbraw, axis=0, keepdims=True))
        lb = (lbe / jnp.sum(lbe, axis=0, keepdims=True))[0:1]
        worst = jnp.max(-jnp.log2(lb), axis=1, keepdims=True) * (GUARD_BLOCK // 2)
        bound_ref[...] = jnp.broadcast_to(worst, bound_ref.shape)

    c_act = _silu(c_ref[...]).astype(_BF16)
    o_ref[...] += _dot(c_act, w_ref[...].astype(_BF16))


def _ada_call(c, w_ada, b_ada, lower_bounds):
    bsz = c.shape[0]
    n_out = w_ada.shape[1]
    const2 = lambda k: (0, 0)
    return pl.pallas_call(
        _ada_kernel,
        grid=(D_MODEL // ADA_ROWS,),
        in_specs=[
            pl.BlockSpec((bsz, ADA_ROWS), lambda k: (0, k)),
            pl.BlockSpec((ADA_ROWS, n_out), lambda k: (k, 0)),
            pl.BlockSpec((1, n_out), const2),
            pl.BlockSpec(lower_bounds.shape, const2),
        ],
        out_specs=[
            pl.BlockSpec((bsz, n_out), const2),
            pl.BlockSpec((SUBLANES, HEAD_DIM), const2),
        ],
        out_shape=[
            jax.ShapeDtypeStruct((bsz, n_out), _F32),
            jax.ShapeDtypeStruct((SUBLANES, HEAD_DIM), _F32),
        ],
        compiler_params=pltpu.CompilerParams(dimension_semantics=("arbitrary",)),
        name="ada",
    )(c, w_ada, b_ada, lower_bounds)


def _neg_abs(x):
    bits = pltpu.bitcast(x, jnp.uint32) | jnp.uint32(0x80000000)
    return pltpu.bitcast(bits, _F32)


def _half_select(level_n, q, k, upper):
    h = level_n // 2
    if h % SUBLANES != 0:
        return jnp.where(upper, q, k)
    pieces = []
    for r0 in range(0, BLOCK, level_n):
        pieces += [k[r0:r0 + h], q[r0 + h:r0 + level_n]]
    return jnp.concatenate(pieces, axis=0)


def _decay_exponent(level_n, b_sc, row, col0):
    h = level_n // 2

    def ref_rows(r, n_rows):
        return jnp.broadcast_to(b_sc[pl.ds(r, 1), pl.ds(col0, HEAD_DIM)], (n_rows, HEAD_DIM))

    if level_n < SUBLANES:
        lower_block = (row[:SUBLANES] % SUBLANES) < level_n
        pieces = [jnp.where(lower_block, ref_rows(r0 + h - 1, SUBLANES),
                            ref_rows(r0 + level_n + h - 1, SUBLANES))
                  for r0 in range(0, BLOCK, SUBLANES)]
    else:
        pieces = [ref_rows(r0 + h - 1, level_n) for r0 in range(0, BLOCK, level_n)]
    b_ref = pieces[0] if len(pieces) == 1 else jnp.concatenate(pieces, axis=0)
    b_h = b_sc[:, pl.ds(col0, HEAD_DIM)]
    return b_h - b_ref


def _decay_factor(level_n, b_sc, row, col0):
    return jnp.exp2(_neg_abs(_decay_exponent(level_n, b_sc, row, col0)))


def _modulated_input(x, ada_b, n1w_ref):
    sh1, sc1 = ada_b[:, 0:D_MODEL], ada_b[:, D_MODEL:2 * D_MODEL]
    ms = jnp.mean(x * x, axis=-1, keepdims=True)
    hmod = x * lax.rsqrt(ms + EPS) * n1w_ref[...]
    return (hmod * (1.0 + sc1) + sh1).astype(_BF16)


BF16_ROWS = 16


def _cast_rows(n_rows, n_steps):
    rows = -(-n_rows // n_steps)
    rows = -(-rows // BF16_ROWS) * BF16_ROWS
    while n_rows % rows:
        rows += BF16_ROWS
    return rows


class _Interleaver:
    def __init__(self, tasks, total_weight):
        self._tasks = list(tasks)
        self._total = total_weight
        self._weight = 0
        self._done = 0

    def __call__(self, weight=1):
        self._weight += weight
        due = min(len(self._tasks), self._weight * len(self._tasks) // self._total)
        while self._done < due:
            self._tasks[self._done]()
            self._done += 1

    def flush(self):
        assert self._weight == self._total, (self._weight, self._total)
        while self._done < len(self._tasks):
            self._tasks[self._done]()
            self._done += 1


PRELUDE_WEIGHT = 4
HEAD_LEVELS_WEIGHT = 3
WEIGHT_PER_BLOCK = 2 * PRELUDE_WEIGHT + HEADS * (HEAD_LEVELS_WEIGHT + 1)


def _mixer_kernel(x_ref, xn_ref, ada_ref, n1w_ref, w_in_f32, ws_ref, bs_ref, lnw_ref, lnb_ref,
                  lbraw_ref, gnw_ref, w_out_f32, tri_ref, lvl_ref, w1_f32, w2_f32,
                  o_ref, w1_bf, w2_bf,
                  state_sc, proj_a, proj_b, h_a, h_b, y_sc, b_sc, wsm_sc, w_in_ref, w_out_ref,
                  pre_bf, pre_f32, zl_sc, zt_sc,
                  *, steps_per_seq, n_steps, guarded):
    g = pl.program_id(0)
    batch = g // steps_per_seq
    batch_next = jnp.minimum(g + 1, n_steps - 1) // steps_per_seq
    ada_b = ada_ref[pl.ds(batch, 1), :]

    w1_bf[...] = w1_f32[...].astype(_BF16)
    w2_bf[...] = w2_f32[...].astype(_BF16)

    @pl.when(g == 0)
    def _():
        w_in_ref[...] = w_in_f32[...].astype(_BF16)
        w_out_ref[...] = w_out_f32[...].astype(_BF16)
        h0 = _modulated_input(x_ref[pl.ds(0, MIX_TILE), :], ada_b, n1w_ref)
        proj_a[...] = _dot(h0, w_in_ref[...])
        h_b[...] = _modulated_input(x_ref[pl.ds(MIX_TILE, MIX_TILE), :], ada_b, n1w_ref)
        col = lax.broadcasted_iota(jnp.int32, (BLOCK, BLOCK), 1)
        rowb = lax.broadcasted_iota(jnp.int32, (BLOCK, BLOCK), 0)
        gate_mask = (rowb // GATE_CHUNK) >= (col // GATE_CHUNK)
        for hd in range(HEADS):
            wsm_sc[hd] = jnp.where(gate_mask, ws_ref[hd], 0.0).astype(_BF16)

    @pl.when(g % steps_per_seq == 0)
    def _():
        state_sc[...] = jnp.zeros_like(state_sc)

    lbraw = lbraw_ref[...]
    lbe = jnp.exp(lbraw - jnp.max(lbraw, axis=0, keepdims=True))
    lb = (lbe / jnp.sum(lbe, axis=0, keepdims=True))[0:1]

    levels = GUARD_LEVELS if guarded else LEVELS
    n_products = len(levels) + (1 if guarded else 0)
    row = lax.broadcasted_iota(jnp.int32, (BLOCK, HEAD_DIM), 0)
    lvl = lvl_ref[...]
    upper_rows = [(row % n) >= (n // 2) for n in levels]

    o0 = 2 * D_GMLP

    def prelude_elementwise(proj_sc, prow0, par, background):
        rows = pl.ds(prow0, BLOCK)
        u = proj_sc[rows, pl.ds(0, D_GMLP)]
        v = proj_sc[rows, pl.ds(D_GMLP, D_GMLP)]
        gv = _gelu(v)
        mu = jnp.mean(gv, axis=-1, keepdims=True)
        cen = gv - mu
        var = jnp.mean(cen * cen, axis=-1, keepdims=True)
        vn = cen * lax.rsqrt(var + EPS) * lnw_ref[...] + lnb_ref[...]
        pre_bf[par, 0] = vn.astype(_BF16)
        pre_f32[par, 0] = _gelu(u)
        background(PRELUDE_WEIGHT)
        fl = proj_sc[rows, pl.ds(o0 + D_HGRN, D_HGRN)]
        f = lb + (1.0 - lb) * jax.nn.sigmoid(fl)
        logf = jnp.log2(f)
        p_hi = logf.astype(_BF16)
        pre_bf[par, 1] = p_hi
        pre_bf[par, 2] = (logf - p_hi.astype(_F32)).astype(_BF16)
        pre_f32[par, 1] = f
        background(PRELUDE_WEIGHT)

    def prelude_matmuls(yrow0, par, b_blk):
        b_blk[...] = _dot(tri_ref[...], jnp.concatenate([pre_bf[par, 1], pre_bf[par, 2]], axis=0))
        mixed = []
        for hd in range(HEADS):
            mixed.append(_dot(wsm_sc[hd], pre_bf[par, 0, :, pl.ds(hd * HEAD_DIM, HEAD_DIM)]) + bs_ref[hd])
        y_a = pre_f32[par, 0] * jnp.concatenate(mixed, axis=-1)
        y_sc[pl.ds(yrow0, BLOCK), pl.ds(0, D_GMLP)] = y_a.astype(_BF16)

    def level_slabs(n):
        h = n // 2
        if h % SUBLANES or n == 1:
            return list(range(0, BLOCK, SUBLANES))
        return [r for r0 in range(0, BLOCK, n) for r in range(r0 + h, r0 + n, SUBLANES)]

    def stage_head(proj_sc, prow0, par, b_blk, hd, slot):
        rows = pl.ds(prow0, BLOCK)
        c0 = hd * HEAD_DIM
        q_h = _silu(proj_sc[rows, pl.ds(o0 + c0, HEAD_DIM)])
        f_h = pre_f32[par, 1, :, pl.ds(c0, HEAD_DIM)]
        k_h = 1.0 - f_h
        for idx, n in enumerate(levels):
            if n == 2:
                z = jnp.where(upper_rows[idx], q_h * f_h, k_h)
            else:
                z = _half_select(n, q_h, k_h, upper_rows[idx]) * _decay_factor(n, b_blk, row, c0)
            slabs = level_slabs(n)
            lhs = jnp.concatenate([z[r:r + SUBLANES] for r in slabs], axis=0)
            zl_sc[slot, idx, pl.ds(0, len(slabs) * SUBLANES), :] = lhs.astype(_BF16)
            zt_sc[slot, idx] = z.T.astype(_BF16)
        if guarded:
            d = _decay_exponent(GUARD_BLOCK, b_blk, row, c0)
            zl_sc[slot, len(levels)] = (q_h * jnp.exp2(d)).astype(_BF16)
            zt_sc[slot, len(levels)] = (k_h * jnp.exp2(-d)).T.astype(_BF16)
        return q_h, k_h

    def head_levels(slot, background):
        attn = [jnp.zeros((SUBLANES, BLOCK), _F32)] * (BLOCK // SUBLANES)
        for idx in range(n_products):
            slabs = level_slabs(levels[idx]) if idx < len(levels) else level_slabs(1)
            res = _dot(zl_sc[slot, idx, pl.ds(0, len(slabs) * SUBLANES), :], zt_sc[slot, idx])
            for j, r in enumerate(slabs):
                valid = lvl[r:r + SUBLANES] == idx
                attn[r // SUBLANES] = jnp.where(valid, res[j * SUBLANES:(j + 1) * SUBLANES],
                                                attn[r // SUBLANES])
        background(HEAD_LEVELS_WEIGHT)
        return jnp.concatenate(attn, axis=0)

    def head_tail(proj_sc, prow0, yrow0, b_blk, hd, attn, q_h, k_h):
        rows = pl.ds(prow0, BLOCK)
        c0 = hd * HEAD_DIM
        v_h = proj_sc[rows, pl.ds(o0 + 2 * D_HGRN + c0, HEAD_DIM)]
        g_h = proj_sc[rows, pl.ds(o0 + 3 * D_HGRN + c0, HEAD_DIM)]
        b_h = b_blk[:, pl.ds(c0, HEAD_DIM)]

        s_t = state_sc[hd]
        q_hat = (q_h * jnp.exp2(b_h)).astype(_BF16)
        v_t = v_h.T.astype(_BF16)
        lhs = jnp.concatenate([attn.astype(_BF16), q_hat], axis=1)
        rhs = jnp.concatenate([v_h.astype(_BF16), s_t.T.astype(_BF16)], axis=0)
        o_h = _dot(lhs, rhs)
        if not guarded:
            o_h = o_h + jnp.sum(q_h * k_h, axis=-1, keepdims=True) * v_h

        b_last = b_blk[pl.ds(BLOCK - 1, 1), pl.ds(c0, HEAD_DIM)]
        k_hat = (k_h * jnp.exp2(b_last - b_h)).astype(_BF16)
        state_sc[hd] = s_t * jnp.exp2(b_last) + _dot(v_t, k_hat)

        oms = jnp.mean(o_h * o_h, axis=-1, keepdims=True)
        y_b = o_h * lax.rsqrt(oms + EPS) * gnw_ref[...] * _silu(g_h)
        y_sc[pl.ds(yrow0, BLOCK), pl.ds(D_GMLP + c0, HEAD_DIM)] = y_b.astype(_BF16)

    blocks_per_tile = MIX_TILE // BLOCK
    g1 = ada_b[:, 2 * D_MODEL:3 * D_MODEL]

    def projection_tasks(h_sc, dst):
        def chunk(c):
            cols = pl.ds(c * MXU_CHUNK, MXU_CHUNK)
            dst[:, cols] = _dot(h_sc[...], w_in_ref[:, cols])
        return [functools.partial(chunk, c) for c in range(D_IN // MXU_CHUNK)]

    def finish_tasks(t):
        trow = pl.ds(t * MIX_TILE, MIX_TILE)

        def chunk(c):
            cols = pl.ds(c * MXU_CHUNK, MXU_CHUNK)
            mix = _dot(y_sc[trow, :], w_out_ref[:, cols])
            o_ref[trow, cols] = x_ref[trow, cols] + g1[:, c * MXU_CHUNK:(c + 1) * MXU_CHUNK] * mix
        return [functools.partial(chunk, c) for c in range(D_MODEL // MXU_CHUNK)]

    def mix_tile(proj_sc, t, tasks):
        background = _Interleaver(tasks, blocks_per_tile * WEIGHT_PER_BLOCK)
        heads = [(i, hd) for i in range(blocks_per_tile) for hd in range(HEADS)]

        def blk(i):
            return i * BLOCK, t * MIX_TILE + i * BLOCK, i % 2, b_sc.at[t * blocks_per_tile + i]

        prow0, yrow0, par, b_blk = blk(0)
        prelude_elementwise(proj_sc, prow0, par, background)
        prelude_matmuls(yrow0, par, b_blk)
        staged = stage_head(proj_sc, prow0, par, b_blk, 0, 0)
        pending = None
        for j, (i, hd) in enumerate(heads):
            prow0, yrow0, par, b_blk = blk(i)
            if hd == 1 and i + 1 < blocks_per_tile:
                prelude_elementwise(proj_sc, blk(i + 1)[0], blk(i + 1)[2], background)
            if hd == 2 and i + 1 < blocks_per_tile:
                prelude_matmuls(blk(i + 1)[1], blk(i + 1)[2], blk(i + 1)[3])
            staged_next = None
            if j + 1 < len(heads):
                ni, nhd = heads[j + 1]
                staged_next = stage_head(proj_sc, blk(ni)[0], blk(ni)[2], blk(ni)[3], nhd, (j + 1) % 2)
            attn = head_levels(j % 2, background)
            if pending is not None:
                pending()
            pending = functools.partial(head_tail, proj_sc, prow0, yrow0, b_blk, hd, attn, *staged)
            background()
            staged = staged_next
        pending()
        background.flush()

    mix_tile(proj_a, 0, projection_tasks(h_b, proj_b))
    ada_next = ada_ref[pl.ds(batch_next, 1), :]
    h_a[...] = _modulated_input(xn_ref[pl.ds(0, MIX_TILE), :], ada_next, n1w_ref)
    mix_tile(proj_b, 1, finish_tasks(0) + projection_tasks(h_a, proj_a))
    h_b[...] = _modulated_input(xn_ref[pl.ds(MIX_TILE, MIX_TILE), :], ada_next, n1w_ref)
    for task in finish_tasks(1):
        task()


def _mixer_call(x, ada, n1w, w_in, w_s, b_s, lnw, lnb, lbraw, gnw, w_out, tri, w1, w2, guarded):
    bsz, seq, _ = x.shape
    step_rows = 2 * MIX_TILE
    steps_per_seq = seq // step_rows
    n_steps = bsz * steps_per_seq
    x2 = x.reshape(bsz * seq, D_MODEL)
    const2 = lambda g: (0, 0)
    const3 = lambda g: (0, 0, 0)
    resident = dict(pipeline_mode=pl.Buffered(1))
    w1_rows = _cast_rows(w1.shape[0], n_steps)
    w2_rows = _cast_rows(w2.shape[0], n_steps)
    w1_map = lambda g: (jnp.minimum(g, w1.shape[0] // w1_rows - 1), 0)
    w2_map = lambda g: (jnp.minimum(g, w2.shape[0] // w2_rows - 1), 0)
    if guarded:
        lvl = jnp.asarray(_level_table(GUARD_LEVELS, GUARD_BLOCK))
    else:
        lvl = jnp.asarray(_level_table(LEVELS))
    kernel = functools.partial(_mixer_kernel, steps_per_seq=steps_per_seq, n_steps=n_steps,
                               guarded=guarded)
    out, w1_bf, w2_bf = pl.pallas_call(
        kernel,
        grid=(n_steps,),
        in_specs=[
            pl.BlockSpec((step_rows, D_MODEL), lambda g: (g, 0)),
            pl.BlockSpec((step_rows, D_MODEL), lambda g: (jnp.minimum(g + 1, n_steps - 1), 0)),
            pl.BlockSpec((bsz, N_ADA * D_MODEL), const2),
            pl.BlockSpec((1, D_MODEL), const2),
            pl.BlockSpec((D_MODEL, D_IN), const2, **resident),
            pl.BlockSpec((HEADS, BLOCK, BLOCK), const3),
            pl.BlockSpec((HEADS, BLOCK, 1), const3),
            pl.BlockSpec((1, D_GMLP), const2),
            pl.BlockSpec((1, D_GMLP), const2),
            pl.BlockSpec((2, D_HGRN), const2),
            pl.BlockSpec((1, HEAD_DIM), const2),
            pl.BlockSpec((D_MODEL, D_MODEL), const2, **resident),
            pl.BlockSpec((BLOCK, 2 * BLOCK), const2),
            pl.BlockSpec((BLOCK, BLOCK), const2),
            pl.BlockSpec((w1_rows, w1.shape[1]), w1_map),
            pl.BlockSpec((w2_rows, w2.shape[1]), w2_map),
        ],
        out_specs=[
            pl.BlockSpec((step_rows, D_MODEL), lambda g: (g, 0)),
            pl.BlockSpec((w1_rows, w1.shape[1]), w1_map),
            pl.BlockSpec((w2_rows, w2.shape[1]), w2_map),
        ],
        out_shape=[
            jax.ShapeDtypeStruct(x2.shape, _F32),
            jax.ShapeDtypeStruct(w1.shape, _BF16),
            jax.ShapeDtypeStruct(w2.shape, _BF16),
        ],
        scratch_shapes=[
            pltpu.VMEM((HEADS, HEAD_DIM, HEAD_DIM), _F32),
            pltpu.VMEM((MIX_TILE, D_IN), _F32),
            pltpu.VMEM((MIX_TILE, D_IN), _F32),
            pltpu.VMEM((MIX_TILE, D_MODEL), _BF16),
            pltpu.VMEM((MIX_TILE, D_MODEL), _BF16),
            pltpu.VMEM((step_rows, D_MODEL), _BF16),
            pltpu.VMEM((step_rows // BLOCK, BLOCK, D_HGRN), _F32),
            pltpu.VMEM((HEADS, BLOCK, BLOCK), _BF16),
            pltpu.VMEM((D_MODEL, D_IN), _BF16),
            pltpu.VMEM((D_MODEL, D_MODEL), _BF16),
            pltpu.VMEM((2, 3, BLOCK, D_HGRN), _BF16),
            pltpu.VMEM((2, 2, BLOCK, D_HGRN), _F32),
            pltpu.VMEM((2, len(LEVELS), BLOCK, HEAD_DIM), _BF16),
            pltpu.VMEM((2, len(LEVELS), HEAD_DIM, BLOCK), _BF16),
        ],
        compiler_params=pltpu.CompilerParams(
            dimension_semantics=("arbitrary",),
            vmem_limit_bytes=VMEM_LIMIT),
        name="mixer",
    )(x2, x2, ada, n1w, w_in, w_s, b_s, lnw, lnb, lbraw, gnw, w_out, tri, lvl, w1, w2)
    return out.reshape(x.shape), w1_bf, w2_bf


def _ffn_kernel(x_ref, ada_ref, n2w_ref, w1_ref, w2_ref, fnw_ref, o_ref):
    ada = ada_ref[pl.ds(pl.program_id(0), 1), :]
    sh2, sc2, g2 = (ada[:, k * D_MODEL:(k + 1) * D_MODEL] for k in (3, 4, 5))

    def hidden(p):
        x = x_ref[0, pl.ds(p * FFN_PART, FFN_PART), :]
        ms = jnp.mean(x * x, axis=-1, keepdims=True)
        hmod = x * lax.rsqrt(ms + EPS) * n2w_ref[...]
        hmod = (hmod * (1.0 + sc2) + sh2).astype(_BF16)
        acts = []
        for c in range(D_FF // MXU_CHUNK):
            gate = _dot(hmod, w1_ref[:, pl.ds(c * MXU_CHUNK, MXU_CHUNK)])
            up = _dot(hmod, w1_ref[:, pl.ds(D_FF + c * MXU_CHUNK, MXU_CHUNK)])
            acts.append((_silu(gate) * up).astype(_BF16))
        return jnp.concatenate(acts, axis=1)

    def finish(p, act):
        rows = pl.ds(p * FFN_PART, FFN_PART)
        x2 = x_ref[0, rows, :] + g2 * _dot(act, w2_ref[...])
        ms2 = jnp.mean(x2 * x2, axis=-1, keepdims=True)
        o_ref[0, rows, :] = x2 * lax.rsqrt(ms2 + EPS) * fnw_ref[...]

    act = hidden(0)
    for p in range(1, FFN_PARTS):
        act_next = hidden(p)
        finish(p - 1, act)
        act = act_next
    finish(FFN_PARTS - 1, act)


def _ffn_call(x, ada, n2w, w1, w2, fnw):
    bsz, seq, _ = x.shape
    tile = FFN_PART * FFN_PARTS
    const2 = lambda b, s: (0, 0)
    resident = dict(pipeline_mode=pl.Buffered(1))
    return pl.pallas_call(
        _ffn_kernel,
        grid=(bsz, seq // tile),
        in_specs=[
            pl.BlockSpec((1, tile, D_MODEL), lambda b, s: (b, s, 0)),
            pl.BlockSpec((bsz, N_ADA * D_MODEL), const2),
            pl.BlockSpec((1, D_MODEL), const2),
            pl.BlockSpec((D_MODEL, 2 * D_FF), const2, **resident),
            pl.BlockSpec((D_FF, D_MODEL), const2, **resident),
            pl.BlockSpec((1, D_MODEL), const2),
        ],
        out_specs=pl.BlockSpec((1, tile, D_MODEL), lambda b, s: (b, s, 0)),
        out_shape=jax.ShapeDtypeStruct(x.shape, _F32),
        compiler_params=pltpu.CompilerParams(
            dimension_semantics=("arbitrary", "arbitrary"),
            vmem_limit_bytes=VMEM_LIMIT),
        name="ffn",
    )(x, ada, n2w, w1, w2, fnw)


def kernel(x, c, w_ada, b_ada, norm1_w, w_in, w_s, b_s, v_ln_w, v_ln_b, lower_bounds,
           gn_w, w_out, norm2_w, w_ffn_in, w_ffn_out, final_norm_w):
    bsz = x.shape[0]
    depth = w_in.shape[0]
    assert depth == 1 and lower_bounds.shape[0] == 2

    tri = jnp.asarray(np.tile(np.tril(np.ones((BLOCK, BLOCK), np.float32)), (1, 2)), _BF16)

    for l in range(depth):
        ada, log2_bound = _ada_call(c, w_ada[l], b_ada[l][None, :], lower_bounds)
        mixer = functools.partial(
            _mixer_call, x, ada, norm1_w[l][None, :], w_in[l], w_s[l],
            b_s[l][:, :, None], v_ln_w[l][None, :], v_ln_b[l][None, :],
            lower_bounds, gn_w[l][None, :], w_out[l], tri, w_ffn_in[l], w_ffn_out[l])
        x, w1_bf, w2_bf = lax.cond(log2_bound[0, 0] <= GUARD_LOG2_RANGE,
                                   lambda: mixer(guarded=True), lambda: mixer(guarded=False))
        x = _ffn_call(x, ada, norm2_w[l][None, :], w1_bf, w2_bf, final_norm_w[None, :])
    return x
```

```python
import functools

import numpy as np
import jax
import jax.numpy as jnp
from jax import lax
from jax.experimental import pallas as pl
from jax.experimental.pallas import tpu as pltpu

D_MODEL = 1024
D_GMLP = 512
D_HGRN = 512
HEADS = 4
HEAD_DIM = 128
BLOCK = 128
GATE_CHUNK = 64
D_IN = 2 * D_GMLP + 4 * D_HGRN
D_FF = 2816
N_ADA = 6
EPS = 1e-6

LEVELS = (128, 64, 32, 16, 8, 4, 2)
GUARD_LEVELS = (128, 64)
GUARD_BLOCK = 32
GUARD_LOG2_RANGE = 100.0

MIX_TILE = 256
MXU_CHUNK = 256
FFN_PART = 256
FFN_PARTS = 4
VMEM_LIMIT = 56 * 1024 * 1024

SUBLANES = 8

_F32 = jnp.float32
_BF16 = jnp.bfloat16


def _dot(a, b):
    return jnp.dot(a, b, preferred_element_type=_F32)


def _silu(x):
    return x * jax.nn.sigmoid(x)


def _gelu(x):
    half = 0.5 * x
    return half + half * lax.erf(x * np.float32(1.0 / np.sqrt(2.0)))


def _level_table(levels, guard_block=None):
    t = np.arange(BLOCK)[:, None]
    s = np.arange(BLOCK)[None, :]
    table = np.full((BLOCK, BLOCK), -1, np.int32)
    if guard_block:
        table[((t // guard_block) == (s // guard_block)) & (s <= t)] = len(levels)
    for idx, n in enumerate(levels):
        h = n // 2
        same = (t // n) == (s // n)
        table[same & ((t % n) >= h) & ((s % n) < h)] = idx
    return table


ADA_ROWS = 256


def _ada_kernel(c_ref, w_ref, b_ref, lbraw_ref, o_ref, bound_ref):
    @pl.when(pl.program_id(0) == 0)
    def _():
        o_ref[...] = jnp.broadcast_to(b_ref[...], o_ref.shape)
        lbraw = lbraw_ref[...]
        lbe = jnp.exp(lbraw - jnp.max(lbraw, axis=0, keepdims=True))
        lb = (lbe / jnp.sum(lbe, axis=0, keepdims=True))[0:1]
        worst = jnp.max(-jnp.log2(lb), axis=1, keepdims=True) * (GUARD_BLOCK // 2)
        bound_ref[...] = jnp.broadcast_to(worst, bound_ref.shape)

    c_act = _silu(c_ref[...]).astype(_BF16)
    o_ref[...] += _dot(c_act, w_ref[...].astype(_BF16))


def _ada_call(c, w_ada, b_ada, lower_bounds):
    bsz = c.shape[0]
    n_out = w_ada.shape[1]
    const2 = lambda k: (0, 0)
    return pl.pallas_call(
        _ada_kernel,
        grid=(D_MODEL // ADA_ROWS,),
        in_specs=[
            pl.BlockSpec((bsz, ADA_ROWS), lambda k: (0, k)),
            pl.BlockSpec((ADA_ROWS, n_out), lambda k: (k, 0)),
            pl.BlockSpec((1, n_out), const2),
            pl.BlockSpec(lower_bounds.shape, const2),
        ],
        out_specs=[
            pl.BlockSpec((bsz, n_out), const2),
            pl.BlockSpec((SUBLANES, HEAD_DIM), const2),
        ],
        out_shape=[
            jax.ShapeDtypeStruct((bsz, n_out), _F32),
            jax.ShapeDtypeStruct((SUBLANES, HEAD_DIM), _F32),
        ],
        compiler_params=pltpu.CompilerParams(dimension_semantics=("arbitrary",)),
        name="ada",
    )(c, w_ada, b_ada, lower_bounds)


def _neg_abs(x):
    bits = pltpu.bitcast(x, jnp.uint32) | jnp.uint32(0x80000000)
    return pltpu.bitcast(bits, _F32)


def _half_select(level_n, q, k, upper):
    h = level_n // 2
    if h % SUBLANES != 0:
        return jnp.where(upper, q, k)
    pieces = []
    for r0 in range(0, BLOCK, level_n):
        pieces += [k[r0:r0 + h], q[r0 + h:r0 + level_n]]
    return jnp.concatenate(pieces, axis=0)


def _decay_exponent(level_n, b_sc, row, col0):
    h = level_n // 2

    def ref_rows(r, n_rows):
        return jnp.broadcast_to(b_sc[pl.ds(r, 1), pl.ds(col0, HEAD_DIM)], (n_rows, HEAD_DIM))

    if level_n < SUBLANES:
        lower_block = (row[:SUBLANES] % SUBLANES) < level_n
        pieces = [jnp.where(lower_block, ref_rows(r0 + h - 1, SUBLANES),
                            ref_rows(r0 + level_n + h - 1, SUBLANES))
                  for r0 in range(0, BLOCK, SUBLANES)]
    else:
        pieces = [ref_rows(r0 + h - 1, level_n) for r0 in range(0, BLOCK, level_n)]
    b_ref = pieces[0] if len(pieces) == 1 else jnp.concatenate(pieces, axis=0)
    b_h = b_sc[:, pl.ds(col0, HEAD_DIM)]
    return b_h - b_ref


def _decay_factor(level_n, b_sc, row, col0):
    return jnp.exp2(_neg_abs(_decay_exponent(level_n, b_sc, row, col0)))


def _modulated_input(x, ada_b, n1w_ref):
    sh1, sc1 = ada_b[:, 0:D_MODEL], ada_b[:, D_MODEL:2 * D_MODEL]
    ms = jnp.mean(x * x, axis=-1, keepdims=True)
    hmod = x * lax.rsqrt(ms + EPS) * n1w_ref[...]
    return (hmod * (1.0 + sc1) + sh1).astype(_BF16)


BF16_ROWS = 16


def _cast_rows(n_rows, n_steps):
    rows = -(-n_rows // n_steps)
    rows = -(-rows // BF16_ROWS) * BF16_ROWS
    while n_rows % rows:
        rows += BF16_ROWS
    return rows


class _Interleaver:
    def __init__(self, tasks, total_weight):
        self._tasks = list(tasks)
        self._total = total_weight
        self._weight = 0
        self._done = 0

    def __call__(self, weight=1):
        self._weight += weight
        due = min(len(self._tasks), self._weight * len(self._tasks) // self._total)
        while self._done < due:
            self._tasks[self._done]()
            self._done += 1

    def flush(self):
        assert self._weight == self._total, (self._weight, self._total)
        while self._done < len(self._tasks):
            self._tasks[self._done]()
            self._done += 1


PRELUDE_WEIGHT = 2
HEAD_LEVELS_WEIGHT = 3
WEIGHT_PER_BLOCK = 2 * PRELUDE_WEIGHT + HEADS * (HEAD_LEVELS_WEIGHT + 1)


def _mixer_kernel(x_ref, xn_ref, ada_ref, n1w_ref, w_in_f32, ws_ref, bs_ref, lnw_ref, lnb_ref,
                  lbraw_ref, gnw_ref, w_out_f32, tri_ref, lvl_ref, w1_f32, w2_f32,
                  o_ref, w1_bf, w2_bf,
                  state_sc, proj_a, proj_b, h_a, h_b, y_sc, b_sc, wsm_sc, w_in_ref, w_out_ref,
                  pre_bf, pre_f32, zl_sc, zt_sc,
                  *, steps_per_seq, n_steps, guarded):
    g = pl.program_id(0)
    batch = g // steps_per_seq
    batch_next = jnp.minimum(g + 1, n_steps - 1) // steps_per_seq
    ada_b = ada_ref[pl.ds(batch, 1), :]

    w1_bf[...] = w1_f32[...].astype(_BF16)
    w2_bf[...] = w2_f32[...].astype(_BF16)

    @pl.when(g == 0)
    def _():
        w_in_ref[...] = w_in_f32[...].astype(_BF16)
        w_out_ref[...] = w_out_f32[...].astype(_BF16)
        h0 = _modulated_input(x_ref[pl.ds(0, MIX_TILE), :], ada_b, n1w_ref)
        proj_a[...] = _dot(h0, w_in_ref[...])
        h_b[...] = _modulated_input(x_ref[pl.ds(MIX_TILE, MIX_TILE), :], ada_b, n1w_ref)
        col = lax.broadcasted_iota(jnp.int32, (BLOCK, BLOCK), 1)
        rowb = lax.broadcasted_iota(jnp.int32, (BLOCK, BLOCK), 0)
        gate_mask = (rowb // GATE_CHUNK) >= (col // GATE_CHUNK)
        for hd in range(HEADS):
            wsm_sc[hd] = jnp.where(gate_mask, ws_ref[hd], 0.0).astype(_BF16)

    @pl.when(g % steps_per_seq == 0)
    def _():
        state_sc[...] = jnp.zeros_like(state_sc)

    lbraw = lbraw_ref[...]
    lbe = jnp.exp(lbraw - jnp.max(lbraw, axis=0, keepdims=True))
    lb = (lbe / jnp.sum(lbe, axis=0, keepdims=True))[0:1]

    levels = GUARD_LEVELS if guarded else LEVELS
    n_products = len(levels) + (1 if guarded else 0)
    row = lax.broadcasted_iota(jnp.int32, (BLOCK, HEAD_DIM), 0)
    lvl = lvl_ref[...]
    upper_rows = [(row % n) >= (n // 2) for n in levels]

    o0 = 2 * D_GMLP

    def prelude_elementwise(proj_sc, prow0, par, background):
        rows = pl.ds(prow0, BLOCK)
        u = proj_sc[rows, pl.ds(0, D_GMLP)]
        v = proj_sc[rows, pl.ds(D_GMLP, D_GMLP)]
        gv = _gelu(v)
        mu = jnp.mean(gv, axis=-1, keepdims=True)
        cen = gv - mu
        var = jnp.mean(cen * cen, axis=-1, keepdims=True)
        vn = cen * lax.rsqrt(var + EPS) * lnw_ref[...] + lnb_ref[...]
        pre_bf[par, 0] = vn.astype(_BF16)
        pre_f32[par, 0] = _gelu(u)
        background(PRELUDE_WEIGHT)
        fl = proj_sc[rows, pl.ds(o0 + D_HGRN, D_HGRN)]
        f = lb + (1.0 - lb) * jax.nn.sigmoid(fl)
        logf = jnp.log2(f)
        p_hi = logf.astype(_BF16)
        pre_bf[par, 1] = p_hi
        pre_bf[par, 2] = (logf - p_hi.astype(_F32)).astype(_BF16)
        pre_f32[par, 1] = f
        background(PRELUDE_WEIGHT)

    def prelude_matmuls(yrow0, par, b_blk):
        b_blk[...] = _dot(tri_ref[...], jnp.concatenate([pre_bf[par, 1], pre_bf[par, 2]], axis=0))
        mixed = []
        for hd in range(HEADS):
            mixed.append(_dot(wsm_sc[hd], pre_bf[par, 0, :, pl.ds(hd * HEAD_DIM, HEAD_DIM)]) + bs_ref[hd])
        y_a = pre_f32[par, 0] * jnp.concatenate(mixed, axis=-1)
        y_sc[pl.ds(yrow0, BLOCK), pl.ds(0, D_GMLP)] = y_a.astype(_BF16)

    def level_slabs(n):
        h = n // 2
        if h % SUBLANES or n == 1:
            return list(range(0, BLOCK, SUBLANES))
        return [r for r0 in range(0, BLOCK, n) for r in range(r0 + h, r0 + n, SUBLANES)]

    def stage_head(proj_sc, prow0, par, b_blk, hd, slot):
        rows = pl.ds(prow0, BLOCK)
        c0 = hd * HEAD_DIM
        q_h = _silu(proj_sc[rows, pl.ds(o0 + c0, HEAD_DIM)])
        f_h = pre_f32[par, 1, :, pl.ds(c0, HEAD_DIM)]
        k_h = 1.0 - f_h
        for idx, n in enumerate(levels):
            if n == 2:
                z = jnp.where(upper_rows[idx], q_h * f_h, k_h)
            else:
                z = _half_select(n, q_h, k_h, upper_rows[idx]) * _decay_factor(n, b_blk, row, c0)
            slabs = level_slabs(n)
            lhs = jnp.concatenate([z[r:r + SUBLANES] for r in slabs], axis=0)
            zl_sc[slot, idx, pl.ds(0, len(slabs) * SUBLANES), :] = lhs.astype(_BF16)
            zt_sc[slot, idx] = z.T.astype(_BF16)
        if guarded:
            d = _decay_exponent(GUARD_BLOCK, b_blk, row, c0)
            zl_sc[slot, len(levels)] = (q_h * jnp.exp2(d)).astype(_BF16)
            zt_sc[slot, len(levels)] = (k_h * jnp.exp2(-d)).T.astype(_BF16)
        return q_h, k_h

    def head_levels(slot, background):
        attn = [jnp.zeros((SUBLANES, BLOCK), _F32)] * (BLOCK // SUBLANES)
        for idx in range(n_products):
            slabs = level_slabs(levels[idx]) if idx < len(levels) else level_slabs(1)
            res = _dot(zl_sc[slot, idx, pl.ds(0, len(slabs) * SUBLANES), :], zt_sc[slot, idx])
            for j, r in enumerate(slabs):
                valid = lvl[r:r + SUBLANES] == idx
                attn[r // SUBLANES] = jnp.where(valid, res[j * SUBLANES:(j + 1) * SUBLANES],
                                                attn[r // SUBLANES])
        background(HEAD_LEVELS_WEIGHT)
        return jnp.concatenate(attn, axis=0)

    def head_tail(proj_sc, prow0, yrow0, b_blk, hd, attn, q_h, k_h):
        rows = pl.ds(prow0, BLOCK)
        c0 = hd * HEAD_DIM
        v_h = proj_sc[rows, pl.ds(o0 + 2 * D_HGRN + c0, HEAD_DIM)]
        g_h = proj_sc[rows, pl.ds(o0 + 3 * D_HGRN + c0, HEAD_DIM)]
        b_h = b_blk[:, pl.ds(c0, HEAD_DIM)]

        s_t = state_sc[hd]
        q_hat = (q_h * jnp.exp2(b_h)).astype(_BF16)
        v_t = v_h.T.astype(_BF16)
        lhs = jnp.concatenate([attn.astype(_BF16), q_hat], axis=1)
        rhs = jnp.concatenate([v_h.astype(_BF16), s_t.T.astype(_BF16)], axis=0)
        o_h = _dot(lhs, rhs)
        if not guarded:
            o_h = o_h + jnp.sum(q_h * k_h, axis=-1, keepdims=True) * v_h

        b_last = b_blk[pl.ds(BLOCK - 1, 1), pl.ds(c0, HEAD_DIM)]
        k_hat = (k_h * jnp.exp2(b_last - b_h)).astype(_BF16)
        state_sc[hd] = s_t * jnp.exp2(b_last) + _dot(v_t, k_hat)

        oms = jnp.mean(o_h * o_h, axis=-1, keepdims=True)
        y_b = o_h * lax.rsqrt(oms + EPS) * gnw_ref[...] * _silu(g_h)
        y_sc[pl.ds(yrow0, BLOCK), pl.ds(D_GMLP + c0, HEAD_DIM)] = y_b.astype(_BF16)

    blocks_per_tile = MIX_TILE // BLOCK
    g1 = ada_b[:, 2 * D_MODEL:3 * D_MODEL]

    def projection_tasks(h_sc, dst):
        def chunk(c):
            cols = pl.ds(c * MXU_CHUNK, MXU_CHUNK)
            dst[:, cols] = _dot(h_sc[...], w_in_ref[:, cols])
        return [functools.partial(chunk, c) for c in range(D_IN // MXU_CHUNK)]

    def finish_tasks(t):
        trow = pl.ds(t * MIX_TILE, MIX_TILE)

        def chunk(c):
            cols = pl.ds(c * MXU_CHUNK, MXU_CHUNK)
            mix = _dot(y_sc[trow, :], w_out_ref[:, cols])
            o_ref[trow, cols] = x_ref[trow, cols] + g1[:, c * MXU_CHUNK:(c + 1) * MXU_CHUNK] * mix
        return [functools.partial(chunk, c) for c in range(D_MODEL // MXU_CHUNK)]

    def mix_tile(proj_sc, t, tasks):
        background = _Interleaver(tasks, blocks_per_tile * WEIGHT_PER_BLOCK)
        heads = [(i, hd) for i in range(blocks_per_tile) for hd in range(HEADS)]

        def blk(i):
            return i * BLOCK, t * MIX_TILE + i * BLOCK, i % 2, b_sc.at[t * blocks_per_tile + i]

        prow0, yrow0, par, b_blk = blk(0)
        prelude_elementwise(proj_sc, prow0, par, background)
        prelude_matmuls(yrow0, par, b_blk)
        staged = stage_head(proj_sc, prow0, par, b_blk, 0, 0)
        pending = None
        for j, (i, hd) in enumerate(heads):
            prow0, yrow0, par, b_blk = blk(i)
            if hd == 1 and i + 1 < blocks_per_tile:
                prelude_elementwise(proj_sc, blk(i + 1)[0], blk(i + 1)[2], background)
            if hd == 2 and i + 1 < blocks_per_tile:
                prelude_matmuls(blk(i + 1)[1], blk(i + 1)[2], blk(i + 1)[3])
            staged_next = None
            if j + 1 < len(heads):
                ni, nhd = heads[j + 1]
                staged_next = stage_head(proj_sc, blk(ni)[0], blk(ni)[2], blk(ni)[3], nhd, (j + 1) % 2)
            attn = head_levels(j % 2, background)
            if pending is not None:
                pending()
            pending = functools.partial(head_tail, proj_sc, prow0, yrow0, b_blk, hd, attn, *staged)
            background()
            staged = staged_next
        pending()
        background.flush()

    mix_tile(proj_a, 0, projection_tasks(h_b, proj_b))
    ada_next = ada_ref[pl.ds(batch_next, 1), :]
    h_a[...] = _modulated_input(xn_ref[pl.ds(0, MIX_TILE), :], ada_next, n1w_ref)
    mix_tile(proj_b, 1, finish_tasks(0) + projection_tasks(h_a, proj_a))
    h_b[...] = _modulated_input(xn_ref[pl.ds(MIX_TILE, MIX_TILE), :], ada_next, n1w_ref)
    for task in finish_tasks(1):
        task()


def _mixer_call(x, ada, n1w, w_in, w_s, b_s, lnw, lnb, lbraw, gnw, w_out, tri, w1, w2, guarded):
    bsz, seq, _ = x.shape
    step_rows = 2 * MIX_TILE
    steps_per_seq = seq // step_rows
    n_steps = bsz * steps_per_seq
    x2 = x.reshape(bsz * seq, D_MODEL)
    const2 = lambda g: (0, 0)
    const3 = lambda g: (0, 0, 0)
    resident = dict(pipeline_mode=pl.Buffered(1))
    w1_rows = _cast_rows(w1.shape[0], n_steps)
    w2_rows = _cast_rows(w2.shape[0], n_steps)
    w1_map = lambda g: (jnp.minimum(g, w1.shape[0] // w1_rows - 1), 0)
    w2_map = lambda g: (jnp.minimum(g, w2.shape[0] // w2_rows - 1), 0)
    if guarded:
        lvl = jnp.asarray(_level_table(GUARD_LEVELS, GUARD_BLOCK))
    else:
        lvl = jnp.asarray(_level_table(LEVELS))
    kernel = functools.partial(_mixer_kernel, steps_per_seq=steps_per_seq, n_steps=n_steps,
                               guarded=guarded)
    out, w1_bf, w2_bf = pl.pallas_call(
        kernel,
        grid=(n_steps,),
        in_specs=[
            pl.BlockSpec((step_rows, D_MODEL), lambda g: (g, 0)),
            pl.BlockSpec((step_rows, D_MODEL), lambda g: (jnp.minimum(g + 1, n_steps - 1), 0)),
            pl.BlockSpec((bsz, N_ADA * D_MODEL), const2),
            pl.BlockSpec((1, D_MODEL), const2),
            pl.BlockSpec((D_MODEL, D_IN), const2, **resident),
            pl.BlockSpec((HEADS, BLOCK, BLOCK), const3),
            pl.BlockSpec((HEADS, BLOCK, 1), const3),
            pl.BlockSpec((1, D_GMLP), const2),
            pl.BlockSpec((1, D_GMLP), const2),
            pl.BlockSpec((2, D_HGRN), const2),
            pl.BlockSpec((1, HEAD_DIM), const2),
            pl.BlockSpec((D_MODEL, D_MODEL), const2, **resident),
            pl.BlockSpec((BLOCK, 2 * BLOCK), const2),
            pl.BlockSpec((BLOCK, BLOCK), const2),
            pl.BlockSpec((w1_rows, w1.shape[1]), w1_map),
            pl.BlockSpec((w2_rows, w2.shape[1]), w2_map),
        ],
        out_specs=[
            pl.BlockSpec((step_rows, D_MODEL), lambda g: (g, 0)),
            pl.BlockSpec((w1_rows, w1.shape[1]), w1_map),
            pl.BlockSpec((w2_rows, w2.shape[1]), w2_map),
        ],
        out_shape=[
            jax.ShapeDtypeStruct(x2.shape, _F32),
            jax.ShapeDtypeStruct(w1.shape, _BF16),
            jax.ShapeDtypeStruct(w2.shape, _BF16),
        ],
        scratch_shapes=[
            pltpu.VMEM((HEADS, HEAD_DIM, HEAD_DIM), _F32),
            pltpu.VMEM((MIX_TILE, D_IN), _F32),
            pltpu.VMEM((MIX_TILE, D_IN), _F32),
            pltpu.VMEM((MIX_TILE, D_MODEL), _BF16),
            pltpu.VMEM((MIX_TILE, D_MODEL), _BF16),
            pltpu.VMEM((step_rows, D_MODEL), _BF16),
            pltpu.VMEM((step_rows // BLOCK, BLOCK, D_HGRN), _F32),
            pltpu.VMEM((HEADS, BLOCK, BLOCK), _BF16),
            pltpu.VMEM((D_MODEL, D_IN), _BF16),
            pltpu.VMEM((D_MODEL, D_MODEL), _BF16),
            pltpu.VMEM((2, 3, BLOCK, D_HGRN), _BF16),
            pltpu.VMEM((2, 2, BLOCK, D_HGRN), _F32),
            pltpu.VMEM((2, len(LEVELS), BLOCK, HEAD_DIM), _BF16),
            pltpu.VMEM((2, len(LEVELS), HEAD_DIM, BLOCK), _BF16),
        ],
        compiler_params=pltpu.CompilerParams(
            dimension_semantics=("arbitrary",),
            vmem_limit_bytes=VMEM_LIMIT),
        name="mixer",
    )(x2, x2, ada, n1w, w_in, w_s, b_s, lnw, lnb, lbraw, gnw, w_out, tri, lvl, w1, w2)
    return out.reshape(x.shape), w1_bf, w2_bf


def _ffn_kernel(x_ref, ada_ref, n2w_ref, w1_ref, w2_ref, fnw_ref, o_ref):
    ada = ada_ref[pl.ds(pl.program_id(0), 1), :]
    sh2, sc2, g2 = (ada[:, k * D_MODEL:(k + 1) * D_MODEL] for k in (3, 4, 5))

    def hidden(p):
        x = x_ref[0, pl.ds(p * FFN_PART, FFN_PART), :]
        ms = jnp.mean(x * x, axis=-1, keepdims=True)
        hmod = x * lax.rsqrt(ms + EPS) * n2w_ref[...]
        hmod = (hmod * (1.0 + sc2) + sh2).astype(_BF16)
        acts = []
        for c in range(D_FF // MXU_CHUNK):
            gate = _dot(hmod, w1_ref[:, pl.ds(c * MXU_CHUNK, MXU_CHUNK)])
            up = _dot(hmod, w1_ref[:, pl.ds(D_FF + c * MXU_CHUNK, MXU_CHUNK)])
            acts.append((_silu(gate) * up).astype(_BF16))
        return jnp.concatenate(acts, axis=1)

    def finish(p, act):
        rows = pl.ds(p * FFN_PART, FFN_PART)
        x2 = x_ref[0, rows, :] + g2 * _dot(act, w2_ref[...])
        ms2 = jnp.mean(x2 * x2, axis=-1, keepdims=True)
        o_ref[0, rows, :] = x2 * lax.rsqrt(ms2 + EPS) * fnw_ref[...]

    act = hidden(0)
    for p in range(1, FFN_PARTS):
        act_next = hidden(p)
        finish(p - 1, act)
        act = act_next
    finish(FFN_PARTS - 1, act)


def _ffn_call(x, ada, n2w, w1, w2, fnw):
    bsz, seq, _ = x.shape
    tile = FFN_PART * FFN_PARTS
    const2 = lambda b, s: (0, 0)
    resident = dict(pipeline_mode=pl.Buffered(1))
    return pl.pallas_call(
        _ffn_kernel,
        grid=(bsz, seq // tile),
        in_specs=[
            pl.BlockSpec((1, tile, D_MODEL), lambda b, s: (b, s, 0)),
            pl.BlockSpec((bsz, N_ADA * D_MODEL), const2),
            pl.BlockSpec((1, D_MODEL), const2),
            pl.BlockSpec((D_MODEL, 2 * D_FF), const2, **resident),
            pl.BlockSpec((D_FF, D_MODEL), const2, **resident),
            pl.BlockSpec((1, D_MODEL), const2),
        ],
        out_specs=pl.BlockSpec((1, tile, D_MODEL), lambda b, s: (b, s, 0)),
        out_shape=jax.ShapeDtypeStruct(x.shape, _F32),
        compiler_params=pltpu.CompilerParams(
            dimension_semantics=("arbitrary", "arbitrary"),
            vmem_limit_bytes=VMEM_LIMIT),
        name="ffn",
    )(x, ada, n2w, w1, w2, fnw)


def kernel(x, c, w_ada, b_ada, norm1_w, w_in, w_s, b_s, v_ln_w, v_ln_b, lower_bounds,
           gn_w, w_out, norm2_w, w_ffn_in, w_ffn_out, final_norm_w):
    bsz = x.shape[0]
    depth = w_in.shape[0]
    assert depth == 1 and lower_bounds.shape[0] == 2

    tri = jnp.asarray(np.tile(np.tril(np.ones((BLOCK, BLOCK), np.float32)), (1, 2)), _BF16)

    for l in range(depth):
        ada, log2_bound = _ada_call(c, w_ada[l], b_ada[l][None, :], lower_bounds)
        mixer = functools.partial(
            _mixer_call, x, ada, norm1_w[l][None, :], w_in[l], w_s[l],
            b_s[l][:, :, None], v_ln_w[l][None, :], v_ln_b[l][None, :],
            lower_bounds, gn_w[l][None, :], w_out[l], tri, w_ffn_in[l], w_ffn_out[l])
        x, w1_bf, w2_bf = lax.cond(log2_bound[0, 0] <= GUARD_LOG2_RANGE,
                                   lambda: mixer(guarded=True), lambda: mixer(guarded=False))
        x = _ffn_call(x, ada, norm2_w[l][None, :], w1_bf, w2_bf, final_norm_w[None, :])
    return x
```

```python
import functools

import numpy as np
import jax
import jax.numpy as jnp
from jax import lax
from jax.experimental import pallas as pl
from jax.experimental.pallas import tpu as pltpu

D_MODEL = 1024
D_GMLP = 512
D_HGRN = 512
HEADS = 4
HEAD_DIM = 128
BLOCK = 128
GATE_CHUNK = 64
D_IN = 2 * D_GMLP + 4 * D_HGRN
D_FF = 2816
N_ADA = 6
EPS = 1e-6

LEVELS = (128, 64, 32, 16, 8, 4, 2)
GUARD_LEVELS = (128, 64)
GUARD_BLOCK = 32
GUARD_LOG2_RANGE = 100.0

MIX_TILE = 256
MXU_CHUNK = 256
FFN_PART = 256
FFN_PARTS = 4
VMEM_LIMIT = 56 * 1024 * 1024
FFN_TEMP_BYTES = 8 * 1024 * 1024

SUBLANES = 8

_F32 = jnp.float32
_BF16 = jnp.bfloat16


def _dot(a, b):
    return jnp.dot(a, b, preferred_element_type=_F32)


def _silu(x):
    return x * jax.nn.sigmoid(x)


def _gelu(x):
    half = 0.5 * x
    return half + half * lax.erf(x * np.float32(1.0 / np.sqrt(2.0)))


def _level_table(levels, guard_block=None):
    t = np.arange(BLOCK)[:, None]
    s = np.arange(BLOCK)[None, :]
    table = np.full((BLOCK, BLOCK), -1, np.int32)
    if guard_block:
        table[((t // guard_block) == (s // guard_block)) & (s <= t)] = len(levels)
    for idx, n in enumerate(levels):
        h = n // 2
        same = (t // n) == (s // n)
        table[same & ((t % n) >= h) & ((s % n) < h)] = idx
    return table


ADA_ROWS = 256


def _ada_kernel(c_ref, w_ref, b_ref, lbraw_ref, o_ref, bound_ref):
    @pl.when(pl.program_id(0) == 0)
    def _():
        o_ref[...] = jnp.broadcast_to(b_ref[...], o_ref.shape)
        lbraw = lbraw_ref[...]
        lbe = jnp.exp(lbraw - jnp.max(lbraw, axis=0, keepdims=True))
        lb = (lbe / jnp.sum(lbe, axis=0, keepdims=True))[0:1]
        worst = jnp.max(-jnp.log2(lb), axis=1, keepdims=True) * (GUARD_BLOCK // 2)
        bound_ref[...] = jnp.broadcast_to(worst, bound_ref.shape)

    c_act = _silu(c_ref[...]).astype(_BF16)
    o_ref[...] += _dot(c_act, w_ref[...].astype(_BF16))


def _ada_call(c, w_ada, b_ada, lower_bounds):
    bsz = c.shape[0]
    n_out = w_ada.shape[1]
    const2 = lambda k: (0, 0)
    return pl.pallas_call(
        _ada_kernel,
        grid=(D_MODEL // ADA_ROWS,),
        in_specs=[
            pl.BlockSpec((bsz, ADA_ROWS), lambda k: (0, k)),
            pl.BlockSpec((ADA_ROWS, n_out), lambda k: (k, 0)),
            pl.BlockSpec((1, n_out), const2),
            pl.BlockSpec(lower_bounds.shape, const2),
        ],
        out_specs=[
            pl.BlockSpec((bsz, n_out), const2),
            pl.BlockSpec((SUBLANES, HEAD_DIM), const2),
        ],
        out_shape=[
            jax.ShapeDtypeStruct((bsz, n_out), _F32),
            jax.ShapeDtypeStruct((SUBLANES, HEAD_DIM), _F32),
        ],
        compiler_params=pltpu.CompilerParams(dimension_semantics=("arbitrary",)),
        name="ada",
    )(c, w_ada, b_ada, lower_bounds)


def _neg_abs(x):
    bits = pltpu.bitcast(x, jnp.uint32) | jnp.uint32(0x80000000)
    return pltpu.bitcast(bits, _F32)


def _half_select(level_n, q, k, upper):
    h = level_n // 2
    if h % SUBLANES != 0:
        return jnp.where(upper, q, k)
    pieces = []
    for r0 in range(0, BLOCK, level_n):
        pieces += [k[r0:r0 + h], q[r0 + h:r0 + level_n]]
    return jnp.concatenate(pieces, axis=0)


def _decay_exponent(level_n, b_sc, row, col0):
    h = level_n // 2

    def ref_rows(r, n_rows):
        return jnp.broadcast_to(b_sc[pl.ds(r, 1), pl.ds(col0, HEAD_DIM)], (n_rows, HEAD_DIM))

    if level_n < SUBLANES:
        lower_block = (row[:SUBLANES] % SUBLANES) < level_n
        pieces = [jnp.where(lower_block, ref_rows(r0 + h - 1, SUBLANES),
                            ref_rows(r0 + level_n + h - 1, SUBLANES))
                  for r0 in range(0, BLOCK, SUBLANES)]
    else:
        pieces = [ref_rows(r0 + h - 1, level_n) for r0 in range(0, BLOCK, level_n)]
    b_ref = pieces[0] if len(pieces) == 1 else jnp.concatenate(pieces, axis=0)
    b_h = b_sc[:, pl.ds(col0, HEAD_DIM)]
    return b_h - b_ref


def _decay_factor(level_n, b_sc, row, col0):
    return jnp.exp2(_neg_abs(_decay_exponent(level_n, b_sc, row, col0)))


def _modulated_input(x, ada_b, n1w_ref):
    sh1, sc1 = ada_b[:, 0:D_MODEL], ada_b[:, D_MODEL:2 * D_MODEL]
    ms = jnp.mean(x * x, axis=-1, keepdims=True)
    hmod = x * lax.rsqrt(ms + EPS) * n1w_ref[...]
    return (hmod * (1.0 + sc1) + sh1).astype(_BF16)


BF16_ROWS = 16


def _cast_rows(n_rows, n_steps):
    rows = -(-n_rows // n_steps)
    rows = -(-rows // BF16_ROWS) * BF16_ROWS
    while n_rows % rows:
        rows += BF16_ROWS
    return rows


class _Interleaver:
    def __init__(self, tasks, total_weight):
        self._tasks = list(tasks)
        self._total = total_weight
        self._weight = 0
        self._done = 0

    def __call__(self, weight=1):
        self._weight += weight
        due = min(len(self._tasks), self._weight * len(self._tasks) // self._total)
        while self._done < due:
            self._tasks[self._done]()
            self._done += 1

    def flush(self):
        assert self._weight == self._total, (self._weight, self._total)
        while self._done < len(self._tasks):
            self._tasks[self._done]()
            self._done += 1


PRELUDE_WEIGHT = 2
HEAD_LEVELS_WEIGHT = 3
WEIGHT_PER_BLOCK = 2 * PRELUDE_WEIGHT + HEADS * (HEAD_LEVELS_WEIGHT + 1)


def _mixer_kernel(x_ref, xn_ref, ada_ref, n1w_ref, w_in_f32, ws_ref, bs_ref, lnw_ref, lnb_ref,
                  lbraw_ref, gnw_ref, w_out_f32, tri_ref, lvl_ref, w1_f32, w2_f32,
                  o_ref, w1_bf, w2_bf,
                  state_sc, proj_a, proj_b, h_a, h_b, y_sc, b_sc, wsm_sc, w_in_ref, w_out_ref,
                  pre_bf, pre_f32, zl_sc, zt_sc,
                  *, steps_per_seq, n_steps, guarded):
    g = pl.program_id(0)
    batch = g // steps_per_seq
    batch_next = jnp.minimum(g + 1, n_steps - 1) // steps_per_seq
    ada_b = ada_ref[pl.ds(batch, 1), :]

    w1_bf[...] = w1_f32[...].astype(_BF16)
    w2_bf[...] = w2_f32[...].astype(_BF16)

    @pl.when(g == 0)
    def _():
        w_in_ref[...] = w_in_f32[...].astype(_BF16)
        w_out_ref[...] = w_out_f32[...].astype(_BF16)
        h0 = _modulated_input(x_ref[pl.ds(0, MIX_TILE), :], ada_b, n1w_ref)
        proj_a[...] = _dot(h0, w_in_ref[...])
        h_b[...] = _modulated_input(x_ref[pl.ds(MIX_TILE, MIX_TILE), :], ada_b, n1w_ref)
        col = lax.broadcasted_iota(jnp.int32, (BLOCK, BLOCK), 1)
        rowb = lax.broadcasted_iota(jnp.int32, (BLOCK, BLOCK), 0)
        gate_mask = (rowb // GATE_CHUNK) >= (col // GATE_CHUNK)
        for hd in range(HEADS):
            wsm_sc[hd] = jnp.where(gate_mask, ws_ref[hd], 0.0).astype(_BF16)

    @pl.when(g % steps_per_seq == 0)
    def _():
        state_sc[...] = jnp.zeros_like(state_sc)

    lbraw = lbraw_ref[...]
    lbe = jnp.exp(lbraw - jnp.max(lbraw, axis=0, keepdims=True))
    lb = (lbe / jnp.sum(lbe, axis=0, keepdims=True))[0:1]

    levels = GUARD_LEVELS if guarded else LEVELS
    n_products = len(levels) + (1 if guarded else 0)
    row = lax.broadcasted_iota(jnp.int32, (BLOCK, HEAD_DIM), 0)
    lvl = lvl_ref[...]
    upper_rows = [(row % n) >= (n // 2) for n in levels]

    o0 = 2 * D_GMLP

    def prelude_elementwise(proj_sc, prow0, par, background):
        rows = pl.ds(prow0, BLOCK)
        u = proj_sc[rows, pl.ds(0, D_GMLP)]
        v = proj_sc[rows, pl.ds(D_GMLP, D_GMLP)]
        gv = _gelu(v)
        mu = jnp.mean(gv, axis=-1, keepdims=True)
        cen = gv - mu
        var = jnp.mean(cen * cen, axis=-1, keepdims=True)
        vn = cen * lax.rsqrt(var + EPS) * lnw_ref[...] + lnb_ref[...]
        pre_bf[par, 0] = vn.astype(_BF16)
        pre_f32[par, 0] = _gelu(u)
        background(PRELUDE_WEIGHT)
        fl = proj_sc[rows, pl.ds(o0 + D_HGRN, D_HGRN)]
        f = lb + (1.0 - lb) * jax.nn.sigmoid(fl)
        logf = jnp.log2(f)
        p_hi = logf.astype(_BF16)
        pre_bf[par, 1] = p_hi
        pre_bf[par, 2] = (logf - p_hi.astype(_F32)).astype(_BF16)
        pre_f32[par, 1] = f
        background(PRELUDE_WEIGHT)

    def prelude_matmuls(yrow0, par, b_blk):
        b_blk[...] = _dot(tri_ref[...], jnp.concatenate([pre_bf[par, 1], pre_bf[par, 2]], axis=0))
        mixed = []
        for hd in range(HEADS):
            mixed.append(_dot(wsm_sc[hd], pre_bf[par, 0, :, pl.ds(hd * HEAD_DIM, HEAD_DIM)]) + bs_ref[hd])
        y_a = pre_f32[par, 0] * jnp.concatenate(mixed, axis=-1)
        y_sc[pl.ds(yrow0, BLOCK), pl.ds(0, D_GMLP)] = y_a.astype(_BF16)

    def level_slabs(n):
        h = n // 2
        if h % SUBLANES or n == 1:
            return list(range(0, BLOCK, SUBLANES))
        return [r for r0 in range(0, BLOCK, n) for r in range(r0 + h, r0 + n, SUBLANES)]

    def stage_head(proj_sc, prow0, par, b_blk, hd, slot):
        rows = pl.ds(prow0, BLOCK)
        c0 = hd * HEAD_DIM
        q_h = _silu(proj_sc[rows, pl.ds(o0 + c0, HEAD_DIM)])
        f_h = pre_f32[par, 1, :, pl.ds(c0, HEAD_DIM)]
        k_h = 1.0 - f_h
        for idx, n in enumerate(levels):
            if n == 2:
                z = jnp.where(upper_rows[idx], q_h * f_h, k_h)
            else:
                z = _half_select(n, q_h, k_h, upper_rows[idx]) * _decay_factor(n, b_blk, row, c0)
            slabs = level_slabs(n)
            lhs = jnp.concatenate([z[r:r + SUBLANES] for r in slabs], axis=0)
            zl_sc[slot, idx, pl.ds(0, len(slabs) * SUBLANES), :] = lhs.astype(_BF16)
            zt_sc[slot, idx] = z.T.astype(_BF16)
        if guarded:
            d = _decay_exponent(GUARD_BLOCK, b_blk, row, c0)
            zl_sc[slot, len(levels)] = (q_h * jnp.exp2(d)).astype(_BF16)
            zt_sc[slot, len(levels)] = (k_h * jnp.exp2(-d)).T.astype(_BF16)
        return q_h, k_h

    def head_levels(slot, background):
        attn = [jnp.zeros((SUBLANES, BLOCK), _F32)] * (BLOCK // SUBLANES)
        for idx in range(n_products):
            slabs = level_slabs(levels[idx]) if idx < len(levels) else level_slabs(1)
            res = _dot(zl_sc[slot, idx, pl.ds(0, len(slabs) * SUBLANES), :], zt_sc[slot, idx])
            for j, r in enumerate(slabs):
                valid = lvl[r:r + SUBLANES] == idx
                attn[r // SUBLANES] = jnp.where(valid, res[j * SUBLANES:(j + 1) * SUBLANES],
                                                attn[r // SUBLANES])
        background(HEAD_LEVELS_WEIGHT)
        return jnp.concatenate(attn, axis=0)

    def head_tail(proj_sc, prow0, yrow0, b_blk, hd, attn, q_h, k_h):
        rows = pl.ds(prow0, BLOCK)
        c0 = hd * HEAD_DIM
        v_h = proj_sc[rows, pl.ds(o0 + 2 * D_HGRN + c0, HEAD_DIM)]
        g_h = proj_sc[rows, pl.ds(o0 + 3 * D_HGRN + c0, HEAD_DIM)]
        b_h = b_blk[:, pl.ds(c0, HEAD_DIM)]

        s_t = state_sc[hd]
        q_hat = (q_h * jnp.exp2(b_h)).astype(_BF16)
        v_t = v_h.T.astype(_BF16)
        lhs = jnp.concatenate([attn.astype(_BF16), q_hat], axis=1)
        rhs = jnp.concatenate([v_h.astype(_BF16), s_t.T.astype(_BF16)], axis=0)
        o_h = _dot(lhs, rhs)
        if not guarded:
            o_h = o_h + jnp.sum(q_h * k_h, axis=-1, keepdims=True) * v_h

        b_last = b_blk[pl.ds(BLOCK - 1, 1), pl.ds(c0, HEAD_DIM)]
        k_hat = (k_h * jnp.exp2(b_last - b_h)).astype(_BF16)
        state_sc[hd] = s_t * jnp.exp2(b_last) + _dot(v_t, k_hat)

        oms = jnp.mean(o_h * o_h, axis=-1, keepdims=True)
        y_b = o_h * lax.rsqrt(oms + EPS) * gnw_ref[...] * _silu(g_h)
        y_sc[pl.ds(yrow0, BLOCK), pl.ds(D_GMLP + c0, HEAD_DIM)] = y_b.astype(_BF16)

    blocks_per_tile = MIX_TILE // BLOCK
    g1 = ada_b[:, 2 * D_MODEL:3 * D_MODEL]

    def projection_tasks(h_sc, dst):
        def chunk(c):
            cols = pl.ds(c * MXU_CHUNK, MXU_CHUNK)
            dst[:, cols] = _dot(h_sc[...], w_in_ref[:, cols])
        return [functools.partial(chunk, c) for c in range(D_IN // MXU_CHUNK)]

    def finish_tasks(t):
        trow = pl.ds(t * MIX_TILE, MIX_TILE)

        def chunk(c):
            cols = pl.ds(c * MXU_CHUNK, MXU_CHUNK)
            mix = _dot(y_sc[trow, :], w_out_ref[:, cols])
            o_ref[trow, cols] = x_ref[trow, cols] + g1[:, c * MXU_CHUNK:(c + 1) * MXU_CHUNK] * mix
        return [functools.partial(chunk, c) for c in range(D_MODEL // MXU_CHUNK)]

    def mix_tile(proj_sc, t, tasks):
        background = _Interleaver(tasks, blocks_per_tile * WEIGHT_PER_BLOCK)
        heads = [(i, hd) for i in range(blocks_per_tile) for hd in range(HEADS)]

        def blk(i):
            return i * BLOCK, t * MIX_TILE + i * BLOCK, i % 2, b_sc.at[t * blocks_per_tile + i]

        prow0, yrow0, par, b_blk = blk(0)
        prelude_elementwise(proj_sc, prow0, par, background)
        prelude_matmuls(yrow0, par, b_blk)
        staged = stage_head(proj_sc, prow0, par, b_blk, 0, 0)
        pending = None
        for j, (i, hd) in enumerate(heads):
            prow0, yrow0, par, b_blk = blk(i)
            if hd == 1 and i + 1 < blocks_per_tile:
                prelude_elementwise(proj_sc, blk(i + 1)[0], blk(i + 1)[2], background)
            if hd == 2 and i + 1 < blocks_per_tile:
                prelude_matmuls(blk(i + 1)[1], blk(i + 1)[2], blk(i + 1)[3])
            staged_next = None
            if j + 1 < len(heads):
                ni, nhd = heads[j + 1]
                staged_next = stage_head(proj_sc, blk(ni)[0], blk(ni)[2], blk(ni)[3], nhd, (j + 1) % 2)
            attn = head_levels(j % 2, background)
            if pending is not None:
                pending()
            pending = functools.partial(head_tail, proj_sc, prow0, yrow0, b_blk, hd, attn, *staged)
            background()
            staged = staged_next
        pending()
        background.flush()

    mix_tile(proj_a, 0, projection_tasks(h_b, proj_b))
    ada_next = ada_ref[pl.ds(batch_next, 1), :]
    h_a[...] = _modulated_input(xn_ref[pl.ds(0, MIX_TILE), :], ada_next, n1w_ref)
    mix_tile(proj_b, 1, finish_tasks(0) + projection_tasks(h_a, proj_a))
    h_b[...] = _modulated_input(xn_ref[pl.ds(MIX_TILE, MIX_TILE), :], ada_next, n1w_ref)
    for task in finish_tasks(1):
        task()


def _mixer_call(x, ada, n1w, w_in, w_s, b_s, lnw, lnb, lbraw, gnw, w_out, tri, w1, w2, guarded):
    bsz, seq, _ = x.shape
    step_rows = 2 * MIX_TILE
    steps_per_seq = seq // step_rows
    n_steps = bsz * steps_per_seq
    x2 = x.reshape(bsz * seq, D_MODEL)
    const2 = lambda g: (0, 0)
    const3 = lambda g: (0, 0, 0)
    resident = dict(pipeline_mode=pl.Buffered(1))
    w1_rows = _cast_rows(w1.shape[0], n_steps)
    w2_rows = _cast_rows(w2.shape[0], n_steps)
    w1_map = lambda g: (jnp.minimum(g, w1.shape[0] // w1_rows - 1), 0)
    w2_map = lambda g: (jnp.minimum(g, w2.shape[0] // w2_rows - 1), 0)
    if guarded:
        lvl = jnp.asarray(_level_table(GUARD_LEVELS, GUARD_BLOCK))
    else:
        lvl = jnp.asarray(_level_table(LEVELS))
    kernel = functools.partial(_mixer_kernel, steps_per_seq=steps_per_seq, n_steps=n_steps,
                               guarded=guarded)
    out, w1_bf, w2_bf = pl.pallas_call(
        kernel,
        grid=(n_steps,),
        in_specs=[
            pl.BlockSpec((step_rows, D_MODEL), lambda g: (g, 0)),
            pl.BlockSpec((step_rows, D_MODEL), lambda g: (jnp.minimum(g + 1, n_steps - 1), 0)),
            pl.BlockSpec((bsz, N_ADA * D_MODEL), const2),
            pl.BlockSpec((1, D_MODEL), const2),
            pl.BlockSpec((D_MODEL, D_IN), const2, **resident),
            pl.BlockSpec((HEADS, BLOCK, BLOCK), const3),
            pl.BlockSpec((HEADS, BLOCK, 1), const3),
            pl.BlockSpec((1, D_GMLP), const2),
            pl.BlockSpec((1, D_GMLP), const2),
            pl.BlockSpec((2, D_HGRN), const2),
            pl.BlockSpec((1, HEAD_DIM), const2),
            pl.BlockSpec((D_MODEL, D_MODEL), const2, **resident),
            pl.BlockSpec((BLOCK, 2 * BLOCK), const2),
            pl.BlockSpec((BLOCK, BLOCK), const2),
            pl.BlockSpec((w1_rows, w1.shape[1]), w1_map),
            pl.BlockSpec((w2_rows, w2.shape[1]), w2_map),
        ],
        out_specs=[
            pl.BlockSpec((step_rows, D_MODEL), lambda g: (g, 0)),
            pl.BlockSpec((w1_rows, w1.shape[1]), w1_map),
            pl.BlockSpec((w2_rows, w2.shape[1]), w2_map),
        ],
        out_shape=[
            jax.ShapeDtypeStruct(x2.shape, _F32),
            jax.ShapeDtypeStruct(w1.shape, _BF16),
            jax.ShapeDtypeStruct(w2.shape, _BF16),
        ],
        scratch_shapes=[
            pltpu.VMEM((HEADS, HEAD_DIM, HEAD_DIM), _F32),
            pltpu.VMEM((MIX_TILE, D_IN), _F32),
            pltpu.VMEM((MIX_TILE, D_IN), _F32),
            pltpu.VMEM((MIX_TILE, D_MODEL), _BF16),
            pltpu.VMEM((MIX_TILE, D_MODEL), _BF16),
            pltpu.VMEM((step_rows, D_MODEL), _BF16),
            pltpu.VMEM((step_rows // BLOCK, BLOCK, D_HGRN), _F32),
            pltpu.VMEM((HEADS, BLOCK, BLOCK), _BF16),
            pltpu.VMEM((D_MODEL, D_IN), _BF16),
            pltpu.VMEM((D_MODEL, D_MODEL), _BF16),
            pltpu.VMEM((2, 3, BLOCK, D_HGRN), _BF16),
            pltpu.VMEM((2, 2, BLOCK, D_HGRN), _F32),
            pltpu.VMEM((2, len(LEVELS), BLOCK, HEAD_DIM), _BF16),
            pltpu.VMEM((2, len(LEVELS), HEAD_DIM, BLOCK), _BF16),
        ],
        compiler_params=pltpu.CompilerParams(
            dimension_semantics=("arbitrary",),
            vmem_limit_bytes=VMEM_LIMIT),
        name="mixer",
    )(x2, x2, ada, n1w, w_in, w_s, b_s, lnw, lnb, lbraw, gnw, w_out, tri, lvl, w1, w2)
    return out.reshape(x.shape), w1_bf, w2_bf


def _ffn_kernel(x_ref, ada_ref, n2w_ref, w1_ref, w2_ref, fnw_ref, o_ref):
    ada = ada_ref[pl.ds(pl.program_id(0), 1), :]
    sh2, sc2, g2 = (ada[:, k * D_MODEL:(k + 1) * D_MODEL] for k in (3, 4, 5))

    def hidden(p):
        x = x_ref[0, pl.ds(p * FFN_PART, FFN_PART), :]
        ms = jnp.mean(x * x, axis=-1, keepdims=True)
        hmod = x * lax.rsqrt(ms + EPS) * n2w_ref[...]
        hmod = (hmod * (1.0 + sc2) + sh2).astype(_BF16)
        acts = []
        for c in range(D_FF // MXU_CHUNK):
            gate = _dot(hmod, w1_ref[:, pl.ds(c * MXU_CHUNK, MXU_CHUNK)])
            up = _dot(hmod, w1_ref[:, pl.ds(D_FF + c * MXU_CHUNK, MXU_CHUNK)])
            acts.append((_silu(gate) * up).astype(_BF16))
        return jnp.concatenate(acts, axis=1)

    def finish(p, act):
        rows = pl.ds(p * FFN_PART, FFN_PART)
        x2 = x_ref[0, rows, :] + g2 * _dot(act, w2_ref[...])
        ms2 = jnp.mean(x2 * x2, axis=-1, keepdims=True)
        o_ref[0, rows, :] = x2 * lax.rsqrt(ms2 + EPS) * fnw_ref[...]

    act = hidden(0)
    for p in range(1, FFN_PARTS):
        act_next = hidden(p)
        finish(p - 1, act)
        act = act_next
    finish(FFN_PARTS - 1, act)


def _ffn_call(x, ada, n2w, w1, w2, fnw):
    bsz, seq, _ = x.shape
    tile = FFN_PART * FFN_PARTS
    const2 = lambda b, s: (0, 0)
    resident = dict(pipeline_mode=pl.Buffered(1))
    buffer_bytes = 2 * 2 * tile * D_MODEL * 4 + (w1.size + w2.size) * 2
    return pl.pallas_call(
        _ffn_kernel,
        grid=(bsz, seq // tile),
        in_specs=[
            pl.BlockSpec((1, tile, D_MODEL), lambda b, s: (b, s, 0)),
            pl.BlockSpec((bsz, N_ADA * D_MODEL), const2),
            pl.BlockSpec((1, D_MODEL), const2),
            pl.BlockSpec((D_MODEL, 2 * D_FF), const2, **resident),
            pl.BlockSpec((D_FF, D_MODEL), const2, **resident),
            pl.BlockSpec((1, D_MODEL), const2),
        ],
        out_specs=pl.BlockSpec((1, tile, D_MODEL), lambda b, s: (b, s, 0)),
        out_shape=jax.ShapeDtypeStruct(x.shape, _F32),
        compiler_params=pltpu.CompilerParams(
            dimension_semantics=("arbitrary", "arbitrary"),
            vmem_limit_bytes=min(VMEM_LIMIT, buffer_bytes + FFN_TEMP_BYTES)),
        name="ffn",
    )(x, ada, n2w, w1, w2, fnw)


def kernel(x, c, w_ada, b_ada, norm1_w, w_in, w_s, b_s, v_ln_w, v_ln_b, lower_bounds,
           gn_w, w_out, norm2_w, w_ffn_in, w_ffn_out, final_norm_w):
    bsz = x.shape[0]
    depth = w_in.shape[0]
    assert depth == 1 and lower_bounds.shape[0] == 2

    tri = jnp.asarray(np.tile(np.tril(np.ones((BLOCK, BLOCK), np.float32)), (1, 2)), _BF16)

    for l in range(depth):
        ada, log2_bound = _ada_call(c, w_ada[l], b_ada[l][None, :], lower_bounds)
        mixer = functools.partial(
            _mixer_call, x, ada, norm1_w[l][None, :], w_in[l], w_s[l],
            b_s[l][:, :, None], v_ln_w[l][None, :], v_ln_b[l][None, :],
            lower_bounds, gn_w[l][None, :], w_out[l], tri, w_ffn_in[l], w_ffn_out[l])
        x, w1_bf, w2_bf = lax.cond(log2_bound[0, 0] <= GUARD_LOG2_RANGE,
                                   lambda: mixer(guarded=True), lambda: mixer(guarded=False))
        x = _ffn_call(x, ada, norm2_w[l][None, :], w1_bf, w2_bf, final_norm_w[None, :])
    return x
```
